```python
import jax, jax.numpy as jnp
from jax import lax
import numpy as np

D_MODEL = 1024
BATCH = 4
SEQ = 8192
DEPTH = 1

GRID_W = 64
CTX_LEN = 256
GLA_HEADS = 4
GLA_DK = 64
GLA_DV = 128
GLA_WIDTH = GLA_HEADS * GLA_DV
GATE_RANK = 16
GATE_NORMALIZER = 16.0
CHUNK = 64
POOL_WINDOWS = (2, 4, 8, 16)
POOL_GROUP = 128
POOL_WIDTH = POOL_GROUP * len(POOL_WINDOWS)
MIX_WIDTH = GLA_WIDTH + POOL_WIDTH
Q_COLS = GLA_HEADS * GLA_DK
K_COLS = GLA_HEADS * GLA_DK
V_COLS = GLA_WIDTH
G_COLS = GLA_WIDTH
A_COLS = 2 * GATE_RANK
P_COLS = POOL_WIDTH
OFF_K = Q_COLS
OFF_V = OFF_K + K_COLS
OFF_G = OFF_V + V_COLS
OFF_A = OFF_G + G_COLS
OFF_P = OFF_A + A_COLS
IN_COLS = OFF_P + P_COLS
N_GROUPS = 4
EXPERTS_PER_GROUP = 8
N_EXPERTS = N_GROUPS * EXPERTS_PER_GROUP
TOP_K_INNER = 2
D_EXPERT = 256
EPS = 1e-6

kernel_name = "hybrid_gla_pool_hmoe_prefix_dit"


def rmsnorm(x, g):
    xf = x.astype(jnp.float32)
    y = xf * lax.rsqrt(jnp.mean(xf * xf, axis=-1, keepdims=True) + EPS)
    return (y * g).astype(x.dtype)


def modulate(h, shift, scale):
    return h * (1.0 + scale) + shift


def flip(t):
    return jnp.flip(t, axis=1)


def split_heads_qkv(q, k, v):
    B, L, _ = q.shape
    q = q.reshape(B, L, GLA_HEADS, GLA_DK) * (GLA_DK ** -0.5)
    k = k.reshape(B, L, GLA_HEADS, GLA_DK)
    v = v.reshape(B, L, GLA_HEADS, GLA_DV)
    return q, k, v


def log_decay(a_low, w_dec, b_dec):
    B, L, _ = a_low.shape
    a = a_low.reshape(B, L, 2, GATE_RANK).astype(jnp.float32)
    la = jax.nn.log_sigmoid(jnp.einsum('bldr,drk->bldk', a, w_dec) + b_dec) / GATE_NORMALIZER
    la = la.reshape(B, L, 2, GLA_HEADS, GLA_DK)
    return la[:, :, 0], la[:, :, 1]


def gla_state(k, v, la, s0):
    B, L, H, dk = k.shape
    dv = v.shape[-1]
    N = L // CHUNK
    kc = k.reshape(B, N, CHUNK, H, dk).astype(jnp.float32)
    vc = v.reshape(B, N, CHUNK, H, dv)
    b = jnp.cumsum(la.reshape(B, N, CHUNK, H, dk), axis=2)
    b_last = b[:, :, -1]
    k_st = kc * jnp.exp(b_last[:, :, None] - b)
    d_state = jnp.einsum('bnchd,bnche->nbhde', k_st, vc)
    decay = jnp.exp(b_last).transpose(1, 0, 2, 3)

    def step(S, inp):
        d, ds = inp
        return d[..., None] * S + ds, S

    s_fin, s_prev = lax.scan(step, s0, (decay, d_state))
    return s_prev, s_fin, b


def gla_chunked(q, k, v, la, s0):
    s_prev, s_fin, b = gla_state(k, v, la, s0)
    B, L, H, dk = q.shape
    dv = v.shape[-1]
    N = L // CHUNK
    qb = q.reshape(B, N, CHUNK, H, dk).astype(jnp.float32) * jnp.exp(b)
    kb = k.reshape(B, N, CHUNK, H, dk).astype(jnp.float32) * jnp.exp(-b)
    vc = v.reshape(B, N, CHUNK, H, dv)
    mask = jnp.tril(jnp.ones((CHUNK, CHUNK), dtype=bool))
    scores = jnp.where(mask, jnp.einsum('bnihd,bnjhd->bnhij', qb, kb), 0.0)
    o = jnp.einsum('bnhij,bnjhe->bnihe', scores, vc) + jnp.einsum('bnihd,nbhde->bnihe', qb, s_prev)
    return o.reshape(B, L, H, dv), s_fin


def gla_bidirectional(q, k, v, la_f, la_b, s0_f, s0_b):
    o_f, s_f = gla_chunked(q, k, v, la_f, s0_f)
    o_b, s_b = gla_chunked(flip(q), flip(k), flip(v), flip(la_b), s0_b)
    return o_f + flip(o_b), s_f, s_b


def window_mean(x, w, axis):
    n = x.shape[axis]
    lo = w // 2
    hi = w - 1 - lo
    cs = jnp.cumsum(x.astype(jnp.float32), axis=axis)
    pad = [(0, 0)] * x.ndim
    pad[axis] = (1, 0)
    cs = jnp.pad(cs, pad)
    idx = np.arange(n)
    start = np.clip(idx - lo, 0, n)
    end = np.clip(idx + hi + 1, 0, n)
    s = jnp.take(cs, jnp.asarray(end), axis=axis) - jnp.take(cs, jnp.asarray(start), axis=axis)
    cshape = [1] * x.ndim
    cshape[axis] = n
    cnt = jnp.asarray((end - start).astype(np.float32).reshape(cshape))
    return s / cnt


def pool_mixer(p, w_pool, pool_scale, rows):
    B, L, _ = p.shape
    groups = p.reshape(B, L, len(POOL_WINDOWS), POOL_GROUP)
    outs = []
    for gi, w in enumerate(POOL_WINDOWS):
        xg = groups[:, :, gi]
        if rows is None:
            m = window_mean(xg, w, 1)
        else:
            xg2 = xg.reshape(B, rows, GRID_W, POOL_GROUP)
            m = window_mean(window_mean(xg2, w, 2), w, 1).reshape(B, L, POOL_GROUP)
        outs.append(m - xg)
    y = jnp.stack(outs, axis=2)
    y = jnp.einsum('blgc,gcd->blgd', y, w_pool).reshape(B, L, POOL_WIDTH)
    return y * pool_scale


def token_mixer(h, w_in, w_dec, b_dec, gla_norm_g, w_pool, pool_scale, w_out, s0_f, s0_b, rows):
    B, L, _ = h.shape
    proj = h @ w_in
    q, k, v, g, a_low, p = jnp.split(proj, [OFF_K, OFF_V, OFF_G, OFF_A, OFF_P], axis=-1)
    q, k, v = split_heads_qkv(q, k, v)
    la_f, la_b = log_decay(a_low, w_dec, b_dec)
    o, s_f, s_b = gla_bidirectional(q, k, v, la_f, la_b, s0_f, s0_b)
    o = rmsnorm(o, gla_norm_g).reshape(B, L, GLA_WIDTH) * jax.nn.silu(g)
    pooled = pool_mixer(p, w_pool, pool_scale, rows)
    out = jnp.concatenate([o, pooled.astype(o.dtype)], axis=-1) @ w_out
    return out, s_f, s_b


def context_states(h, w_in, w_dec, b_dec):
    B, L, _ = h.shape
    kv = h @ w_in[:, OFF_K:OFF_G]
    a_low = h @ w_in[:, OFF_A:OFF_P]
    k = kv[..., :K_COLS].reshape(B, L, GLA_HEADS, GLA_DK)
    v = kv[..., K_COLS:].reshape(B, L, GLA_HEADS, GLA_DV)
    la_f, la_b = log_decay(a_low, w_dec, b_dec)
    s0 = jnp.zeros((B, GLA_HEADS, GLA_DK, GLA_DV), jnp.float32)
    _, s_f, _ = gla_state(k, v, la_f, s0)
    _, s_b, _ = gla_state(flip(k), flip(v), flip(la_b), s0)
    return s_f, s_b


def hier_moe(h, w_rg, w_re, w_e_in, w_e_out):
    B, L, D = h.shape
    t = h.reshape(-1, D)
    pg = jax.nn.softmax((t @ w_rg).astype(jnp.float32), axis=-1)
    gp, gi = lax.top_k(pg, 1)
    le = (t @ w_re).astype(jnp.float32).reshape(-1, N_GROUPS, EXPERTS_PER_GROUP)
    le = jnp.take_along_axis(le, gi[:, :, None], axis=1)[:, 0]
    ew, ei = lax.top_k(jax.nn.softmax(le, axis=-1), TOP_K_INNER)
    ew = ew / jnp.sum(ew, axis=-1, keepdims=True)
    weights = gp * ew
    eid = gi * EXPERTS_PER_GROUP + ei
    gates = jnp.einsum('tk,tke->et', weights, jax.nn.one_hot(eid, N_EXPERTS, dtype=jnp.float32))

    def body(acc, inp):
        wi, wo, ge = inp
        a, u = jnp.split(t @ wi, 2, axis=-1)
        return acc + ((jax.nn.silu(a) * u) @ wo) * ge[:, None], None

    y, _ = lax.scan(body, jnp.zeros(t.shape, jnp.float32), (w_e_in, w_e_out, gates))
    return y.reshape(B, L, D).astype(h.dtype)


def setup_inputs(seed: int = 0) -> dict:
    key = jax.random.key(seed)
    ks = jax.random.split(key, 24)
    f32 = jnp.float32

    def nrm(k, shape, s):
        return jax.random.normal(k, shape, f32) * s

    D = D_MODEL
    return {
        "x": nrm(ks[0], (BATCH, SEQ, D), 1.0),
        "c": nrm(ks[1], (BATCH, D), 1.0),
        "ctx": nrm(ks[2], (BATCH, CTX_LEN, D), 1.0),
        "c_ctx": nrm(ks[3], (D,), 1.0),
        "w_ada": nrm(ks[4], (DEPTH, D, 6 * D), D ** -0.5),
        "b_ada": nrm(ks[5], (DEPTH, 6 * D), 0.02),
        "norm1_g": 1.0 + nrm(ks[6], (DEPTH, D), 0.02),
        "w_in": nrm(ks[7], (DEPTH, D, IN_COLS), D ** -0.5),
        "w_decay": nrm(ks[8], (DEPTH, 2, GATE_RANK, GLA_HEADS * GLA_DK), GATE_RANK ** -0.5),
        "b_decay": nrm(ks[9], (DEPTH, 2, GLA_HEADS * GLA_DK), 0.1),
        "gla_norm_g": 1.0 + nrm(ks[10], (DEPTH, GLA_DV), 0.02),
        "w_pool": nrm(ks[11], (DEPTH, len(POOL_WINDOWS), POOL_GROUP, POOL_GROUP), POOL_GROUP ** -0.5),
        "pool_scale": 1.0 + nrm(ks[12], (DEPTH, POOL_WIDTH), 0.02),
        "w_out": nrm(ks[13], (DEPTH, MIX_WIDTH, D), MIX_WIDTH ** -0.5),
        "norm2_g": 1.0 + nrm(ks[14], (DEPTH, D), 0.02),
        "w_router_group": nrm(ks[15], (DEPTH, D, N_GROUPS), D ** -0.5),
        "w_router_expert": nrm(ks[16], (DEPTH, D, N_EXPERTS), D ** -0.5),
        "w_expert_in": nrm(ks[17], (DEPTH, N_EXPERTS, D, 2 * D_EXPERT), D ** -0.5),
        "w_expert_out": nrm(ks[18], (DEPTH, N_EXPERTS, D_EXPERT, D), D_EXPERT ** -0.5),
        "final_norm_g": 1.0 + nrm(ks[19], (D,), 0.02),
    }


def reference(x, c, ctx, c_ctx, w_ada, b_ada, norm1_g, w_in, w_decay, b_decay, gla_norm_g, w_pool,
              pool_scale, w_out, norm2_g, w_router_group, w_router_expert, w_expert_in, w_expert_out,
              final_norm_g):
    rows = x.shape[1] // GRID_W
    xc = ctx
    for i in range(DEPTH):
        mod = jax.nn.silu(c) @ w_ada[i] + b_ada[i]
        sh_a, sc_a, gt_a, sh_m, sc_m, gt_m = jnp.split(mod[:, None, :], 6, axis=-1)
        mod_c = jax.nn.silu(c_ctx) @ w_ada[i] + b_ada[i]
        csh_a, csc_a, cgt_a, csh_m, csc_m, cgt_m = jnp.split(mod_c, 6, axis=-1)

        hc = modulate(rmsnorm(xc, norm1_g[i]), csh_a, csc_a)
        if i + 1 < DEPTH:
            zero_s = jnp.zeros((xc.shape[0], GLA_HEADS, GLA_DK, GLA_DV), jnp.float32)
            out_c, s_f, s_b = token_mixer(hc, w_in[i], w_decay[i], b_decay[i], gla_norm_g[i], w_pool[i],
                                          pool_scale[i], w_out[i], zero_s, zero_s, None)
            xc_next = xc + (cgt_a * out_c).astype(xc.dtype)
            hc2 = modulate(rmsnorm(xc_next, norm2_g[i]), csh_m, csc_m)
            xc_next = xc_next + (cgt_m * hier_moe(hc2, w_router_group[i], w_router_expert[i],
                                                  w_expert_in[i], w_expert_out[i])).astype(xc.dtype)
        else:
            s_f, s_b = context_states(hc, w_in[i], w_decay[i], b_decay[i])
            xc_next = xc

        h = modulate(rmsnorm(x, norm1_g[i]), sh_a, sc_a)
        out, _, _ = token_mixer(h, w_in[i], w_decay[i], b_decay[i], gla_norm_g[i], w_pool[i],
                                pool_scale[i], w_out[i], s_f, s_b, rows)
        x = x + (gt_a * out).astype(x.dtype)
        h2 = modulate(rmsnorm(x, norm2_g[i]), sh_m, sc_m)
        x = x + (gt_m * hier_moe(h2, w_router_group[i], w_router_expert[i],
                                 w_expert_in[i], w_expert_out[i])).astype(x.dtype)
        xc = xc_next
    return rmsnorm(x, final_norm_g)
```

```python
import functools

import jax
import jax.numpy as jnp
from jax import lax
from jax.experimental import pallas as pl
from jax.experimental.pallas import tpu as pltpu

F32 = jnp.float32
BF16 = jnp.bfloat16

GRID_W = 64
GRID_W_LOG2 = 6
assert 1 << GRID_W_LOG2 == GRID_W
GLA_HEADS = 4
GLA_DK = 64
GLA_DV = 128
GATE_RANK = 16
GATE_NORMALIZER = 16.0
CHUNK = 64
POOL_WINDOWS = (2, 4, 8, 16)
POOL_GROUP = 128
N_GROUPS = 4
EXPERTS_PER_GROUP = 8
N_EXPERTS = N_GROUPS * EXPERTS_PER_GROUP
D_EXPERT = 256
EPS = 1e-6

QK_W = GLA_HEADS * GLA_DK
V_W = GLA_HEADS * GLA_DV
POOL_W = POOL_GROUP * len(POOL_WINDOWS)
LANES = 128
VMEM_LIMIT = 48 * 1024 * 1024

_NT = (((1,), (1,)), ((), ()))


def _dot(a, b):
    return jnp.dot(a, b, preferred_element_type=F32)


def _split_bf16(x):
    hi = x.astype(BF16)
    lo = (x - hi.astype(F32)).astype(BF16)
    return hi, lo


def _rms(x):
    return x * lax.rsqrt(jnp.mean(x * x, axis=-1, keepdims=True) + EPS)


def _silu(x):
    return x / (1.0 + jnp.exp(-x))


def _params(*sem):
    return pltpu.CompilerParams(dimension_semantics=sem, vmem_limit_bytes=VMEM_LIMIT)


def _ada_kernel(c_ref, w_ref, b_ref, o_ref):
    s = _silu(c_ref[...]).astype(BF16)
    o_ref[...] = _dot(s, w_ref[...].astype(BF16)) + b_ref[...]


def _ada_mod(c8, w_ada, b_ada):
    rows, d = c8.shape
    n = w_ada.shape[1]
    tn = 1024
    return pl.pallas_call(
        _ada_kernel,
        grid=(n // tn,),
        in_specs=[pl.BlockSpec((rows, d), lambda j: (0, 0)),
                  pl.BlockSpec((d, tn), lambda j: (0, j)),
                  pl.BlockSpec((1, tn), lambda j: (0, j))],
        out_specs=pl.BlockSpec((rows, tn), lambda j: (0, j)),
        out_shape=jax.ShapeDtypeStruct((rows, n), F32),
        compiler_params=_params("arbitrary"),
        name="ada_mod",
    )(c8, w_ada, b_ada.reshape(1, n))


def _inproj_kernel(x_ref, sh_ref, sc_ref, g1_ref, w_ref, wdec_ref, bdec_ref,
                   q_ref, k_ref, v_ref, g_ref, la_ref, p_ref):
    x = x_ref[0]
    h = _rms(x) * g1_ref[...]
    h = h * (1.0 + sc_ref[0]) + sh_ref[0]
    hb = h.astype(BF16)
    o = 0
    q_ref[0] = (_dot(hb, w_ref[:, o:o + QK_W]) * (GLA_DK ** -0.5)).astype(BF16)
    o += QK_W
    k_ref[0] = _dot(hb, w_ref[:, o:o + QK_W]).astype(BF16)
    o += QK_W
    v_ref[0] = _dot(hb, w_ref[:, o:o + V_W]).astype(BF16)
    o += V_W
    g_ref[0] = _dot(hb, w_ref[:, o:o + V_W]).astype(BF16)
    o += V_W
    p_ref[0] = _dot(hb, w_ref[:, o:o + POOL_W])
    o += POOL_W
    a_low = _dot(hb, w_ref[:, o:o + LANES])
    z = _dot(a_low.astype(BF16), wdec_ref[...]) + bdec_ref[...]
    log_sig = jnp.minimum(z, 0.0) - jnp.log(1.0 + jnp.exp(-jnp.abs(z)))
    la_ref[0] = log_sig / GATE_NORMALIZER


def _inproj(x, shift, scale, g1, w_r, wdec, bdec, tm):
    b, l, d = x.shape
    wcols = w_r.shape[1]
    tok = lambda bi, i: (bi, i, 0)
    vec = lambda bi, i: (bi, 0, 0)
    fixed = lambda bi, i: (0, 0)
    outs = [(QK_W, BF16), (QK_W, BF16), (V_W, BF16), (V_W, BF16), (2 * QK_W, F32), (POOL_W, F32)]
    return pl.pallas_call(
        _inproj_kernel,
        grid=(b, l // tm),
        in_specs=[pl.BlockSpec((1, tm, d), tok),
                  pl.BlockSpec((1, 1, d), vec),
                  pl.BlockSpec((1, 1, d), vec),
                  pl.BlockSpec((1, d), fixed),
                  pl.BlockSpec((d, wcols), fixed),
                  pl.BlockSpec(wdec.shape, fixed),
                  pl.BlockSpec(bdec.shape, fixed)],
        out_specs=[pl.BlockSpec((1, tm, w), tok) for w, _ in outs],
        out_shape=[jax.ShapeDtypeStruct((b, l, w), dt) for w, dt in outs],
        compiler_params=_params("arbitrary", "arbitrary"),
        name="inproj",
    )(x, shift, scale, g1, w_r, wdec, bdec)


def _gla_chunk(q, k, v, la, s_ref, tri, tri_mask, last_row, head_masks):
    la_hi, la_lo = _split_bf16(la)
    b = _dot(tri, la_hi) + _dot(tri, la_lo)
    b_last = b[last_row:last_row + 1, :]
    decay_col = jnp.broadcast_to(jnp.exp(b_last), (LANES, QK_W)).T
    qb = q * jnp.exp(b)
    kb = (k * jnp.exp(-b)).astype(BF16)
    k_st = k * jnp.exp(b_last - b)
    lhs = jnp.concatenate([jnp.where(m, qb, 0.0).astype(BF16) for m in head_masks], axis=0)
    scores = lax.dot_general(lhs, kb, _NT, preferred_element_type=F32)
    probs = jnp.where(tri_mask, scores, 0.0).astype(BF16)
    s_old = s_ref[...]
    inter = _dot(lhs, s_old.astype(BF16))
    k_st_t = k_st.T.astype(BF16)
    c = q.shape[0]
    outs, d_states = [], []
    for h in range(GLA_HEADS):
        v_h = v[:, h * GLA_DV:(h + 1) * GLA_DV]
        outs.append(_dot(probs[h * c:(h + 1) * c], v_h) + inter[h * c:(h + 1) * c])
        d_states.append(_dot(k_st_t[h * GLA_DK:(h + 1) * GLA_DK], v_h))
    s_ref[...] = decay_col * s_old + jnp.concatenate(d_states, axis=0)
    return outs


def _gla_kernel(qf_ref, kf_ref, vf_ref, laf_ref, qb_ref, kb_ref, vb_ref, lab_ref, s0f_ref, s0b_ref,
                of_ref, ob_ref, sf_out_ref, sb_out_ref, sf_ref, sb_ref, *, n_chunks):
    j = pl.program_id(1)

    @pl.when(j == 0)
    def _():
        sf_ref[...] = s0f_ref[0]
        sb_ref[...] = s0b_ref[0]

    c = CHUNK
    row = lax.broadcasted_iota(jnp.int32, (c, c), 0)
    col = lax.broadcasted_iota(jnp.int32, (c, c), 1)
    tri_f = jnp.where(col <= row, 1.0, 0.0).astype(BF16)
    tri_b = jnp.where(col >= row, 1.0, 0.0).astype(BF16)
    row4 = lax.broadcasted_iota(jnp.int32, (GLA_HEADS * c, c), 0) & (c - 1)
    col4 = lax.broadcasted_iota(jnp.int32, (GLA_HEADS * c, c), 1)
    mask_f = col4 <= row4
    mask_b = col4 >= row4
    lane = lax.broadcasted_iota(jnp.int32, (c, QK_W), 1)
    head_masks = [(lane >= h * GLA_DK) & (lane < (h + 1) * GLA_DK) for h in range(GLA_HEADS)]

    def body(i, carry):
        rf = pl.multiple_of(i * c, c)
        rb = pl.multiple_of((n_chunks - 1 - i) * c, c)
        outs_f = _gla_chunk(qf_ref[0, pl.ds(rf, c), :].astype(F32), kf_ref[0, pl.ds(rf, c), :].astype(F32),
                            vf_ref[0, pl.ds(rf, c), :], laf_ref[0, pl.ds(rf, c), :],
                            sf_ref, tri_f, mask_f, c - 1, head_masks)
        outs_b = _gla_chunk(qb_ref[0, pl.ds(rb, c), :].astype(F32), kb_ref[0, pl.ds(rb, c), :].astype(F32),
                            vb_ref[0, pl.ds(rb, c), :], lab_ref[0, pl.ds(rb, c), :],
                            sb_ref, tri_b, mask_b, 0, head_masks)
        for h in range(GLA_HEADS):
            of_ref[0, pl.ds(rf, c), h * GLA_DV:(h + 1) * GLA_DV] = outs_f[h].astype(BF16)
            ob_ref[0, pl.ds(rb, c), h * GLA_DV:(h + 1) * GLA_DV] = outs_b[h].astype(BF16)
        return carry

    lax.fori_loop(0, n_chunks, body, 0)

    @pl.when(j == pl.num_programs(1) - 1)
    def _():
        sf_out_ref[0] = sf_ref[...]
        sb_out_ref[0] = sb_ref[...]


def _gla(q, k, v, la, s0f, s0b, tb):
    b, l, _ = q.shape
    nb = l // tb
    fwd = lambda bi, j: (bi, j, 0)
    bwd = lambda bi, j: (bi, nb - 1 - j, 0)
    fwd_la = lambda bi, j: (bi, j, 0)
    bwd_la = lambda bi, j: (bi, nb - 1 - j, 1)
    st = lambda bi, j: (bi, 0, 0)
    state_shape = (b, QK_W, GLA_DV)
    return pl.pallas_call(
        functools.partial(_gla_kernel, n_chunks=tb // CHUNK),
        grid=(b, nb),
        in_specs=[pl.BlockSpec((1, tb, QK_W), fwd), pl.BlockSpec((1, tb, QK_W), fwd),
                  pl.BlockSpec((1, tb, V_W), fwd), pl.BlockSpec((1, tb, QK_W), fwd_la),
                  pl.BlockSpec((1, tb, QK_W), bwd), pl.BlockSpec((1, tb, QK_W), bwd),
                  pl.BlockSpec((1, tb, V_W), bwd), pl.BlockSpec((1, tb, QK_W), bwd_la),
                  pl.BlockSpec((1, QK_W, GLA_DV), st), pl.BlockSpec((1, QK_W, GLA_DV), st)],
        out_specs=[pl.BlockSpec((1, tb, V_W), fwd), pl.BlockSpec((1, tb, V_W), bwd),
                   pl.BlockSpec((1, QK_W, GLA_DV), st), pl.BlockSpec((1, QK_W, GLA_DV), st)],
        out_shape=[jax.ShapeDtypeStruct((b, l, V_W), BF16), jax.ShapeDtypeStruct((b, l, V_W), BF16),
                   jax.ShapeDtypeStruct(state_shape, F32), jax.ShapeDtypeStruct(state_shape, F32)],
        scratch_shapes=[pltpu.VMEM((QK_W, GLA_DV), F32), pltpu.VMEM((QK_W, GLA_DV), F32)],
        compiler_params=_params("arbitrary", "arbitrary"),
        name="gla",
    )(q, k, v, la, q, k, v, la, s0f, s0b)


POOL_BLK = 256


def _pool_body(w, p_ref, wp_ref, ps_ref, o_ref, buf_ref, n_tok):
    lo = w // 2
    hi = w - 1 - lo
    pad = lo * GRID_W
    n_rows = n_tok // GRID_W
    blk = POOL_BLK
    i0 = lax.broadcasted_iota(jnp.int32, (blk, blk), 0)
    i1 = lax.broadcasted_iota(jnp.int32, (blk, blk), 1)
    same_row = (i0 >> GRID_W_LOG2) == (i1 >> GRID_W_LOG2)
    band = jnp.where(same_row & (i1 - i0 >= -lo) & (i1 - i0 <= hi), 1.0, 0.0).astype(BF16)
    zeros = jnp.zeros((GRID_W * 8, LANES), F32)
    buf_ref[0:pad, :] = zeros[0:pad]
    buf_ref[pad + n_tok:pad + n_tok + hi * GRID_W + GRID_W, :] = zeros[0:hi * GRID_W + GRID_W]
    wp = wp_ref[0].astype(BF16)
    scale = ps_ref[0]

    def col_step(t, carry):
        r = pl.multiple_of(t * blk, blk)
        x_hi, x_lo = _split_bf16(p_ref[0, pl.ds(r, blk), :])
        buf_ref[pl.ds(pad + r, blk), :] = _dot(band, x_hi) + _dot(band, x_lo)
        return carry

    lax.fori_loop(0, n_tok // blk, col_step, 0)

    def row_step(t, carry):
        r = pl.multiple_of(t * blk, blk)
        acc = buf_ref[pl.ds(r, blk), :]
        for d in range(1, w):
            acc = acc + buf_ref[pl.ds(r + d * GRID_W, blk), :]
        tok = r + lax.broadcasted_iota(jnp.int32, (blk, LANES), 0)
        g_row = tok >> GRID_W_LOG2
        g_col = tok & (GRID_W - 1)
        cnt_c = jnp.minimum(g_col + hi, GRID_W - 1) - jnp.maximum(g_col - lo, 0) + 1
        cnt_r = jnp.minimum(g_row + hi, n_rows - 1) - jnp.maximum(g_row - lo, 0) + 1
        m = acc / (cnt_c * cnt_r).astype(F32)
        y = (m - p_ref[0, pl.ds(r, blk), :]).astype(BF16)
        o_ref[0, pl.ds(r, blk), :] = (_dot(y, wp) * scale).astype(BF16)
        return carry

    lax.fori_loop(0, n_tok // blk, row_step, 0)


def _pool_kernel(p_ref, wp_ref, ps_ref, o_ref, buf_ref, *, n_tok):
    g = pl.program_id(1)
    for gi, w in enumerate(POOL_WINDOWS):
        @pl.when(g == gi)
        def _(w=w):
            _pool_body(w, p_ref, wp_ref, ps_ref, o_ref, buf_ref, n_tok)


def _pool(p, w_pool, pool_scale):
    b, l, _ = p.shape
    n_g = len(POOL_WINDOWS)
    max_w = max(POOL_WINDOWS)
    buf_rows = l + (max_w + 1) * GRID_W
    return pl.pallas_call(
        functools.partial(_pool_kernel, n_tok=l),
        grid=(b, n_g),
        in_specs=[pl.BlockSpec((1, l, POOL_GROUP), lambda bi, g: (bi, 0, g)),
                  pl.BlockSpec((1, POOL_GROUP, POOL_GROUP), lambda bi, g: (g, 0, 0)),
                  pl.BlockSpec((1, 1, POOL_GROUP), lambda bi, g: (g, 0, 0))],
        out_specs=pl.BlockSpec((1, l, POOL_GROUP), lambda bi, g: (bi, 0, g)),
        out_shape=jax.ShapeDtypeStruct((b, l, POOL_W), BF16),
        scratch_shapes=[pltpu.VMEM((buf_rows, LANES), F32)],
        compiler_params=_params("arbitrary", "arbitrary"),
        name="pool",
    )(p, w_pool, pool_scale.reshape(n_g, 1, POOL_GROUP))


ROUTER_GROUP_LANE0 = N_EXPERTS


def _outproj_kernel(of_ref, ob_ref, g_ref, pooled_ref, x_ref, gn_ref, wout_ref, gt_ref, sh_ref, sc_ref, g2_ref,
                    wr_hi_ref, wr_lo_ref, x1_ref, h2_ref, gates_ref):
    o = of_ref[0].astype(F32) + ob_ref[0].astype(F32)
    gate = _silu(g_ref[0].astype(F32))
    parts = []
    for h in range(GLA_HEADS):
        sl = slice(h * GLA_DV, (h + 1) * GLA_DV)
        parts.append((_rms(o[:, sl]) * gn_ref[...] * gate[:, sl]).astype(BF16))
    o_n = jnp.concatenate(parts, axis=-1)
    out = _dot(o_n, wout_ref[0:V_W, :]) + _dot(pooled_ref[0], wout_ref[V_W:V_W + POOL_W, :])
    x1 = x_ref[0] + gt_ref[0] * out
    x1_ref[0] = x1
    h2 = _rms(x1) * g2_ref[...]
    h2 = h2 * (1.0 + sc_ref[0]) + sh_ref[0]
    h2_bf = h2.astype(BF16)
    h2_ref[0] = h2_bf
    h2_lo = (h2 - h2_bf.astype(F32)).astype(BF16)
    logits = _dot(h2_bf, wr_hi_ref[...]) + _dot(h2_lo, wr_hi_ref[...]) + _dot(h2_bf, wr_lo_ref[...])

    lane = lax.broadcasted_iota(jnp.int32, logits.shape, 1)
    neg = -jnp.inf
    big = jnp.int32(1 << 20)
    is_group = (lane >= ROUTER_GROUP_LANE0) & (lane < ROUTER_GROUP_LANE0 + N_GROUPS)
    gl = jnp.where(is_group, logits, neg)
    g_max = jnp.max(gl, axis=-1, keepdims=True)
    g_idx = jnp.min(jnp.where(gl == g_max, lane - ROUTER_GROUP_LANE0, big), axis=-1, keepdims=True)
    g_prob = 1.0 / jnp.sum(jnp.where(is_group, jnp.exp(gl - g_max), 0.0), axis=-1, keepdims=True)
    in_group = (lane >= g_idx * EXPERTS_PER_GROUP) & (lane < (g_idx + 1) * EXPERTS_PER_GROUP)
    el = jnp.where(in_group, logits, neg)
    m1 = jnp.max(el, axis=-1, keepdims=True)
    i1 = jnp.min(jnp.where(el == m1, lane, big), axis=-1, keepdims=True)
    el2 = jnp.where(lane == i1, neg, el)
    m2 = jnp.max(el2, axis=-1, keepdims=True)
    i2 = jnp.min(jnp.where(el2 == m2, lane, big), axis=-1, keepdims=True)
    r = jnp.exp(m2 - m1)
    w1 = g_prob / (1.0 + r)
    w2 = g_prob * r / (1.0 + r)
    gates_ref[0] = jnp.where(lane == i1, w1, 0.0) + jnp.where(lane == i2, w2, 0.0)


def _outproj(o_f, o_b, g, pooled, x, gn, w_out, gt_a, sh_m, sc_m, g2, wr_hi, wr_lo, tm):
    b, l, d = x.shape
    tok = lambda bi, i: (bi, i, 0)
    vec = lambda bi, i: (bi, 0, 0)
    fixed = lambda bi, i: (0, 0)
    return pl.pallas_call(
        _outproj_kernel,
        grid=(b, l // tm),
        in_specs=[pl.BlockSpec((1, tm, V_W), tok), pl.BlockSpec((1, tm, V_W), tok),
                  pl.BlockSpec((1, tm, V_W), tok), pl.BlockSpec((1, tm, POOL_W), tok),
                  pl.BlockSpec((1, tm, d), tok),
                  pl.BlockSpec((1, GLA_DV), fixed),
                  pl.BlockSpec(w_out.shape, fixed),
                  pl.BlockSpec((1, 1, d), vec), pl.BlockSpec((1, 1, d), vec), pl.BlockSpec((1, 1, d), vec),
                  pl.BlockSpec((1, d), fixed),
                  pl.BlockSpec(wr_hi.shape, fixed), pl.BlockSpec(wr_lo.shape, fixed)],
        out_specs=[pl.BlockSpec((1, tm, d), tok), pl.BlockSpec((1, tm, d), tok), pl.BlockSpec((1, tm, LANES), tok)],
        out_shape=[jax.ShapeDtypeStruct((b, l, d), F32), jax.ShapeDtypeStruct((b, l, d), BF16),
                   jax.ShapeDtypeStruct((b, l, LANES), F32)],
        compiler_params=_params("arbitrary", "arbitrary"),
        name="outproj_router",
    )(o_f, o_b, g, pooled, x, gn, w_out, gt_a, sh_m, sc_m, g2, wr_hi, wr_lo)


def _moe_kernel(h2_ref, gates_ref, x1_ref, wi_ref, wo_ref, gt_ref, gf_ref, o_ref, acc_ref):
    e = pl.program_id(2)

    @pl.when(e == 0)
    def _():
        acc_ref[...] = jnp.zeros_like(acc_ref)

    au = _dot(h2_ref[0], wi_ref[0])
    a = au[:, :D_EXPERT]
    u = au[:, D_EXPERT:]
    gates = gates_ref[0]
    lane = lax.broadcasted_iota(jnp.int32, gates.shape, 1)
    ge = jnp.sum(jnp.where(lane == e, gates, 0.0), axis=-1, keepdims=True)
    hid = (_silu(a) * u * ge).astype(BF16)
    acc_ref[...] += _dot(hid, wo_ref[0])

    @pl.when(e == pl.num_programs(2) - 1)
    def _():
        x2 = x1_ref[0] + gt_ref[0] * acc_ref[...]
        o_ref[0] = _rms(x2) * gf_ref[...]


def _moe(h2, gates, x1, wi, wo, gt_m, gf, tm):
    b, l, d = x1.shape
    n_e = wi.shape[0]
    tok = lambda bi, i, e: (bi, i, 0)
    vec = lambda bi, i, e: (bi, 0, 0)
    return pl.pallas_call(
        _moe_kernel,
        grid=(b, l // tm, n_e),
        in_specs=[pl.BlockSpec((1, tm, d), tok), pl.BlockSpec((1, tm, LANES), tok), pl.BlockSpec((1, tm, d), tok),
                  pl.BlockSpec((1,) + wi.shape[1:], lambda bi, i, e: (e, 0, 0)),
                  pl.BlockSpec((1,) + wo.shape[1:], lambda bi, i, e: (e, 0, 0)),
                  pl.BlockSpec((1, 1, d), vec),
                  pl.BlockSpec((1, d), lambda bi, i, e: (0, 0))],
        out_specs=pl.BlockSpec((1, tm, d), tok),
        out_shape=jax.ShapeDtypeStruct((b, l, d), F32),
        scratch_shapes=[pltpu.VMEM((tm, d), F32)],
        compiler_params=_params("arbitrary", "arbitrary", "arbitrary"),
        name="moe",
    )(h2, gates, x1, wi, wo, gt_m, gf)


def kernel(x, c, ctx, c_ctx, w_ada, b_ada, norm1_g, w_in, w_decay, b_decay, gla_norm_g, w_pool, pool_scale, w_out,
           norm2_g, w_router_group, w_router_expert, w_expert_in, w_expert_out, final_norm_g):
    assert w_ada.shape[0] == 1, "single-layer trunk"
    b, l, d = x.shape
    off_a = 2 * QK_W + 2 * V_W
    a_cols = 2 * GATE_RANK
    off_p = off_a + a_cols

    c8 = jnp.zeros((8, d), F32).at[:b].set(c).at[b].set(c_ctx)
    mod = _ada_mod(c8, w_ada[0], b_ada[0])
    sh_a, sc_a, gt_a, sh_m, sc_m, gt_m = [m.reshape(8, 1, d) for m in jnp.split(mod, 6, axis=-1)]
    ctx_rows = lambda m: jnp.broadcast_to(m[b:b + 1], (b, 1, d))

    w = w_in[0]
    w_r = jnp.concatenate([w[:, :off_a], w[:, off_p:off_p + POOL_W], w[:, off_a:off_p],
                           jnp.zeros((d, LANES - a_cols), F32)], axis=1).astype(BF16)
    wdec = jnp.zeros((LANES, 2 * QK_W), F32)
    wdec = wdec.at[:GATE_RANK, :QK_W].set(w_decay[0, 0]).at[GATE_RANK:a_cols, QK_W:].set(w_decay[0, 1]).astype(BF16)
    bdec = b_decay[0].reshape(1, 2 * QK_W)
    g1 = norm1_g[0].reshape(1, d)

    q_c, k_c, v_c, _, la_c, _ = _inproj(ctx, ctx_rows(sh_a), ctx_rows(sc_a), g1, w_r, wdec, bdec, tm=ctx.shape[1])
    zero_s = jnp.zeros((b, QK_W, GLA_DV), F32)
    _, _, s_f, s_b = _gla(q_c, k_c, v_c, la_c, zero_s, zero_s, tb=ctx.shape[1])

    q, k, v, g, la, p = _inproj(x, sh_a[:b], sc_a[:b], g1, w_r, wdec, bdec, tm=1024)
    o_f, o_b, _, _ = _gla(q, k, v, la, s_f, s_b, tb=1024)
    pooled = _pool(p, w_pool[0], pool_scale[0])

    w_router = jnp.zeros((d, LANES), F32)
    w_router = w_router.at[:, :N_EXPERTS].set(w_router_expert[0])
    w_router = w_router.at[:, ROUTER_GROUP_LANE0:ROUTER_GROUP_LANE0 + N_GROUPS].set(w_router_group[0])
    wr_hi = w_router.astype(BF16)
    wr_lo = (w_router - wr_hi.astype(F32)).astype(BF16)
    x1, h2, gates = _outproj(o_f, o_b, g, pooled, x, gla_norm_g[0].reshape(1, GLA_DV), w_out[0].astype(BF16),
                             gt_a[:b], sh_m[:b], sc_m[:b], norm2_g[0].reshape(1, d), wr_hi, wr_lo, tm=512)

    return _moe(h2, gates, x1, w_expert_in[0].astype(BF16), w_expert_out[0].astype(BF16), gt_m[:b],
                final_norm_g.reshape(1, d), tm=1024)
```

```python
import functools

import jax
import jax.numpy as jnp
from jax import lax
from jax.experimental import pallas as pl
from jax.experimental.pallas import tpu as pltpu

F32 = jnp.float32
BF16 = jnp.bfloat16

GRID_W = 64
GRID_W_LOG2 = 6
assert 1 << GRID_W_LOG2 == GRID_W
GLA_HEADS = 4
GLA_DK = 64
GLA_DV = 128
GATE_RANK = 16
GATE_NORMALIZER = 16.0
CHUNK = 64
POOL_WINDOWS = (2, 4, 8, 16)
POOL_GROUP = 128
N_GROUPS = 4
EXPERTS_PER_GROUP = 8
N_EXPERTS = N_GROUPS * EXPERTS_PER_GROUP
D_EXPERT = 256
EPS = 1e-6

QK_W = GLA_HEADS * GLA_DK
V_W = GLA_HEADS * GLA_DV
POOL_W = POOL_GROUP * len(POOL_WINDOWS)
LANES = 128
VMEM_LIMIT = 48 * 1024 * 1024

_NT = (((1,), (1,)), ((), ()))


def _dot(a, b):
    return jnp.dot(a, b, preferred_element_type=F32)


def _split_bf16(x):
    hi = x.astype(BF16)
    lo = (x - hi.astype(F32)).astype(BF16)
    return hi, lo


def _rms(x):
    return x * lax.rsqrt(jnp.mean(x * x, axis=-1, keepdims=True) + EPS)


def _silu(x):
    return x / (1.0 + jnp.exp(-x))


def _params(*sem):
    return pltpu.CompilerParams(dimension_semantics=sem, vmem_limit_bytes=VMEM_LIMIT)


def _ada_kernel(c_ref, w_ref, b_ref, o_ref):
    s = _silu(c_ref[...]).astype(BF16)
    o_ref[...] = _dot(s, w_ref[...].astype(BF16)) + b_ref[...]


def _ada_mod(c8, w_ada, b_ada):
    rows, d = c8.shape
    n = w_ada.shape[1]
    tn = 1024
    return pl.pallas_call(
        _ada_kernel,
        grid=(n // tn,),
        in_specs=[pl.BlockSpec((rows, d), lambda j: (0, 0)),
                  pl.BlockSpec((d, tn), lambda j: (0, j)),
                  pl.BlockSpec((1, tn), lambda j: (0, j))],
        out_specs=pl.BlockSpec((rows, tn), lambda j: (0, j)),
        out_shape=jax.ShapeDtypeStruct((rows, n), F32),
        compiler_params=_params("arbitrary"),
        name="ada_mod",
    )(c8, w_ada, b_ada.reshape(1, n))


def _inproj_kernel(x_ref, sh_ref, sc_ref, g1_ref, w_ref, wdec_ref, bdec_ref,
                   q_ref, k_ref, v_ref, g_ref, la_ref, p_ref):
    x = x_ref[0]
    h = _rms(x) * g1_ref[...]
    h = h * (1.0 + sc_ref[0]) + sh_ref[0]
    hb = h.astype(BF16)
    o = 0
    q_ref[0] = (_dot(hb, w_ref[:, o:o + QK_W]) * (GLA_DK ** -0.5)).astype(BF16)
    o += QK_W
    k_ref[0] = _dot(hb, w_ref[:, o:o + QK_W]).astype(BF16)
    o += QK_W
    v_ref[0] = _dot(hb, w_ref[:, o:o + V_W]).astype(BF16)
    o += V_W
    g_ref[0] = _dot(hb, w_ref[:, o:o + V_W]).astype(BF16)
    o += V_W
    p_ref[0] = _dot(hb, w_ref[:, o:o + POOL_W])
    o += POOL_W
    a_low = _dot(hb, w_ref[:, o:o + LANES])
    z = _dot(a_low.astype(BF16), wdec_ref[...]) + bdec_ref[...]
    log_sig = jnp.minimum(z, 0.0) - jnp.log(1.0 + jnp.exp(-jnp.abs(z)))
    la_ref[0] = log_sig / GATE_NORMALIZER


def _inproj(x, shift, scale, g1, w_r, wdec, bdec, tm):
    b, l, d = x.shape
    wcols = w_r.shape[1]
    tok = lambda bi, i: (bi, i, 0)
    vec = lambda bi, i: (bi, 0, 0)
    fixed = lambda bi, i: (0, 0)
    outs = [(QK_W, BF16), (QK_W, BF16), (V_W, BF16), (V_W, BF16), (2 * QK_W, F32), (POOL_W, F32)]
    return pl.pallas_call(
        _inproj_kernel,
        grid=(b, l // tm),
        in_specs=[pl.BlockSpec((1, tm, d), tok),
                  pl.BlockSpec((1, 1, d), vec),
                  pl.BlockSpec((1, 1, d), vec),
                  pl.BlockSpec((1, d), fixed),
                  pl.BlockSpec((d, wcols), fixed),
                  pl.BlockSpec(wdec.shape, fixed),
                  pl.BlockSpec(bdec.shape, fixed)],
        out_specs=[pl.BlockSpec((1, tm, w), tok) for w, _ in outs],
        out_shape=[jax.ShapeDtypeStruct((b, l, w), dt) for w, dt in outs],
        compiler_params=_params("arbitrary", "arbitrary"),
        name="inproj",
    )(x, shift, scale, g1, w_r, wdec, bdec)


def _gla_chunk(q, k, v, la, s_ref, tri, tri_mask, last_row, head_masks):
    la_hi, la_lo = _split_bf16(la)
    b = _dot(tri, la_hi) + _dot(tri, la_lo)
    b_last = b[last_row:last_row + 1, :]
    decay_col = jnp.broadcast_to(jnp.exp(b_last), (LANES, QK_W)).T
    qb = q * jnp.exp(b)
    kb = (k * jnp.exp(-b)).astype(BF16)
    k_st = k * jnp.exp(b_last - b)
    lhs = jnp.concatenate([jnp.where(m, qb, 0.0).astype(BF16) for m in head_masks], axis=0)
    scores = lax.dot_general(lhs, kb, _NT, preferred_element_type=F32)
    probs = jnp.where(tri_mask, scores, 0.0).astype(BF16)
    s_old = s_ref[...]
    inter = _dot(lhs, s_old.astype(BF16))
    k_st_t = k_st.T.astype(BF16)
    c = q.shape[0]
    outs, d_states = [], []
    for h in range(GLA_HEADS):
        v_h = v[:, h * GLA_DV:(h + 1) * GLA_DV]
        outs.append(_dot(probs[h * c:(h + 1) * c], v_h) + inter[h * c:(h + 1) * c])
        d_states.append(_dot(k_st_t[h * GLA_DK:(h + 1) * GLA_DK], v_h))
    s_ref[...] = decay_col * s_old + jnp.concatenate(d_states, axis=0)
    return outs


def _gla_kernel(qf_ref, kf_ref, vf_ref, laf_ref, qb_ref, kb_ref, vb_ref, lab_ref, s0f_ref, s0b_ref,
                of_ref, ob_ref, sf_out_ref, sb_out_ref, sf_ref, sb_ref, *, n_chunks):
    j = pl.program_id(1)

    @pl.when(j == 0)
    def _():
        sf_ref[...] = s0f_ref[0]
        sb_ref[...] = s0b_ref[0]

    c = CHUNK
    row = lax.broadcasted_iota(jnp.int32, (c, c), 0)
    col = lax.broadcasted_iota(jnp.int32, (c, c), 1)
    tri_f = jnp.where(col <= row, 1.0, 0.0).astype(BF16)
    tri_b = jnp.where(col >= row, 1.0, 0.0).astype(BF16)
    row4 = lax.broadcasted_iota(jnp.int32, (GLA_HEADS * c, c), 0) & (c - 1)
    col4 = lax.broadcasted_iota(jnp.int32, (GLA_HEADS * c, c), 1)
    mask_f = col4 <= row4
    mask_b = col4 >= row4
    lane = lax.broadcasted_iota(jnp.int32, (c, QK_W), 1)
    head_masks = [(lane >= h * GLA_DK) & (lane < (h + 1) * GLA_DK) for h in range(GLA_HEADS)]

    def body(i, carry):
        rf = pl.multiple_of(i * c, c)
        rb = pl.multiple_of((n_chunks - 1 - i) * c, c)
        outs_f = _gla_chunk(qf_ref[0, pl.ds(rf, c), :].astype(F32), kf_ref[0, pl.ds(rf, c), :].astype(F32),
                            vf_ref[0, pl.ds(rf, c), :], laf_ref[0, pl.ds(rf, c), :],
                            sf_ref, tri_f, mask_f, c - 1, head_masks)
        outs_b = _gla_chunk(qb_ref[0, pl.ds(rb, c), :].astype(F32), kb_ref[0, pl.ds(rb, c), :].astype(F32),
                            vb_ref[0, pl.ds(rb, c), :], lab_ref[0, pl.ds(rb, c), :],
                            sb_ref, tri_b, mask_b, 0, head_masks)
        for h in range(GLA_HEADS):
            of_ref[0, pl.ds(rf, c), h * GLA_DV:(h + 1) * GLA_DV] = outs_f[h].astype(BF16)
            ob_ref[0, pl.ds(rb, c), h * GLA_DV:(h + 1) * GLA_DV] = outs_b[h].astype(BF16)
        return carry

    lax.fori_loop(0, n_chunks, body, 0)

    @pl.when(j == pl.num_programs(1) - 1)
    def _():
        sf_out_ref[0] = sf_ref[...]
        sb_out_ref[0] = sb_ref[...]


def _gla(q, k, v, la, s0f, s0b, tb):
    b, l, _ = q.shape
    nb = l // tb
    fwd = lambda bi, j: (bi, j, 0)
    bwd = lambda bi, j: (bi, nb - 1 - j, 0)
    fwd_la = lambda bi, j: (bi, j, 0)
    bwd_la = lambda bi, j: (bi, nb - 1 - j, 1)
    st = lambda bi, j: (bi, 0, 0)
    state_shape = (b, QK_W, GLA_DV)
    return pl.pallas_call(
        functools.partial(_gla_kernel, n_chunks=tb // CHUNK),
        grid=(b, nb),
        in_specs=[pl.BlockSpec((1, tb, QK_W), fwd), pl.BlockSpec((1, tb, QK_W), fwd),
                  pl.BlockSpec((1, tb, V_W), fwd), pl.BlockSpec((1, tb, QK_W), fwd_la),
                  pl.BlockSpec((1, tb, QK_W), bwd), pl.BlockSpec((1, tb, QK_W), bwd),
                  pl.BlockSpec((1, tb, V_W), bwd), pl.BlockSpec((1, tb, QK_W), bwd_la),
                  pl.BlockSpec((1, QK_W, GLA_DV), st), pl.BlockSpec((1, QK_W, GLA_DV), st)],
        out_specs=[pl.BlockSpec((1, tb, V_W), fwd), pl.BlockSpec((1, tb, V_W), bwd),
                   pl.BlockSpec((1, QK_W, GLA_DV), st), pl.BlockSpec((1, QK_W, GLA_DV), st)],
        out_shape=[jax.ShapeDtypeStruct((b, l, V_W), BF16), jax.ShapeDtypeStruct((b, l, V_W), BF16),
                   jax.ShapeDtypeStruct(state_shape, F32), jax.ShapeDtypeStruct(state_shape, F32)],
        scratch_shapes=[pltpu.VMEM((QK_W, GLA_DV), F32), pltpu.VMEM((QK_W, GLA_DV), F32)],
        compiler_params=_params("arbitrary", "arbitrary"),
        name="gla",
    )(q, k, v, la, q, k, v, la, s0f, s0b)


POOL_BLK = 256


def _pool_body(w, p_ref, wp_ref, ps_ref, o_ref, buf_ref, n_tok):
    lo = w // 2
    hi = w - 1 - lo
    pad = lo * GRID_W
    n_rows = n_tok // GRID_W
    blk = POOL_BLK
    i0 = lax.broadcasted_iota(jnp.int32, (blk, blk), 0)
    i1 = lax.broadcasted_iota(jnp.int32, (blk, blk), 1)
    same_row = (i0 >> GRID_W_LOG2) == (i1 >> GRID_W_LOG2)
    band = jnp.where(same_row & (i1 - i0 >= -lo) & (i1 - i0 <= hi), 1.0, 0.0).astype(BF16)
    zeros = jnp.zeros((GRID_W * 8, LANES), F32)
    buf_ref[0:pad, :] = zeros[0:pad]
    buf_ref[pad + n_tok:pad + n_tok + hi * GRID_W + GRID_W, :] = zeros[0:hi * GRID_W + GRID_W]
    wp = wp_ref[0].astype(BF16)
    scale = ps_ref[0]

    def col_step(t, carry):
        r = pl.multiple_of(t * blk, blk)
        x_hi, x_lo = _split_bf16(p_ref[0, pl.ds(r, blk), :])
        buf_ref[pl.ds(pad + r, blk), :] = _dot(band, x_hi) + _dot(band, x_lo)
        return carry

    lax.fori_loop(0, n_tok // blk, col_step, 0)

    def row_step(t, carry):
        r = pl.multiple_of(t * blk, blk)
        acc = buf_ref[pl.ds(r, blk), :]
        for d in range(1, w):
            acc = acc + buf_ref[pl.ds(r + d * GRID_W, blk), :]
        tok = r + lax.broadcasted_iota(jnp.int32, (blk, LANES), 0)
        g_row = tok >> GRID_W_LOG2
        g_col = tok & (GRID_W - 1)
        cnt_c = jnp.minimum(g_col + hi, GRID_W - 1) - jnp.maximum(g_col - lo, 0) + 1
        cnt_r = jnp.minimum(g_row + hi, n_rows - 1) - jnp.maximum(g_row - lo, 0) + 1
        m = acc / (cnt_c * cnt_r).astype(F32)
        y = (m - p_ref[0, pl.ds(r, blk), :]).astype(BF16)
        o_ref[0, pl.ds(r, blk), :] = (_dot(y, wp) * scale).astype(BF16)
        return carry

    lax.fori_loop(0, n_tok // blk, row_step, 0)


def _pool_kernel(p_ref, wp_ref, ps_ref, o_ref, buf_ref, *, n_tok):
    g = pl.program_id(1)
    for gi, w in enumerate(POOL_WINDOWS):
        @pl.when(g == gi)
        def _(w=w):
            _pool_body(w, p_ref, wp_ref, ps_ref, o_ref, buf_ref, n_tok)


def _pool(p, w_pool, pool_scale):
    b, l, _ = p.shape
    n_g = len(POOL_WINDOWS)
    max_w = max(POOL_WINDOWS)
    buf_rows = l + (max_w + 1) * GRID_W
    return pl.pallas_call(
        functools.partial(_pool_kernel, n_tok=l),
        grid=(b, n_g),
        in_specs=[pl.BlockSpec((1, l, POOL_GROUP), lambda bi, g: (bi, 0, g)),
                  pl.BlockSpec((1, POOL_GROUP, POOL_GROUP), lambda bi, g: (g, 0, 0)),
                  pl.BlockSpec((1, 1, POOL_GROUP), lambda bi, g: (g, 0, 0))],
        out_specs=pl.BlockSpec((1, l, POOL_GROUP), lambda bi, g: (bi, 0, g)),
        out_shape=jax.ShapeDtypeStruct((b, l, POOL_W), BF16),
        scratch_shapes=[pltpu.VMEM((buf_rows, LANES), F32)],
        compiler_params=_params("arbitrary", "arbitrary"),
        name="pool",
    )(p, w_pool, pool_scale.reshape(n_g, 1, POOL_GROUP))


ROUTER_ROWS = 48
ROUTER_GROUP_ROW0 = N_EXPERTS
INFO_GATE_HI, INFO_GATE_LO, INFO_GROUP = 0, EXPERTS_PER_GROUP, 2 * EXPERTS_PER_GROUP


def _outproj_kernel(of_ref, ob_ref, g_ref, pooled_ref, x_ref, gn_ref, wout_ref, gt_ref, sh_ref, sc_ref, g2_ref,
                    wr_hi_ref, wr_lo_ref, x1_ref, h2_ref, info_ref, cnt_ref):
    o = of_ref[0].astype(F32) + ob_ref[0].astype(F32)
    gate = _silu(g_ref[0].astype(F32))
    parts = []
    for h in range(GLA_HEADS):
        sl = slice(h * GLA_DV, (h + 1) * GLA_DV)
        parts.append((_rms(o[:, sl]) * gn_ref[...] * gate[:, sl]).astype(BF16))
    o_n = jnp.concatenate(parts, axis=-1)
    out = _dot(o_n, wout_ref[0:V_W, :]) + _dot(pooled_ref[0], wout_ref[V_W:V_W + POOL_W, :])
    x1 = x_ref[0] + gt_ref[0] * out
    x1_ref[0] = x1
    h2 = _rms(x1) * g2_ref[...]
    h2 = h2 * (1.0 + sc_ref[0]) + sh_ref[0]
    h2_bf = h2.astype(BF16)
    h2_ref[0] = h2_bf
    h2_lo = (h2 - h2_bf.astype(F32)).astype(BF16)
    logits = (lax.dot_general(wr_hi_ref[...], h2_bf, _NT, preferred_element_type=F32)
              + lax.dot_general(wr_hi_ref[...], h2_lo, _NT, preferred_element_type=F32)
              + lax.dot_general(wr_lo_ref[...], h2_bf, _NT, preferred_element_type=F32))
    tm = logits.shape[1]
    row = lax.broadcasted_iota(jnp.int32, logits.shape, 0)
    neg = -jnp.inf
    big = jnp.int32(1 << 20)
    is_group = (row >= ROUTER_GROUP_ROW0) & (row < ROUTER_GROUP_ROW0 + N_GROUPS)
    gl = jnp.where(is_group, logits, neg)
    g_max = jnp.max(gl, axis=0, keepdims=True)
    g_idx = jnp.min(jnp.where(gl == g_max, row - ROUTER_GROUP_ROW0, big), axis=0, keepdims=True)
    g_prob = 1.0 / jnp.sum(jnp.where(is_group, jnp.exp(gl - g_max), 0.0), axis=0, keepdims=True)
    n_e = EXPERTS_PER_GROUP
    el = logits[0:n_e]
    for g in range(1, N_GROUPS):
        el = jnp.where(g_idx == g, logits[g * n_e:(g + 1) * n_e], el)
    r8 = lax.broadcasted_iota(jnp.int32, el.shape, 0)
    m1 = jnp.max(el, axis=0, keepdims=True)
    i1 = jnp.min(jnp.where(el == m1, r8, big), axis=0, keepdims=True)
    el2 = jnp.where(r8 == i1, neg, el)
    m2 = jnp.max(el2, axis=0, keepdims=True)
    i2 = jnp.min(jnp.where(el2 == m2, r8, big), axis=0, keepdims=True)
    r = jnp.exp(m2 - m1)
    w1 = g_prob / (1.0 + r)
    w2 = g_prob * r / (1.0 + r)
    gates = jnp.where(r8 == i1, w1, 0.0) + jnp.where(r8 == i2, w2, 0.0)
    gates_hi = gates.astype(BF16).astype(F32)
    gates_lo = (gates - gates_hi).astype(BF16).astype(F32)
    info_t = jnp.concatenate([gates_hi, gates_lo, jnp.broadcast_to(g_idx.astype(F32), (n_e, tm)),
                              jnp.zeros((LANES - 3 * n_e, tm), F32)], axis=0)
    info_ref[0] = info_t.T
    group_count = jnp.sum(jnp.where(r8 == g_idx, 1.0, 0.0), axis=1, keepdims=True)
    cnt_ref[0, 0] = jnp.broadcast_to(group_count, (n_e, LANES)).astype(jnp.int32)


def _outproj(o_f, o_b, g, pooled, x, gn, w_out, gt_a, sh_m, sc_m, g2, wr_hi, wr_lo, tm):
    b, l, d = x.shape
    tok = lambda bi, i: (bi, i, 0)
    vec = lambda bi, i: (bi, 0, 0)
    fixed = lambda bi, i: (0, 0)
    return pl.pallas_call(
        _outproj_kernel,
        grid=(b, l // tm),
        in_specs=[pl.BlockSpec((1, tm, V_W), tok), pl.BlockSpec((1, tm, V_W), tok),
                  pl.BlockSpec((1, tm, V_W), tok), pl.BlockSpec((1, tm, POOL_W), tok),
                  pl.BlockSpec((1, tm, d), tok),
                  pl.BlockSpec((1, GLA_DV), fixed),
                  pl.BlockSpec(w_out.shape, fixed),
                  pl.BlockSpec((1, 1, d), vec), pl.BlockSpec((1, 1, d), vec), pl.BlockSpec((1, 1, d), vec),
                  pl.BlockSpec((1, d), fixed),
                  pl.BlockSpec(wr_hi.shape, fixed), pl.BlockSpec(wr_lo.shape, fixed)],
        out_specs=[pl.BlockSpec((1, tm, d), tok), pl.BlockSpec((1, tm, d), tok), pl.BlockSpec((1, tm, LANES), tok),
                   pl.BlockSpec((1, 1, EXPERTS_PER_GROUP, LANES), lambda bi, i: (bi, i, 0, 0))],
        out_shape=[jax.ShapeDtypeStruct((b, l, d), F32), jax.ShapeDtypeStruct((b, l, d), BF16),
                   jax.ShapeDtypeStruct((b, l, LANES), F32),
                   jax.ShapeDtypeStruct((b, l // tm, EXPERTS_PER_GROUP, LANES), jnp.int32)],
        compiler_params=_params("arbitrary", "arbitrary"),
        name="outproj_router",
    )(o_f, o_b, g, pooled, x, gn, w_out, gt_a, sh_m, sc_m, g2, wr_hi, wr_lo)


SORT_TM = 512
PIECE = 16
PIECE_LOG2 = 4
assert 1 << PIECE_LOG2 == PIECE
SORT_ROWS = 640
EXPERT_TM = 512
_TN = (((0,), (0,)), ((), ()))


def _sorted_tiles(n_tok):
    rows = n_tok + (n_tok // SORT_TM) * N_GROUPS * (PIECE - 1) + N_GROUPS * (EXPERT_TM - 1)
    return -(-rows // EXPERT_TM)


def _tile_pieces(n_ref, tile):
    n = n_ref[tile * N_GROUPS]
    for g in range(1, N_GROUPS):
        n = n + n_ref[tile * N_GROUPS + g]
    return n


def _sort_kernel(off_ref, n_ref, zoff_ref, zn_ref, h2_ref, info_ref, xs_hbm, dest_ref, stage, zbuf, sem, zsem):
    i = pl.program_id(0)
    n_steps = pl.num_programs(0)
    slot = i % 2
    tm = SORT_TM
    d = h2_ref.shape[1]

    def piece_copy(s, src_row, dst_row):
        return pltpu.make_async_copy(stage.at[s, pl.ds(src_row, PIECE), :], xs_hbm.at[pl.ds(dst_row, PIECE), :],
                                     sem.at[s])

    def wait_tile(tile, s):
        def body(j, carry):
            piece_copy(s, 0, 0).wait()
            return carry
        lax.fori_loop(0, _tile_pieces(n_ref, tile), body, 0)

    @pl.when(i >= 2)
    def _():
        wait_tile(i - 2, slot)

    info = info_ref[...]
    lane = lax.broadcasted_iota(jnp.int32, info.shape, 1)
    g_col = jnp.sum(jnp.where(lane == INFO_GROUP, info, 0.0), axis=-1, keepdims=True)
    onehot = jnp.where((lane < N_GROUPS) & (lane.astype(F32) == g_col), 1.0, 0.0)
    r_i = lax.broadcasted_iota(jnp.int32, (tm, tm), 0)
    c_i = lax.broadcasted_iota(jnp.int32, (tm, tm), 1)
    rank = _dot(jnp.where(c_i < r_i, 1.0, 0.0).astype(BF16), onehot.astype(BF16))
    cnt = jnp.sum(onehot, axis=0, keepdims=True).astype(jnp.int32)
    npad = (cnt + (PIECE - 1)) >> PIECE_LOG2
    l_r = lax.broadcasted_iota(jnp.int32, (LANES, LANES), 0)
    l_c = lax.broadcasted_iota(jnp.int32, (LANES, LANES), 1)
    npad_rows = jnp.broadcast_to(npad.astype(F32), (8, LANES)).astype(BF16)
    base = _dot(npad_rows, jnp.where(l_r < l_c, 1.0, 0.0).astype(BF16))[0:1] * PIECE
    dest = jnp.sum(onehot * (rank + base), axis=-1, keepdims=True)
    dest_ref[...] = jnp.broadcast_to(dest, (tm, LANES))
    slot_id = lax.broadcasted_iota(jnp.int32, (tm, SORT_ROWS), 1)
    perm = jnp.where(slot_id == dest.astype(jnp.int32), 1.0, 0.0).astype(BF16)
    meta = jnp.where(lane < INFO_GROUP, info, 0.0).astype(BF16)
    stage[slot, :, 0:d] = lax.dot_general(perm, h2_ref[...], _TN, preferred_element_type=F32).astype(BF16)
    stage[slot, :, d:d + LANES] = lax.dot_general(perm, meta, _TN, preferred_element_type=F32).astype(BF16)

    src = jnp.int32(0)
    for g in range(N_GROUPS):
        n_g = n_ref[i * N_GROUPS + g]
        dst0 = off_ref[i * N_GROUPS + g]

        def issue(j, carry, src=src, dst0=dst0):
            piece_copy(slot, pl.multiple_of(src + j * PIECE, PIECE), pl.multiple_of(dst0 + j * PIECE, PIECE)).start()
            return carry
        lax.fori_loop(0, n_g, issue, 0)
        src = src + n_g * PIECE

    @pl.when(i == n_steps - 1)
    def _():
        zbuf[...] = jnp.zeros_like(zbuf)

        def zero_copy(row):
            return pltpu.make_async_copy(zbuf, xs_hbm.at[pl.ds(row, PIECE), :], zsem)

        n_zero = jnp.int32(0)
        for g in range(N_GROUPS):
            def issue_zero(j, carry, g=g):
                zero_copy(pl.multiple_of(zoff_ref[g] + j * PIECE, PIECE)).start()
                return carry
            lax.fori_loop(0, zn_ref[g], issue_zero, 0)
            n_zero = n_zero + zn_ref[g]

        def wait_zero(j, carry):
            zero_copy(0).wait()
            return carry
        lax.fori_loop(0, n_zero, wait_zero, 0)

        @pl.when(n_steps >= 2)
        def _():
            wait_tile(i - 1, 1 - slot)
        wait_tile(i, slot)


def _sort_tokens(h2, info, off, n_pieces, zero_off, zero_n, n_rows):
    t, d = h2.shape
    width = d + LANES
    tok = lambda i, *_: (i, 0)
    return pl.pallas_call(
        _sort_kernel,
        grid_spec=pltpu.PrefetchScalarGridSpec(
            num_scalar_prefetch=4,
            grid=(t // SORT_TM,),
            in_specs=[pl.BlockSpec((SORT_TM, d), tok), pl.BlockSpec((SORT_TM, LANES), tok)],
            out_specs=[pl.BlockSpec(memory_space=pl.ANY), pl.BlockSpec((SORT_TM, LANES), tok)],
            scratch_shapes=[pltpu.VMEM((2, SORT_ROWS, width), BF16), pltpu.VMEM((PIECE, width), BF16),
                            pltpu.SemaphoreType.DMA((2,)), pltpu.SemaphoreType.DMA(())]),
        out_shape=[jax.ShapeDtypeStruct((n_rows, width), BF16), jax.ShapeDtypeStruct((t, LANES), F32)],
        compiler_params=_params("arbitrary"),
        name="moe_sort",
    )(off, n_pieces, zero_off, zero_n, h2, info)


def _experts_kernel(grp_ref, valid_ref, xs_ref, wi_ref, wo_ref, ys_ref):
    i = pl.program_id(0)
    d = ys_ref.shape[1]

    @pl.when(valid_ref[i] == 1)
    def _():
        x = xs_ref[:, 0:d]
        meta = xs_ref[:, d:d + LANES].astype(F32)
        lane = lax.broadcasted_iota(jnp.int32, meta.shape, 1)
        hidden = []
        for e in range(EXPERTS_PER_GROUP):
            au = _dot(x, wi_ref[0, e])
            sel = (lane == INFO_GATE_HI + e) | (lane == INFO_GATE_LO + e)
            gate = jnp.sum(jnp.where(sel, meta, 0.0), axis=-1, keepdims=True)
            hidden.append((_silu(au[:, :D_EXPERT]) * au[:, D_EXPERT:] * gate).astype(BF16))
        ys_ref[...] = _dot(jnp.concatenate(hidden, axis=-1), wo_ref[0]).astype(BF16)

    @pl.when(valid_ref[i] == 0)
    def _():
        ys_ref[...] = jnp.zeros_like(ys_ref)


def _experts(xs, tile_grp, tile_valid, wi, wo):
    rows, width = xs.shape
    d = wo.shape[2]
    return pl.pallas_call(
        _experts_kernel,
        grid_spec=pltpu.PrefetchScalarGridSpec(
            num_scalar_prefetch=2,
            grid=(rows // EXPERT_TM,),
            in_specs=[pl.BlockSpec((EXPERT_TM, width), lambda i, grp, valid: (i, 0)),
                      pl.BlockSpec((1,) + wi.shape[1:], lambda i, grp, valid: (grp[i], 0, 0, 0)),
                      pl.BlockSpec((1,) + wo.shape[1:], lambda i, grp, valid: (grp[i], 0, 0))],
            out_specs=pl.BlockSpec((EXPERT_TM, d), lambda i, grp, valid: (i, 0))),
        out_shape=jax.ShapeDtypeStruct((rows, d), BF16),
        compiler_params=_params("arbitrary"),
        name="moe_experts",
    )(tile_grp, tile_valid, xs, wi, wo)


def _combine_kernel(off_ref, n_ref, ys_hbm, dest_ref, x1_ref, gt_ref, gf_ref, o_ref, stage, sem):
    i = pl.program_id(0)
    n_steps = pl.num_programs(0)
    tm = SORT_TM

    def piece_copy(s, src_row, dst_row):
        return pltpu.make_async_copy(ys_hbm.at[pl.ds(src_row, PIECE), :], stage.at[s, pl.ds(dst_row, PIECE), :],
                                     sem.at[s])

    def fetch(tile, s):
        dst = jnp.int32(0)
        for g in range(N_GROUPS):
            n_g = n_ref[tile * N_GROUPS + g]
            src0 = off_ref[tile * N_GROUPS + g]

            def issue(j, carry, src0=src0, dst=dst):
                piece_copy(s, pl.multiple_of(src0 + j * PIECE, PIECE), pl.multiple_of(dst + j * PIECE, PIECE)).start()
                return carry
            lax.fori_loop(0, n_g, issue, 0)
            dst = dst + n_g * PIECE

    @pl.when(i == 0)
    def _():
        fetch(0, 0)

    @pl.when(i + 1 < n_steps)
    def _():
        fetch(i + 1, (i + 1) % 2)

    slot = i % 2
    n_tile = _tile_pieces(n_ref, i)

    def wait_body(j, carry):
        piece_copy(slot, 0, 0).wait()
        return carry
    lax.fori_loop(0, n_tile, wait_body, 0)

    ys = stage[slot]
    row_id = lax.broadcasted_iota(jnp.int32, ys.shape, 0)
    ys = jnp.where(row_id < n_tile * PIECE, ys, jnp.zeros_like(ys))
    slot_id = lax.broadcasted_iota(jnp.int32, (tm, SORT_ROWS), 1)
    perm = jnp.where(slot_id == dest_ref[:, 0:1].astype(jnp.int32), 1.0, 0.0).astype(BF16)
    y = _dot(perm, ys)
    x2 = x1_ref[...] + gt_ref[0] * y
    o_ref[...] = _rms(x2) * gf_ref[...]


def _combine(ys, off, n_pieces, dest, x1, gt_m, gf, tiles_per_batch):
    t, d = x1.shape
    n_tiles = t // SORT_TM
    tok = lambda i, off, n: (i, 0)
    return pl.pallas_call(
        _combine_kernel,
        grid_spec=pltpu.PrefetchScalarGridSpec(
            num_scalar_prefetch=2,
            grid=(n_tiles,),
            in_specs=[pl.BlockSpec(memory_space=pl.ANY),
                      pl.BlockSpec((SORT_TM, LANES), tok),
                      pl.BlockSpec((SORT_TM, d), tok),
                      pl.BlockSpec((1, 1, d), lambda i, off, n: (i // tiles_per_batch, 0, 0)),
                      pl.BlockSpec((1, d), lambda i, off, n: (0, 0))],
            out_specs=pl.BlockSpec((SORT_TM, d), tok),
            scratch_shapes=[pltpu.VMEM((2, SORT_ROWS, d), BF16), pltpu.SemaphoreType.DMA((2,))]),
        out_shape=jax.ShapeDtypeStruct((t, d), F32),
        compiler_params=_params("arbitrary"),
        name="moe_combine",
    )(off, n_pieces, ys, dest, x1, gt_m, gf)


def _sparse_moe(h2, info, counts, x1, wi, wo, gt_m, gf):
    b, l, d = x1.shape
    t = b * l
    i32 = jnp.int32

    cnt = counts[:, :, :N_GROUPS, 0].reshape(t // SORT_TM, N_GROUPS)
    n_pieces = (cnt + (PIECE - 1)) // PIECE
    rows = n_pieces * PIECE
    total = jnp.sum(rows, axis=0)
    tiles_g = (total + EXPERT_TM - 1) // EXPERT_TM
    ends = jnp.cumsum(tiles_g)
    starts = (ends - tiles_g) * EXPERT_TM
    off = jnp.cumsum(rows, axis=0) - rows + starts[None, :]
    n_sorted_tiles = _sorted_tiles(t)
    region_end = jnp.concatenate([starts[1:], jnp.array([n_sorted_tiles * EXPERT_TM], i32)])
    zero_off = starts + total
    zero_n = (region_end - zero_off) // PIECE
    step = jnp.arange(n_sorted_tiles, dtype=i32)
    grp = jnp.minimum(jnp.sum((step[:, None] >= ends[None, :]).astype(i32), axis=1), N_GROUPS - 1)
    valid = (step < ends[-1]).astype(i32)

    flat = lambda a: a.reshape(-1).astype(i32)
    xs, dest = _sort_tokens(h2.reshape(t, d), info.reshape(t, LANES), flat(off), flat(n_pieces), flat(zero_off),
                            flat(zero_n), n_sorted_tiles * EXPERT_TM)
    wi4 = wi.reshape(N_GROUPS, EXPERTS_PER_GROUP, d, 2 * D_EXPERT)
    wo4 = wo.reshape(N_GROUPS, EXPERTS_PER_GROUP * D_EXPERT, d)
    ys = _experts(xs, grp, valid, wi4, wo4)
    out = _combine(ys, flat(off), flat(n_pieces), dest, x1.reshape(t, d), gt_m, gf, l // SORT_TM)
    return out.reshape(b, l, d)


def kernel(x, c, ctx, c_ctx, w_ada, b_ada, norm1_g, w_in, w_decay, b_decay, gla_norm_g, w_pool, pool_scale, w_out,
           norm2_g, w_router_group, w_router_expert, w_expert_in, w_expert_out, final_norm_g):
    assert w_ada.shape[0] == 1, "single-layer trunk"
    b, l, d = x.shape
    off_a = 2 * QK_W + 2 * V_W
    a_cols = 2 * GATE_RANK
    off_p = off_a + a_cols

    c8 = jnp.zeros((8, d), F32).at[:b].set(c).at[b].set(c_ctx)
    mod = _ada_mod(c8, w_ada[0], b_ada[0])
    sh_a, sc_a, gt_a, sh_m, sc_m, gt_m = [m.reshape(8, 1, d) for m in jnp.split(mod, 6, axis=-1)]
    ctx_rows = lambda m: jnp.broadcast_to(m[b:b + 1], (b, 1, d))

    w = w_in[0]
    w_r = jnp.concatenate([w[:, :off_a], w[:, off_p:off_p + POOL_W], w[:, off_a:off_p],
                           jnp.zeros((d, LANES - a_cols), F32)], axis=1).astype(BF16)
    wdec = jnp.zeros((LANES, 2 * QK_W), F32)
    wdec = wdec.at[:GATE_RANK, :QK_W].set(w_decay[0, 0]).at[GATE_RANK:a_cols, QK_W:].set(w_decay[0, 1]).astype(BF16)
    bdec = b_decay[0].reshape(1, 2 * QK_W)
    g1 = norm1_g[0].reshape(1, d)

    q_c, k_c, v_c, _, la_c, _ = _inproj(ctx, ctx_rows(sh_a), ctx_rows(sc_a), g1, w_r, wdec, bdec, tm=ctx.shape[1])
    zero_s = jnp.zeros((b, QK_W, GLA_DV), F32)
    _, _, s_f, s_b = _gla(q_c, k_c, v_c, la_c, zero_s, zero_s, tb=ctx.shape[1])

    q, k, v, g, la, p = _inproj(x, sh_a[:b], sc_a[:b], g1, w_r, wdec, bdec, tm=1024)
    o_f, o_b, _, _ = _gla(q, k, v, la, s_f, s_b, tb=1024)
    pooled = _pool(p, w_pool[0], pool_scale[0])

    w_router = jnp.zeros((ROUTER_ROWS, d), F32)
    w_router = w_router.at[:N_EXPERTS].set(w_router_expert[0].T)
    w_router = w_router.at[ROUTER_GROUP_ROW0:ROUTER_GROUP_ROW0 + N_GROUPS].set(w_router_group[0].T)
    wr_hi = w_router.astype(BF16)
    wr_lo = (w_router - wr_hi.astype(F32)).astype(BF16)
    x1, h2, info, counts = _outproj(o_f, o_b, g, pooled, x, gla_norm_g[0].reshape(1, GLA_DV), w_out[0].astype(BF16),
                                    gt_a[:b], sh_m[:b], sc_m[:b], norm2_g[0].reshape(1, d), wr_hi, wr_lo,
                                    tm=SORT_TM)

    return _sparse_moe(h2, info, counts, x1, w_expert_in[0].astype(BF16), w_expert_out[0].astype(BF16), gt_m[:b],
                       final_norm_g.reshape(1, d))
```

```python
import functools

import jax
import jax.numpy as jnp
from jax import lax
from jax.experimental import pallas as pl
from jax.experimental.pallas import tpu as pltpu

F32 = jnp.float32
BF16 = jnp.bfloat16

GRID_W = 64
GRID_W_LOG2 = 6
assert 1 << GRID_W_LOG2 == GRID_W
GLA_HEADS = 4
GLA_DK = 64
GLA_DV = 128
GATE_RANK = 16
GATE_NORMALIZER = 16.0
CHUNK = 64
POOL_WINDOWS = (2, 4, 8, 16)
POOL_GROUP = 128
N_GROUPS = 4
EXPERTS_PER_GROUP = 8
N_EXPERTS = N_GROUPS * EXPERTS_PER_GROUP
D_EXPERT = 256
EPS = 1e-6

QK_W = GLA_HEADS * GLA_DK
V_W = GLA_HEADS * GLA_DV
POOL_W = POOL_GROUP * len(POOL_WINDOWS)
LANES = 128
VMEM_LIMIT = 48 * 1024 * 1024

_NT = (((1,), (1,)), ((), ()))


def _dot(a, b):
    return jnp.dot(a, b, preferred_element_type=F32)


def _split_bf16(x):
    hi = x.astype(BF16)
    lo = (x - hi.astype(F32)).astype(BF16)
    return hi, lo


def _rms(x):
    return x * lax.rsqrt(jnp.mean(x * x, axis=-1, keepdims=True) + EPS)


def _silu(x):
    return x / (1.0 + jnp.exp(-x))


def _params(*sem):
    return pltpu.CompilerParams(dimension_semantics=sem, vmem_limit_bytes=VMEM_LIMIT)


def _ada_kernel(c_ref, w_ref, b_ref, o_ref):
    s = _silu(c_ref[...]).astype(BF16)
    o_ref[...] = _dot(s, w_ref[...].astype(BF16)) + b_ref[...]


def _ada_mod(c8, w_ada, b_ada):
    rows, d = c8.shape
    n = w_ada.shape[1]
    tn = 1024
    return pl.pallas_call(
        _ada_kernel,
        grid=(n // tn,),
        in_specs=[pl.BlockSpec((rows, d), lambda j: (0, 0)),
                  pl.BlockSpec((d, tn), lambda j: (0, j)),
                  pl.BlockSpec((1, tn), lambda j: (0, j))],
        out_specs=pl.BlockSpec((rows, tn), lambda j: (0, j)),
        out_shape=jax.ShapeDtypeStruct((rows, n), F32),
        compiler_params=_params("arbitrary"),
        name="ada_mod",
    )(c8, w_ada, b_ada.reshape(1, n))


def _inproj_kernel(x_ref, sh_ref, sc_ref, g1_ref, w_ref, wdec_ref, bdec_ref,
                   q_ref, k_ref, v_ref, g_ref, la_ref, p_ref):
    x = x_ref[0]
    h = _rms(x) * g1_ref[...]
    h = h * (1.0 + sc_ref[0]) + sh_ref[0]
    hb = h.astype(BF16)
    o = 0
    q_ref[0] = (_dot(hb, w_ref[:, o:o + QK_W]) * (GLA_DK ** -0.5)).astype(BF16)
    o += QK_W
    k_ref[0] = _dot(hb, w_ref[:, o:o + QK_W]).astype(BF16)
    o += QK_W
    v_ref[0] = _dot(hb, w_ref[:, o:o + V_W]).astype(BF16)
    o += V_W
    g_ref[0] = _dot(hb, w_ref[:, o:o + V_W]).astype(BF16)
    o += V_W
    p_ref[0] = _dot(hb, w_ref[:, o:o + POOL_W])
    o += POOL_W
    a_low = _dot(hb, w_ref[:, o:o + LANES])
    z = _dot(a_low.astype(BF16), wdec_ref[...]) + bdec_ref[...]
    log_sig = jnp.minimum(z, 0.0) - jnp.log(1.0 + jnp.exp(-jnp.abs(z)))
    la_ref[0] = log_sig / GATE_NORMALIZER


def _inproj(x, shift, scale, g1, w_r, wdec, bdec, tm):
    b, l, d = x.shape
    wcols = w_r.shape[1]
    tok = lambda bi, i: (bi, i, 0)
    vec = lambda bi, i: (bi, 0, 0)
    fixed = lambda bi, i: (0, 0)
    outs = [(QK_W, BF16), (QK_W, BF16), (V_W, BF16), (V_W, BF16), (2 * QK_W, F32), (POOL_W, F32)]
    return pl.pallas_call(
        _inproj_kernel,
        grid=(b, l // tm),
        in_specs=[pl.BlockSpec((1, tm, d), tok),
                  pl.BlockSpec((1, 1, d), vec),
                  pl.BlockSpec((1, 1, d), vec),
                  pl.BlockSpec((1, d), fixed),
                  pl.BlockSpec((d, wcols), fixed),
                  pl.BlockSpec(wdec.shape, fixed),
                  pl.BlockSpec(bdec.shape, fixed)],
        out_specs=[pl.BlockSpec((1, tm, w), tok) for w, _ in outs],
        out_shape=[jax.ShapeDtypeStruct((b, l, w), dt) for w, dt in outs],
        compiler_params=_params("arbitrary", "arbitrary"),
        name="inproj",
    )(x, shift, scale, g1, w_r, wdec, bdec)


GLA_GROUP = 256
CHUNKS_PER_GROUP = GLA_GROUP // CHUNK
CHUNK_LOG2 = 6
assert 1 << CHUNK_LOG2 == CHUNK


def _head_masks(rows):
    lane = lax.broadcasted_iota(jnp.int32, (rows, QK_W), 1)
    return [(lane >= h * GLA_DK) & (lane < (h + 1) * GLA_DK) for h in range(GLA_HEADS)]


def _gla_bulk(d, g, reverse, q_ref, k_ref, v_ref, la_ref, qd_scr, oi_scr, ds_scr, dc_scr):
    n = GLA_GROUP
    r = pl.multiple_of(g * n, n)
    i0 = lax.broadcasted_iota(jnp.int32, (n, n), 0)
    i1 = lax.broadcasted_iota(jnp.int32, (n, n), 1)
    causal = ((i0 >> CHUNK_LOG2) == (i1 >> CHUNK_LOG2)) & ((i1 >= i0) if reverse else (i1 <= i0))
    tri = jnp.where(causal, 1.0, 0.0).astype(BF16)
    la_hi, la_lo = _split_bf16(la_ref[0, pl.ds(r, n), :])
    b = _dot(tri, la_hi) + _dot(tri, la_lo)
    last_row = 0 if reverse else CHUNK - 1
    b_end = b.reshape(CHUNKS_PER_GROUP, CHUNK, QK_W)[:, last_row:last_row + 1, :]
    b_last = jnp.broadcast_to(b_end, (CHUNKS_PER_GROUP, CHUNK, QK_W)).reshape(n, QK_W)
    q = q_ref[0, pl.ds(r, n), :].astype(F32)
    k = k_ref[0, pl.ds(r, n), :].astype(F32)
    qd = (q * jnp.exp(b)).astype(BF16)
    kd = (k * jnp.exp(-b)).astype(BF16)
    k_st = k * jnp.exp(b_last - b)
    qd_scr[d, pl.ds(r, n), :] = qd
    v = v_ref[0, pl.ds(r, n), :]
    for h, m in enumerate(_head_masks(n)):
        scores = lax.dot_general(jnp.where(m, qd, jnp.zeros_like(qd)), kd, _NT, preferred_element_type=F32)
        probs = jnp.where(causal, scores, 0.0).astype(BF16)
        oi_scr[d, pl.ds(r, n), h * GLA_DV:(h + 1) * GLA_DV] = _dot(probs, v[:, h * GLA_DV:(h + 1) * GLA_DV])
    decay = jnp.exp(b_end)
    for c in range(CHUNKS_PER_GROUP):
        ci = g * CHUNKS_PER_GROUP + c
        k_st_t = k_st[c * CHUNK:(c + 1) * CHUNK].T.astype(BF16)
        for h in range(GLA_HEADS):
            ds_scr[d, ci, h * GLA_DK:(h + 1) * GLA_DK, :] = _dot(
                k_st_t[h * GLA_DK:(h + 1) * GLA_DK], v[c * CHUNK:(c + 1) * CHUNK, h * GLA_DV:(h + 1) * GLA_DV])
        dc_scr[d, ci] = jnp.broadcast_to(decay[c], (LANES, QK_W)).T


def _gla_kernel(qf_ref, kf_ref, vf_ref, laf_ref, qb_ref, kb_ref, vb_ref, lab_ref, s0f_ref, s0b_ref,
                of_ref, ob_ref, sf_out_ref, sb_out_ref, s_ref, qd_scr, oi_scr, ds_scr, dc_scr, st_scr, *, n_chunks):
    j = pl.program_id(1)

    @pl.when(j == 0)
    def _():
        s_ref[0] = s0f_ref[0]
        s_ref[1] = s0b_ref[0]

    def bulk(g, carry):
        _gla_bulk(0, g, False, qf_ref, kf_ref, vf_ref, laf_ref, qd_scr, oi_scr, ds_scr, dc_scr)
        _gla_bulk(1, g, True, qb_ref, kb_ref, vb_ref, lab_ref, qd_scr, oi_scr, ds_scr, dc_scr)
        return carry
    lax.fori_loop(0, n_chunks // CHUNKS_PER_GROUP, bulk, 0)

    def recur(i, carry):
        for d, ci in ((0, i), (1, n_chunks - 1 - i)):
            s = s_ref[d]
            st_scr[d, ci] = s.astype(BF16)
            s_ref[d] = dc_scr[d, ci] * s + ds_scr[d, ci]
        return carry
    lax.fori_loop(0, n_chunks, recur, 0)

    masks = _head_masks(CHUNK)

    def inter(g, carry):
        for d, o_ref in ((0, of_ref), (1, ob_ref)):
            for c in range(CHUNKS_PER_GROUP):
                ci = g * CHUNKS_PER_GROUP + c
                r = pl.multiple_of(ci * CHUNK, CHUNK)
                qd = qd_scr[d, pl.ds(r, CHUNK), :]
                lhs = jnp.concatenate([jnp.where(m, qd, jnp.zeros_like(qd)) for m in masks], axis=0)
                from_state = _dot(lhs, st_scr[d, ci])
                for h in range(GLA_HEADS):
                    cols = slice(h * GLA_DV, (h + 1) * GLA_DV)
                    o_ref[0, pl.ds(r, CHUNK), cols] = (oi_scr[d, pl.ds(r, CHUNK), cols]
                                                       + from_state[h * CHUNK:(h + 1) * CHUNK]).astype(BF16)
        return carry
    lax.fori_loop(0, n_chunks // CHUNKS_PER_GROUP, inter, 0)

    @pl.when(j == pl.num_programs(1) - 1)
    def _():
        sf_out_ref[0] = s_ref[0]
        sb_out_ref[0] = s_ref[1]


def _gla(q, k, v, la, s0f, s0b, tb):
    b, l, _ = q.shape
    assert tb % GLA_GROUP == 0 and l % tb == 0
    nb = l // tb
    n_chunks = tb // CHUNK
    fwd = lambda bi, j: (bi, j, 0)
    bwd = lambda bi, j: (bi, nb - 1 - j, 0)
    fwd_la = lambda bi, j: (bi, j, 0)
    bwd_la = lambda bi, j: (bi, nb - 1 - j, 1)
    st = lambda bi, j: (bi, 0, 0)
    state_shape = (b, QK_W, GLA_DV)
    return pl.pallas_call(
        functools.partial(_gla_kernel, n_chunks=tb // CHUNK),
        grid=(b, nb),
        in_specs=[pl.BlockSpec((1, tb, QK_W), fwd), pl.BlockSpec((1, tb, QK_W), fwd),
                  pl.BlockSpec((1, tb, V_W), fwd), pl.BlockSpec((1, tb, QK_W), fwd_la),
                  pl.BlockSpec((1, tb, QK_W), bwd), pl.BlockSpec((1, tb, QK_W), bwd),
                  pl.BlockSpec((1, tb, V_W), bwd), pl.BlockSpec((1, tb, QK_W), bwd_la),
                  pl.BlockSpec((1, QK_W, GLA_DV), st), pl.BlockSpec((1, QK_W, GLA_DV), st)],
        out_specs=[pl.BlockSpec((1, tb, V_W), fwd), pl.BlockSpec((1, tb, V_W), bwd),
                   pl.BlockSpec((1, QK_W, GLA_DV), st), pl.BlockSpec((1, QK_W, GLA_DV), st)],
        out_shape=[jax.ShapeDtypeStruct((b, l, V_W), BF16), jax.ShapeDtypeStruct((b, l, V_W), BF16),
                   jax.ShapeDtypeStruct(state_shape, F32), jax.ShapeDtypeStruct(state_shape, F32)],
        scratch_shapes=[pltpu.VMEM((2, QK_W, GLA_DV), F32),
                        pltpu.VMEM((2, tb, QK_W), BF16),
                        pltpu.VMEM((2, tb, V_W), F32),
                        pltpu.VMEM((2, n_chunks, QK_W, GLA_DV), F32),
                        pltpu.VMEM((2, n_chunks, QK_W, GLA_DV), F32),
                        pltpu.VMEM((2, n_chunks, QK_W, GLA_DV), BF16)],
        compiler_params=_params("arbitrary", "arbitrary"),
        name="gla",
    )(q, k, v, la, q, k, v, la, s0f, s0b)


POOL_BLK = 256
POOL_COL_STEP = 1024
POOL_ROW_STEP = 256
POOL_OUT_STEP = 1024


def _pool_body(w, first_batch, p_ref, wp_ref, ps_ref, o_ref, buf_ref, inv_ref, y_ref, n_tok):
    lo = w // 2
    hi = w - 1 - lo
    pad = lo * GRID_W
    n_rows = n_tok // GRID_W
    blk = POOL_BLK

    @pl.when(first_batch)
    def _():
        zeros = jnp.zeros((GRID_W * 8, LANES), F32)
        buf_ref[0:pad, :] = zeros[0:pad]
        buf_ref[pad + n_tok:pad + n_tok + hi * GRID_W + GRID_W, :] = zeros[0:hi * GRID_W + GRID_W]

        def inv_step(t, carry):
            r = pl.multiple_of(t * POOL_OUT_STEP, POOL_OUT_STEP)
            tok = r + lax.broadcasted_iota(jnp.int32, (POOL_OUT_STEP, LANES), 0)
            g_row = tok >> GRID_W_LOG2
            g_col = tok & (GRID_W - 1)
            cnt_c = jnp.minimum(g_col + hi, GRID_W - 1) - jnp.maximum(g_col - lo, 0) + 1
            cnt_r = jnp.minimum(g_row + hi, n_rows - 1) - jnp.maximum(g_row - lo, 0) + 1
            inv_ref[pl.ds(r, POOL_OUT_STEP), :] = 1.0 / (cnt_c * cnt_r).astype(F32)
            return carry
        lax.fori_loop(0, n_tok // POOL_OUT_STEP, inv_step, 0)

    i0 = lax.broadcasted_iota(jnp.int32, (blk, blk), 0)
    i1 = lax.broadcasted_iota(jnp.int32, (blk, blk), 1)
    same_row = (i0 >> GRID_W_LOG2) == (i1 >> GRID_W_LOG2)
    band = jnp.where(same_row & (i1 - i0 >= -lo) & (i1 - i0 <= hi), 1.0, 0.0).astype(BF16)

    def col_step(t, carry):
        for u in range(POOL_COL_STEP // blk):
            r = pl.multiple_of(t * POOL_COL_STEP + u * blk, blk)
            x_hi, x_lo = _split_bf16(p_ref[0, pl.ds(r, blk), :])
            s = _dot(band, jnp.concatenate([x_hi, x_lo], axis=-1))
            buf_ref[pl.ds(pad + r, blk), :] = s[:, :LANES] + s[:, LANES:]
        return carry
    lax.fori_loop(0, n_tok // POOL_COL_STEP, col_step, 0)

    def row_step(t, carry):
        r = pl.multiple_of(t * POOL_ROW_STEP, POOL_ROW_STEP)
        acc = buf_ref[pl.ds(r, POOL_ROW_STEP), :]
        for d in range(1, w):
            acc = acc + buf_ref[pl.ds(r + d * GRID_W, POOL_ROW_STEP), :]
        m = acc * inv_ref[pl.ds(r, POOL_ROW_STEP), :]
        y_ref[pl.ds(r, POOL_ROW_STEP), :] = (m - p_ref[0, pl.ds(r, POOL_ROW_STEP), :]).astype(BF16)
        return carry
    lax.fori_loop(0, n_tok // POOL_ROW_STEP, row_step, 0)

    wp = wp_ref[0].astype(BF16)
    scale = ps_ref[0]

    def out_step(t, carry):
        r = pl.multiple_of(t * POOL_OUT_STEP, POOL_OUT_STEP)
        o_ref[0, pl.ds(r, POOL_OUT_STEP), :] = (_dot(y_ref[pl.ds(r, POOL_OUT_STEP), :], wp) * scale).astype(BF16)
        return carry
    lax.fori_loop(0, n_tok // POOL_OUT_STEP, out_step, 0)


def _pool_kernel(p_ref, wp_ref, ps_ref, o_ref, buf_ref, inv_ref, y_ref, *, n_tok):
    g = pl.program_id(0)
    first_batch = pl.program_id(1) == 0
    for gi, w in enumerate(POOL_WINDOWS):
        @pl.when(g == gi)
        def _(w=w):
            _pool_body(w, first_batch, p_ref, wp_ref, ps_ref, o_ref, buf_ref, inv_ref, y_ref, n_tok)


def _pool(p, w_pool, pool_scale):
    b, l, _ = p.shape
    n_g = len(POOL_WINDOWS)
    max_w = max(POOL_WINDOWS)
    buf_rows = l + (max_w + 1) * GRID_W
    return pl.pallas_call(
        functools.partial(_pool_kernel, n_tok=l),
        grid=(n_g, b),
        in_specs=[pl.BlockSpec((1, l, POOL_GROUP), lambda g, bi: (bi, 0, g)),
                  pl.BlockSpec((1, POOL_GROUP, POOL_GROUP), lambda g, bi: (g, 0, 0)),
                  pl.BlockSpec((1, 1, POOL_GROUP), lambda g, bi: (g, 0, 0))],
        out_specs=pl.BlockSpec((1, l, POOL_GROUP), lambda g, bi: (bi, 0, g)),
        out_shape=jax.ShapeDtypeStruct((b, l, POOL_W), BF16),
        scratch_shapes=[pltpu.VMEM((buf_rows, LANES), F32), pltpu.VMEM((l, LANES), F32),
                        pltpu.VMEM((l, LANES), BF16)],
        compiler_params=_params("arbitrary", "arbitrary"),
        name="pool",
    )(p, w_pool, pool_scale.reshape(n_g, 1, POOL_GROUP))


ROUTER_ROWS = 48
ROUTER_GROUP_ROW0 = N_EXPERTS
INFO_GATE_HI, INFO_GATE_LO, INFO_GROUP = 0, EXPERTS_PER_GROUP, 2 * EXPERTS_PER_GROUP


def _outproj_kernel(of_ref, ob_ref, g_ref, pooled_ref, x_ref, gn_ref, wout_ref, gt_ref, sh_ref, sc_ref, g2_ref,
                    wr_hi_ref, wr_lo_ref, x1_ref, h2_ref, info_ref, cnt_ref):
    o = of_ref[0].astype(F32) + ob_ref[0].astype(F32)
    gate = _silu(g_ref[0].astype(F32))
    parts = []
    for h in range(GLA_HEADS):
        sl = slice(h * GLA_DV, (h + 1) * GLA_DV)
        parts.append((_rms(o[:, sl]) * gn_ref[...] * gate[:, sl]).astype(BF16))
    o_n = jnp.concatenate(parts, axis=-1)
    out = _dot(o_n, wout_ref[0:V_W, :]) + _dot(pooled_ref[0], wout_ref[V_W:V_W + POOL_W, :])
    x1 = x_ref[0] + gt_ref[0] * out
    x1_ref[0] = x1
    h2 = _rms(x1) * g2_ref[...]
    h2 = h2 * (1.0 + sc_ref[0]) + sh_ref[0]
    h2_bf = h2.astype(BF16)
    h2_ref[0] = h2_bf
    h2_lo = (h2 - h2_bf.astype(F32)).astype(BF16)
    logits = (lax.dot_general(wr_hi_ref[...], h2_bf, _NT, preferred_element_type=F32)
              + lax.dot_general(wr_hi_ref[...], h2_lo, _NT, preferred_element_type=F32)
              + lax.dot_general(wr_lo_ref[...], h2_bf, _NT, preferred_element_type=F32))
    tm = logits.shape[1]
    row = lax.broadcasted_iota(jnp.int32, logits.shape, 0)
    neg = -jnp.inf
    big = jnp.int32(1 << 20)
    is_group = (row >= ROUTER_GROUP_ROW0) & (row < ROUTER_GROUP_ROW0 + N_GROUPS)
    gl = jnp.where(is_group, logits, neg)
    g_max = jnp.max(gl, axis=0, keepdims=True)
    g_idx = jnp.min(jnp.where(gl == g_max, row - ROUTER_GROUP_ROW0, big), axis=0, keepdims=True)
    g_prob = 1.0 / jnp.sum(jnp.where(is_group, jnp.exp(gl - g_max), 0.0), axis=0, keepdims=True)
    n_e = EXPERTS_PER_GROUP
    el = logits[0:n_e]
    for g in range(1, N_GROUPS):
        el = jnp.where(g_idx == g, logits[g * n_e:(g + 1) * n_e], el)
    r8 = lax.broadcasted_iota(jnp.int32, el.shape, 0)
    m1 = jnp.max(el, axis=0, keepdims=True)
    i1 = jnp.min(jnp.where(el == m1, r8, big), axis=0, keepdims=True)
    el2 = jnp.where(r8 == i1, neg, el)
    m2 = jnp.max(el2, axis=0, keepdims=True)
    i2 = jnp.min(jnp.where(el2 == m2, r8, big), axis=0, keepdims=True)
    r = jnp.exp(m2 - m1)
    w1 = g_prob / (1.0 + r)
    w2 = g_prob * r / (1.0 + r)
    gates = jnp.where(r8 == i1, w1, 0.0) + jnp.where(r8 == i2, w2, 0.0)
    gates_hi = gates.astype(BF16).astype(F32)
    gates_lo = (gates - gates_hi).astype(BF16).astype(F32)
    info_t = jnp.concatenate([gates_hi, gates_lo, jnp.broadcast_to(g_idx.astype(F32), (n_e, tm)),
                              jnp.zeros((LANES - 3 * n_e, tm), F32)], axis=0)
    info_ref[0] = info_t.T
    group_count = jnp.sum(jnp.where(r8 == g_idx, 1.0, 0.0), axis=1, keepdims=True)
    cnt_ref[0, 0] = jnp.broadcast_to(group_count, (n_e, LANES)).astype(jnp.int32)


def _outproj(o_f, o_b, g, pooled, x, gn, w_out, gt_a, sh_m, sc_m, g2, wr_hi, wr_lo, tm):
    b, l, d = x.shape
    tok = lambda bi, i: (bi, i, 0)
    vec = lambda bi, i: (bi, 0, 0)
    fixed = lambda bi, i: (0, 0)
    return pl.pallas_call(
        _outproj_kernel,
        grid=(b, l // tm),
        in_specs=[pl.BlockSpec((1, tm, V_W), tok), pl.BlockSpec((1, tm, V_W), tok),
                  pl.BlockSpec((1, tm, V_W), tok), pl.BlockSpec((1, tm, POOL_W), tok),
                  pl.BlockSpec((1, tm, d), tok),
                  pl.BlockSpec((1, GLA_DV), fixed),
                  pl.BlockSpec(w_out.shape, fixed),
                  pl.BlockSpec((1, 1, d), vec), pl.BlockSpec((1, 1, d), vec), pl.BlockSpec((1, 1, d), vec),
                  pl.BlockSpec((1, d), fixed),
                  pl.BlockSpec(wr_hi.shape, fixed), pl.BlockSpec(wr_lo.shape, fixed)],
        out_specs=[pl.BlockSpec((1, tm, d), tok), pl.BlockSpec((1, tm, d), tok), pl.BlockSpec((1, tm, LANES), tok),
                   pl.BlockSpec((1, 1, EXPERTS_PER_GROUP, LANES), lambda bi, i: (bi, i, 0, 0))],
        out_shape=[jax.ShapeDtypeStruct((b, l, d), F32), jax.ShapeDtypeStruct((b, l, d), BF16),
                   jax.ShapeDtypeStruct((b, l, LANES), F32),
                   jax.ShapeDtypeStruct((b, l // tm, EXPERTS_PER_GROUP, LANES), jnp.int32)],
        compiler_params=_params("arbitrary", "arbitrary"),
        name="outproj_router",
    )(o_f, o_b, g, pooled, x, gn, w_out, gt_a, sh_m, sc_m, g2, wr_hi, wr_lo)


SORT_TM = 512
PIECE = 16
PIECE_LOG2 = 4
assert 1 << PIECE_LOG2 == PIECE
SORT_ROWS = 640
EXPERT_TM = 512
_TN = (((0,), (0,)), ((), ()))


def _sorted_tiles(n_tok):
    rows = n_tok + (n_tok // SORT_TM) * N_GROUPS * (PIECE - 1) + N_GROUPS * (EXPERT_TM - 1)
    return -(-rows // EXPERT_TM)


def _tile_pieces(n_ref, tile):
    n = n_ref[tile * N_GROUPS]
    for g in range(1, N_GROUPS):
        n = n + n_ref[tile * N_GROUPS + g]
    return n


def _sort_kernel(off_ref, n_ref, zoff_ref, zn_ref, h2_ref, info_ref, xs_hbm, dest_ref, stage, zbuf, sem, zsem):
    i = pl.program_id(0)
    n_steps = pl.num_programs(0)
    slot = i % 2
    tm = SORT_TM
    d = h2_ref.shape[1]

    def piece_copy(s, src_row, dst_row):
        return pltpu.make_async_copy(stage.at[s, pl.ds(src_row, PIECE), :], xs_hbm.at[pl.ds(dst_row, PIECE), :],
                                     sem.at[s])

    def wait_tile(tile, s):
        def body(j, carry):
            piece_copy(s, 0, 0).wait()
            return carry
        lax.fori_loop(0, _tile_pieces(n_ref, tile), body, 0)

    @pl.when(i >= 2)
    def _():
        wait_tile(i - 2, slot)

    info = info_ref[...]
    lane = lax.broadcasted_iota(jnp.int32, info.shape, 1)
    g_col = jnp.sum(jnp.where(lane == INFO_GROUP, info, 0.0), axis=-1, keepdims=True)
    onehot = jnp.where((lane < N_GROUPS) & (lane.astype(F32) == g_col), 1.0, 0.0)
    r_i = lax.broadcasted_iota(jnp.int32, (tm, tm), 0)
    c_i = lax.broadcasted_iota(jnp.int32, (tm, tm), 1)
    rank = _dot(jnp.where(c_i < r_i, 1.0, 0.0).astype(BF16), onehot.astype(BF16))
    cnt = jnp.sum(onehot, axis=0, keepdims=True).astype(jnp.int32)
    npad = (cnt + (PIECE - 1)) >> PIECE_LOG2
    l_r = lax.broadcasted_iota(jnp.int32, (LANES, LANES), 0)
    l_c = lax.broadcasted_iota(jnp.int32, (LANES, LANES), 1)
    npad_rows = jnp.broadcast_to(npad.astype(F32), (8, LANES)).astype(BF16)
    base = _dot(npad_rows, jnp.where(l_r < l_c, 1.0, 0.0).astype(BF16))[0:1] * PIECE
    dest = jnp.sum(onehot * (rank + base), axis=-1, keepdims=True)
    dest_ref[...] = jnp.broadcast_to(dest, (tm, LANES))
    slot_id = lax.broadcasted_iota(jnp.int32, (tm, SORT_ROWS), 1)
    perm = jnp.where(slot_id == dest.astype(jnp.int32), 1.0, 0.0).astype(BF16)
    meta = jnp.where(lane < INFO_GROUP, info, 0.0).astype(BF16)
    stage[slot, :, 0:d] = lax.dot_general(perm, h2_ref[...], _TN, preferred_element_type=F32).astype(BF16)
    stage[slot, :, d:d + LANES] = lax.dot_general(perm, meta, _TN, preferred_element_type=F32).astype(BF16)

    src = jnp.int32(0)
    for g in range(N_GROUPS):
        n_g = n_ref[i * N_GROUPS + g]
        dst0 = off_ref[i * N_GROUPS + g]

        def issue(j, carry, src=src, dst0=dst0):
            piece_copy(slot, pl.multiple_of(src + j * PIECE, PIECE), pl.multiple_of(dst0 + j * PIECE, PIECE)).start()
            return carry
        lax.fori_loop(0, n_g, issue, 0)
        src = src + n_g * PIECE

    @pl.when(i == n_steps - 1)
    def _():
        zbuf[...] = jnp.zeros_like(zbuf)

        def zero_copy(row):
            return pltpu.make_async_copy(zbuf, xs_hbm.at[pl.ds(row, PIECE), :], zsem)

        n_zero = jnp.int32(0)
        for g in range(N_GROUPS):
            def issue_zero(j, carry, g=g):
                zero_copy(pl.multiple_of(zoff_ref[g] + j * PIECE, PIECE)).start()
                return carry
            lax.fori_loop(0, zn_ref[g], issue_zero, 0)
            n_zero = n_zero + zn_ref[g]

        def wait_zero(j, carry):
            zero_copy(0).wait()
            return carry
        lax.fori_loop(0, n_zero, wait_zero, 0)

        @pl.when(n_steps >= 2)
        def _():
            wait_tile(i - 1, 1 - slot)
        wait_tile(i, slot)


def _sort_tokens(h2, info, off, n_pieces, zero_off, zero_n, n_rows):
    t, d = h2.shape
    width = d + LANES
    tok = lambda i, *_: (i, 0)
    return pl.pallas_call(
        _sort_kernel,
        grid_spec=pltpu.PrefetchScalarGridSpec(
            num_scalar_prefetch=4,
            grid=(t // SORT_TM,),
            in_specs=[pl.BlockSpec((SORT_TM, d), tok), pl.BlockSpec((SORT_TM, LANES), tok)],
            out_specs=[pl.BlockSpec(memory_space=pl.ANY), pl.BlockSpec((SORT_TM, LANES), tok)],
            scratch_shapes=[pltpu.VMEM((2, SORT_ROWS, width), BF16), pltpu.VMEM((PIECE, width), BF16),
                            pltpu.SemaphoreType.DMA((2,)), pltpu.SemaphoreType.DMA(())]),
        out_shape=[jax.ShapeDtypeStruct((n_rows, width), BF16), jax.ShapeDtypeStruct((t, LANES), F32)],
        compiler_params=_params("arbitrary"),
        name="moe_sort",
    )(off, n_pieces, zero_off, zero_n, h2, info)


def _experts_kernel(grp_ref, valid_ref, xs_ref, wi_ref, wo_ref, ys_ref):
    i = pl.program_id(0)
    d = ys_ref.shape[1]

    @pl.when(valid_ref[i] == 1)
    def _():
        x = xs_ref[:, 0:d]
        meta = xs_ref[:, d:d + LANES].astype(F32)
        lane = lax.broadcasted_iota(jnp.int32, meta.shape, 1)
        hidden = []
        for e in range(EXPERTS_PER_GROUP):
            au = _dot(x, wi_ref[0, e])
            sel = (lane == INFO_GATE_HI + e) | (lane == INFO_GATE_LO + e)
            gate = jnp.sum(jnp.where(sel, meta, 0.0), axis=-1, keepdims=True)
            hidden.append((_silu(au[:, :D_EXPERT]) * au[:, D_EXPERT:] * gate).astype(BF16))
        ys_ref[...] = _dot(jnp.concatenate(hidden, axis=-1), wo_ref[0]).astype(BF16)

    @pl.when(valid_ref[i] == 0)
    def _():
        ys_ref[...] = jnp.zeros_like(ys_ref)


def _experts(xs, tile_grp, tile_valid, wi, wo):
    rows, width = xs.shape
    d = wo.shape[2]
    return pl.pallas_call(
        _experts_kernel,
        grid_spec=pltpu.PrefetchScalarGridSpec(
            num_scalar_prefetch=2,
            grid=(rows // EXPERT_TM,),
            in_specs=[pl.BlockSpec((EXPERT_TM, width), lambda i, grp, valid: (i, 0)),
                      pl.BlockSpec((1,) + wi.shape[1:], lambda i, grp, valid: (grp[i], 0, 0, 0)),
                      pl.BlockSpec((1,) + wo.shape[1:], lambda i, grp, valid: (grp[i], 0, 0))],
            out_specs=pl.BlockSpec((EXPERT_TM, d), lambda i, grp, valid: (i, 0))),
        out_shape=jax.ShapeDtypeStruct((rows, d), BF16),
        compiler_params=_params("arbitrary"),
        name="moe_experts",
    )(tile_grp, tile_valid, xs, wi, wo)


def _combine_kernel(off_ref, n_ref, ys_hbm, dest_ref, x1_ref, gt_ref, gf_ref, o_ref, stage, sem):
    i = pl.program_id(0)
    n_steps = pl.num_programs(0)
    tm = SORT_TM

    def piece_copy(s, src_row, dst_row):
        return pltpu.make_async_copy(ys_hbm.at[pl.ds(src_row, PIECE), :], stage.at[s, pl.ds(dst_row, PIECE), :],
                                     sem.at[s])

    def fetch(tile, s):
        dst = jnp.int32(0)
        for g in range(N_GROUPS):
            n_g = n_ref[tile * N_GROUPS + g]
            src0 = off_ref[tile * N_GROUPS + g]

            def issue(j, carry, src0=src0, dst=dst):
                piece_copy(s, pl.multiple_of(src0 + j * PIECE, PIECE), pl.multiple_of(dst + j * PIECE, PIECE)).start()
                return carry
            lax.fori_loop(0, n_g, issue, 0)
            dst = dst + n_g * PIECE

    @pl.when(i == 0)
    def _():
        fetch(0, 0)

    @pl.when(i + 1 < n_steps)
    def _():
        fetch(i + 1, (i + 1) % 2)

    slot = i % 2
    n_tile = _tile_pieces(n_ref, i)

    def wait_body(j, carry):
        piece_copy(slot, 0, 0).wait()
        return carry
    lax.fori_loop(0, n_tile, wait_body, 0)

    def zero_body(j, carry):
        stage[slot, pl.ds(pl.multiple_of(j * PIECE, PIECE), PIECE), :] = jnp.zeros((PIECE, stage.shape[2]), BF16)
        return carry
    lax.fori_loop(n_tile, SORT_ROWS // PIECE, zero_body, 0)

    ys = stage[slot]
    slot_id = lax.broadcasted_iota(jnp.int32, (tm, SORT_ROWS), 1)
    perm = jnp.where(slot_id == dest_ref[:, 0:1].astype(jnp.int32), 1.0, 0.0).astype(BF16)
    y = _dot(perm, ys)
    x2 = x1_ref[...] + gt_ref[0] * y
    o_ref[...] = _rms(x2) * gf_ref[...]


def _combine(ys, off, n_pieces, dest, x1, gt_m, gf, tiles_per_batch):
    t, d = x1.shape
    n_tiles = t // SORT_TM
    tok = lambda i, off, n: (i, 0)
    return pl.pallas_call(
        _combine_kernel,
        grid_spec=pltpu.PrefetchScalarGridSpec(
            num_scalar_prefetch=2,
            grid=(n_tiles,),
            in_specs=[pl.BlockSpec(memory_space=pl.ANY),
                      pl.BlockSpec((SORT_TM, LANES), tok),
                      pl.BlockSpec((SORT_TM, d), tok),
                      pl.BlockSpec((1, 1, d), lambda i, off, n: (i // tiles_per_batch, 0, 0)),
                      pl.BlockSpec((1, d), lambda i, off, n: (0, 0))],
            out_specs=pl.BlockSpec((SORT_TM, d), tok),
            scratch_shapes=[pltpu.VMEM((2, SORT_ROWS, d), BF16), pltpu.SemaphoreType.DMA((2,))]),
        out_shape=jax.ShapeDtypeStruct((t, d), F32),
        compiler_params=_params("arbitrary"),
        name="moe_combine",
    )(off, n_pieces, ys, dest, x1, gt_m, gf)


def _sparse_moe(h2, info, counts, x1, wi, wo, gt_m, gf):
    b, l, d = x1.shape
    t = b * l
    i32 = jnp.int32

    cnt = counts[:, :, :N_GROUPS, 0].reshape(t // SORT_TM, N_GROUPS)
    n_pieces = (cnt + (PIECE - 1)) // PIECE
    rows = n_pieces * PIECE
    total = jnp.sum(rows, axis=0)
    tiles_g = (total + EXPERT_TM - 1) // EXPERT_TM
    ends = jnp.cumsum(tiles_g)
    starts = (ends - tiles_g) * EXPERT_TM
    off = jnp.cumsum(rows, axis=0) - rows + starts[None, :]
    n_sorted_tiles = _sorted_tiles(t)
    region_end = jnp.concatenate([starts[1:], jnp.array([n_sorted_tiles * EXPERT_TM], i32)])
    zero_off = starts + total
    zero_n = (region_end - zero_off) // PIECE
    step = jnp.arange(n_sorted_tiles, dtype=i32)
    grp = jnp.minimum(jnp.sum((step[:, None] >= ends[None, :]).astype(i32), axis=1), N_GROUPS - 1)
    valid = (step < ends[-1]).astype(i32)

    flat = lambda a: a.reshape(-1).astype(i32)
    xs, dest = _sort_tokens(h2.reshape(t, d), info.reshape(t, LANES), flat(off), flat(n_pieces), flat(zero_off),
                            flat(zero_n), n_sorted_tiles * EXPERT_TM)
    wi4 = wi.reshape(N_GROUPS, EXPERTS_PER_GROUP, d, 2 * D_EXPERT)
    wo4 = wo.reshape(N_GROUPS, EXPERTS_PER_GROUP * D_EXPERT, d)
    ys = _experts(xs, grp, valid, wi4, wo4)
    out = _combine(ys, flat(off), flat(n_pieces), dest, x1.reshape(t, d), gt_m, gf, l // SORT_TM)
    return out.reshape(b, l, d)


def kernel(x, c, ctx, c_ctx, w_ada, b_ada, norm1_g, w_in, w_decay, b_decay, gla_norm_g, w_pool, pool_scale, w_out,
           norm2_g, w_router_group, w_router_expert, w_expert_in, w_expert_out, final_norm_g):
    assert w_ada.shape[0] == 1, "single-layer trunk"
    b, l, d = x.shape
    off_a = 2 * QK_W + 2 * V_W
    a_cols = 2 * GATE_RANK
    off_p = off_a + a_cols

    c8 = jnp.zeros((8, d), F32).at[:b].set(c).at[b].set(c_ctx)
    mod = _ada_mod(c8, w_ada[0], b_ada[0])
    sh_a, sc_a, gt_a, sh_m, sc_m, gt_m = [m.reshape(8, 1, d) for m in jnp.split(mod, 6, axis=-1)]
    ctx_rows = lambda m: jnp.broadcast_to(m[b:b + 1], (b, 1, d))

    w = w_in[0]
    w_r = jnp.concatenate([w[:, :off_a], w[:, off_p:off_p + POOL_W], w[:, off_a:off_p],
                           jnp.zeros((d, LANES - a_cols), F32)], axis=1).astype(BF16)
    wdec = jnp.zeros((LANES, 2 * QK_W), F32)
    wdec = wdec.at[:GATE_RANK, :QK_W].set(w_decay[0, 0]).at[GATE_RANK:a_cols, QK_W:].set(w_decay[0, 1]).astype(BF16)
    bdec = b_decay[0].reshape(1, 2 * QK_W)
    g1 = norm1_g[0].reshape(1, d)

    q_c, k_c, v_c, _, la_c, _ = _inproj(ctx, ctx_rows(sh_a), ctx_rows(sc_a), g1, w_r, wdec, bdec, tm=ctx.shape[1])
    zero_s = jnp.zeros((b, QK_W, GLA_DV), F32)
    _, _, s_f, s_b = _gla(q_c, k_c, v_c, la_c, zero_s, zero_s, tb=ctx.shape[1])

    q, k, v, g, la, p = _inproj(x, sh_a[:b], sc_a[:b], g1, w_r, wdec, bdec, tm=1024)
    o_f, o_b, _, _ = _gla(q, k, v, la, s_f, s_b, tb=1024)
    pooled = _pool(p, w_pool[0], pool_scale[0])

    w_router = jnp.zeros((ROUTER_ROWS, d), F32)
    w_router = w_router.at[:N_EXPERTS].set(w_router_expert[0].T)
    w_router = w_router.at[ROUTER_GROUP_ROW0:ROUTER_GROUP_ROW0 + N_GROUPS].set(w_router_group[0].T)
    wr_hi = w_router.astype(BF16)
    wr_lo = (w_router - wr_hi.astype(F32)).astype(BF16)
    x1, h2, info, counts = _outproj(o_f, o_b, g, pooled, x, gla_norm_g[0].reshape(1, GLA_DV), w_out[0].astype(BF16),
                                    gt_a[:b], sh_m[:b], sc_m[:b], norm2_g[0].reshape(1, d), wr_hi, wr_lo,
                                    tm=SORT_TM)

    return _sparse_moe(h2, info, counts, x1, w_expert_in[0].astype(BF16), w_expert_out[0].astype(BF16), gt_m[:b],
                       final_norm_g.reshape(1, d))
```

```python
import functools

import jax
import jax.numpy as jnp
from jax import lax
from jax.experimental import pallas as pl
from jax.experimental.pallas import tpu as pltpu

F32 = jnp.float32
BF16 = jnp.bfloat16

GRID_W = 64
GRID_W_LOG2 = 6
assert 1 << GRID_W_LOG2 == GRID_W
GLA_HEADS = 4
GLA_DK = 64
GLA_DV = 128
GATE_RANK = 16
GATE_NORMALIZER = 16.0
CHUNK = 64
POOL_WINDOWS = (2, 4, 8, 16)
POOL_GROUP = 128
N_GROUPS = 4
EXPERTS_PER_GROUP = 8
N_EXPERTS = N_GROUPS * EXPERTS_PER_GROUP
D_EXPERT = 256
EPS = 1e-6

QK_W = GLA_HEADS * GLA_DK
V_W = GLA_HEADS * GLA_DV
POOL_W = POOL_GROUP * len(POOL_WINDOWS)
LANES = 128
VMEM_LIMIT = 48 * 1024 * 1024

_NT = (((1,), (1,)), ((), ()))


def _dot(a, b):
    return jnp.dot(a, b, preferred_element_type=F32)


def _split_bf16(x):
    hi = x.astype(BF16)
    lo = (x - hi.astype(F32)).astype(BF16)
    return hi, lo


def _rms(x):
    return x * lax.rsqrt(jnp.mean(x * x, axis=-1, keepdims=True) + EPS)


def _silu(x):
    return x / (1.0 + jnp.exp(-x))


def _params(*sem):
    return pltpu.CompilerParams(dimension_semantics=sem, vmem_limit_bytes=VMEM_LIMIT)


def _ada_kernel(c_ref, w_ref, b_ref, o_ref):
    s = _silu(c_ref[...]).astype(BF16)
    o_ref[...] = _dot(s, w_ref[...].astype(BF16)) + b_ref[...]


def _ada_mod(c8, w_ada, b_ada):
    rows, d = c8.shape
    n = w_ada.shape[1]
    tn = 1024
    return pl.pallas_call(
        _ada_kernel,
        grid=(n // tn,),
        in_specs=[pl.BlockSpec((rows, d), lambda j: (0, 0)),
                  pl.BlockSpec((d, tn), lambda j: (0, j)),
                  pl.BlockSpec((1, tn), lambda j: (0, j))],
        out_specs=pl.BlockSpec((rows, tn), lambda j: (0, j)),
        out_shape=jax.ShapeDtypeStruct((rows, n), F32),
        compiler_params=_params("arbitrary"),
        name="ada_mod",
    )(c8, w_ada, b_ada.reshape(1, n))


def _inproj_kernel(x_ref, sh_ref, sc_ref, g1_ref, w_ref, wdec_ref, bdec_ref,
                   q_ref, k_ref, v_ref, g_ref, la_ref, p_ref):
    x = x_ref[0]
    h = _rms(x) * g1_ref[...]
    h = h * (1.0 + sc_ref[0]) + sh_ref[0]
    hb = h.astype(BF16)
    o = 0
    q_ref[0] = (_dot(hb, w_ref[:, o:o + QK_W]) * (GLA_DK ** -0.5)).astype(BF16)
    o += QK_W
    k_ref[0] = _dot(hb, w_ref[:, o:o + QK_W]).astype(BF16)
    o += QK_W
    v_ref[0] = _dot(hb, w_ref[:, o:o + V_W]).astype(BF16)
    o += V_W
    g_ref[0] = _dot(hb, w_ref[:, o:o + V_W]).astype(BF16)
    o += V_W
    p_ref[0] = _dot(hb, w_ref[:, o:o + POOL_W])
    o += POOL_W
    a_low = _dot(hb, w_ref[:, o:o + LANES])
    z = _dot(a_low.astype(BF16), wdec_ref[...]) + bdec_ref[...]
    log_sig = jnp.minimum(z, 0.0) - jnp.log(1.0 + jnp.exp(-jnp.abs(z)))
    la_ref[0] = log_sig / GATE_NORMALIZER


def _inproj(x, shift, scale, g1, w_r, wdec, bdec, tm):
    b, l, d = x.shape
    wcols = w_r.shape[1]
    tok = lambda bi, i: (bi, i, 0)
    vec = lambda bi, i: (bi, 0, 0)
    fixed = lambda bi, i: (0, 0)
    outs = [(QK_W, BF16), (QK_W, BF16), (V_W, BF16), (V_W, BF16), (2 * QK_W, F32), (POOL_W, F32)]
    return pl.pallas_call(
        _inproj_kernel,
        grid=(b, l // tm),
        in_specs=[pl.BlockSpec((1, tm, d), tok),
                  pl.BlockSpec((1, 1, d), vec),
                  pl.BlockSpec((1, 1, d), vec),
                  pl.BlockSpec((1, d), fixed),
                  pl.BlockSpec((d, wcols), fixed),
                  pl.BlockSpec(wdec.shape, fixed),
                  pl.BlockSpec(bdec.shape, fixed)],
        out_specs=[pl.BlockSpec((1, tm, w), tok) for w, _ in outs],
        out_shape=[jax.ShapeDtypeStruct((b, l, w), dt) for w, dt in outs],
        compiler_params=_params("arbitrary", "arbitrary"),
        name="inproj",
    )(x, shift, scale, g1, w_r, wdec, bdec)


GLA_GROUP = 256
CHUNKS_PER_GROUP = GLA_GROUP // CHUNK
CHUNK_LOG2 = 6
assert 1 << CHUNK_LOG2 == CHUNK


def _head_masks(rows):
    lane = lax.broadcasted_iota(jnp.int32, (rows, QK_W), 1)
    return [(lane >= h * GLA_DK) & (lane < (h + 1) * GLA_DK) for h in range(GLA_HEADS)]


def _gla_bulk(d, g, reverse, q_ref, k_ref, v_ref, la_ref, qd_scr, oi_scr, ds_scr, dc_scr):
    n = GLA_GROUP
    r = pl.multiple_of(g * n, n)
    i0 = lax.broadcasted_iota(jnp.int32, (n, n), 0)
    i1 = lax.broadcasted_iota(jnp.int32, (n, n), 1)
    causal = ((i0 >> CHUNK_LOG2) == (i1 >> CHUNK_LOG2)) & ((i1 >= i0) if reverse else (i1 <= i0))
    tri = jnp.where(causal, 1.0, 0.0).astype(BF16)
    la_hi, la_lo = _split_bf16(la_ref[0, pl.ds(r, n), :])
    b = _dot(tri, la_hi) + _dot(tri, la_lo)
    last_row = 0 if reverse else CHUNK - 1
    b_end = b.reshape(CHUNKS_PER_GROUP, CHUNK, QK_W)[:, last_row:last_row + 1, :]
    b_last = jnp.broadcast_to(b_end, (CHUNKS_PER_GROUP, CHUNK, QK_W)).reshape(n, QK_W)
    q = q_ref[0, pl.ds(r, n), :].astype(F32)
    k = k_ref[0, pl.ds(r, n), :].astype(F32)
    qd = (q * jnp.exp(b)).astype(BF16)
    kd = (k * jnp.exp(-b)).astype(BF16)
    k_st = k * jnp.exp(b_last - b)
    qd_scr[d, pl.ds(r, n), :] = qd
    v = v_ref[0, pl.ds(r, n), :]
    for h, m in enumerate(_head_masks(n)):
        scores = lax.dot_general(jnp.where(m, qd, jnp.zeros_like(qd)), kd, _NT, preferred_element_type=F32)
        probs = jnp.where(causal, scores, 0.0).astype(BF16)
        oi_scr[d, pl.ds(r, n), h * GLA_DV:(h + 1) * GLA_DV] = _dot(probs, v[:, h * GLA_DV:(h + 1) * GLA_DV])
    decay = jnp.exp(b_end)
    for c in range(CHUNKS_PER_GROUP):
        ci = g * CHUNKS_PER_GROUP + c
        k_st_t = k_st[c * CHUNK:(c + 1) * CHUNK].T.astype(BF16)
        for h in range(GLA_HEADS):
            ds_scr[d, ci, h * GLA_DK:(h + 1) * GLA_DK, :] = _dot(
                k_st_t[h * GLA_DK:(h + 1) * GLA_DK], v[c * CHUNK:(c + 1) * CHUNK, h * GLA_DV:(h + 1) * GLA_DV])
        dc_scr[d, ci] = jnp.broadcast_to(decay[c], (LANES, QK_W)).T


def _gla_kernel(qf_ref, kf_ref, vf_ref, laf_ref, qb_ref, kb_ref, vb_ref, lab_ref, s0f_ref, s0b_ref,
                of_ref, ob_ref, sf_out_ref, sb_out_ref, s_ref, qd_scr, oi_scr, ds_scr, dc_scr, st_scr, *, n_chunks):
    j = pl.program_id(1)

    @pl.when(j == 0)
    def _():
        s_ref[0] = s0f_ref[0]
        s_ref[1] = s0b_ref[0]

    def bulk(g, carry):
        _gla_bulk(0, g, False, qf_ref, kf_ref, vf_ref, laf_ref, qd_scr, oi_scr, ds_scr, dc_scr)
        _gla_bulk(1, g, True, qb_ref, kb_ref, vb_ref, lab_ref, qd_scr, oi_scr, ds_scr, dc_scr)
        return carry
    lax.fori_loop(0, n_chunks // CHUNKS_PER_GROUP, bulk, 0)

    def recur(i, carry):
        for d, ci in ((0, i), (1, n_chunks - 1 - i)):
            s = s_ref[d]
            st_scr[d, ci] = s.astype(BF16)
            s_ref[d] = dc_scr[d, ci] * s + ds_scr[d, ci]
        return carry
    lax.fori_loop(0, n_chunks, recur, 0)

    masks = _head_masks(CHUNK)

    def inter(g, carry):
        for d, o_ref in ((0, of_ref), (1, ob_ref)):
            for c in range(CHUNKS_PER_GROUP):
                ci = g * CHUNKS_PER_GROUP + c
                r = pl.multiple_of(ci * CHUNK, CHUNK)
                qd = qd_scr[d, pl.ds(r, CHUNK), :]
                lhs = jnp.concatenate([jnp.where(m, qd, jnp.zeros_like(qd)) for m in masks], axis=0)
                from_state = _dot(lhs, st_scr[d, ci])
                for h in range(GLA_HEADS):
                    cols = slice(h * GLA_DV, (h + 1) * GLA_DV)
                    o_ref[0, pl.ds(r, CHUNK), cols] = (oi_scr[d, pl.ds(r, CHUNK), cols]
                                                       + from_state[h * CHUNK:(h + 1) * CHUNK]).astype(BF16)
        return carry
    lax.fori_loop(0, n_chunks // CHUNKS_PER_GROUP, inter, 0)

    @pl.when(j == pl.num_programs(1) - 1)
    def _():
        sf_out_ref[0] = s_ref[0]
        sb_out_ref[0] = s_ref[1]


def _gla(q, k, v, la, s0f, s0b, tb):
    b, l, _ = q.shape
    assert tb % GLA_GROUP == 0 and l % tb == 0
    nb = l // tb
    n_chunks = tb // CHUNK
    fwd = lambda bi, j: (bi, j, 0)
    bwd = lambda bi, j: (bi, nb - 1 - j, 0)
    fwd_la = lambda bi, j: (bi, j, 0)
    bwd_la = lambda bi, j: (bi, nb - 1 - j, 1)
    st = lambda bi, j: (bi, 0, 0)
    state_shape = (b, QK_W, GLA_DV)
    return pl.pallas_call(
        functools.partial(_gla_kernel, n_chunks=tb // CHUNK),
        grid=(b, nb),
        in_specs=[pl.BlockSpec((1, tb, QK_W), fwd), pl.BlockSpec((1, tb, QK_W), fwd),
                  pl.BlockSpec((1, tb, V_W), fwd), pl.BlockSpec((1, tb, QK_W), fwd_la),
                  pl.BlockSpec((1, tb, QK_W), bwd), pl.BlockSpec((1, tb, QK_W), bwd),
                  pl.BlockSpec((1, tb, V_W), bwd), pl.BlockSpec((1, tb, QK_W), bwd_la),
                  pl.BlockSpec((1, QK_W, GLA_DV), st), pl.BlockSpec((1, QK_W, GLA_DV), st)],
        out_specs=[pl.BlockSpec((1, tb, V_W), fwd), pl.BlockSpec((1, tb, V_W), bwd),
                   pl.BlockSpec((1, QK_W, GLA_DV), st), pl.BlockSpec((1, QK_W, GLA_DV), st)],
        out_shape=[jax.ShapeDtypeStruct((b, l, V_W), BF16), jax.ShapeDtypeStruct((b, l, V_W), BF16),
                   jax.ShapeDtypeStruct(state_shape, F32), jax.ShapeDtypeStruct(state_shape, F32)],
        scratch_shapes=[pltpu.VMEM((2, QK_W, GLA_DV), F32),
                        pltpu.VMEM((2, tb, QK_W), BF16),
                        pltpu.VMEM((2, tb, V_W), F32),
                        pltpu.VMEM((2, n_chunks, QK_W, GLA_DV), F32),
                        pltpu.VMEM((2, n_chunks, QK_W, GLA_DV), F32),
                        pltpu.VMEM((2, n_chunks, QK_W, GLA_DV), BF16)],
        compiler_params=_params("arbitrary", "arbitrary"),
        name="gla",
    )(q, k, v, la, q, k, v, la, s0f, s0b)


POOL_BLK = 256
POOL_COL_STEP = 1024
POOL_ROW_STEP = 256
POOL_OUT_STEP = 1024


def _pool_body(w, first_batch, p_ref, wp_ref, ps_ref, o_ref, buf_ref, inv_ref, y_ref, n_tok):
    lo = w // 2
    hi = w - 1 - lo
    pad = lo * GRID_W
    n_rows = n_tok // GRID_W
    blk = POOL_BLK

    @pl.when(first_batch)
    def _():
        zeros = jnp.zeros((GRID_W * 8, LANES), F32)
        buf_ref[0:pad, :] = zeros[0:pad]
        buf_ref[pad + n_tok:pad + n_tok + hi * GRID_W + GRID_W, :] = zeros[0:hi * GRID_W + GRID_W]

        def inv_step(t, carry):
            r = pl.multiple_of(t * POOL_OUT_STEP, POOL_OUT_STEP)
            tok = r + lax.broadcasted_iota(jnp.int32, (POOL_OUT_STEP, LANES), 0)
            g_row = tok >> GRID_W_LOG2
            g_col = tok & (GRID_W - 1)
            cnt_c = jnp.minimum(g_col + hi, GRID_W - 1) - jnp.maximum(g_col - lo, 0) + 1
            cnt_r = jnp.minimum(g_row + hi, n_rows - 1) - jnp.maximum(g_row - lo, 0) + 1
            inv_ref[pl.ds(r, POOL_OUT_STEP), :] = 1.0 / (cnt_c * cnt_r).astype(F32)
            return carry
        lax.fori_loop(0, n_tok // POOL_OUT_STEP, inv_step, 0)

    i0 = lax.broadcasted_iota(jnp.int32, (blk, blk), 0)
    i1 = lax.broadcasted_iota(jnp.int32, (blk, blk), 1)
    same_row = (i0 >> GRID_W_LOG2) == (i1 >> GRID_W_LOG2)
    band = jnp.where(same_row & (i1 - i0 >= -lo) & (i1 - i0 <= hi), 1.0, 0.0).astype(BF16)

    def col_step(t, carry):
        for u in range(POOL_COL_STEP // blk):
            r = pl.multiple_of(t * POOL_COL_STEP + u * blk, blk)
            x_hi, x_lo = _split_bf16(p_ref[0, pl.ds(r, blk), :])
            s = _dot(band, jnp.concatenate([x_hi, x_lo], axis=-1))
            buf_ref[pl.ds(pad + r, blk), :] = s[:, :LANES] + s[:, LANES:]
        return carry
    lax.fori_loop(0, n_tok // POOL_COL_STEP, col_step, 0)

    def row_step(t, carry):
        r = pl.multiple_of(t * POOL_ROW_STEP, POOL_ROW_STEP)
        acc = buf_ref[pl.ds(r, POOL_ROW_STEP), :]
        for d in range(1, w):
            acc = acc + buf_ref[pl.ds(r + d * GRID_W, POOL_ROW_STEP), :]
        m = acc * inv_ref[pl.ds(r, POOL_ROW_STEP), :]
        y_ref[pl.ds(r, POOL_ROW_STEP), :] = (m - p_ref[0, pl.ds(r, POOL_ROW_STEP), :]).astype(BF16)
        return carry
    lax.fori_loop(0, n_tok // POOL_ROW_STEP, row_step, 0)

    wp = wp_ref[0].astype(BF16)
    scale = ps_ref[0]

    def out_step(t, carry):
        r = pl.multiple_of(t * POOL_OUT_STEP, POOL_OUT_STEP)
        o_ref[0, pl.ds(r, POOL_OUT_STEP), :] = (_dot(y_ref[pl.ds(r, POOL_OUT_STEP), :], wp) * scale).astype(BF16)
        return carry
    lax.fori_loop(0, n_tok // POOL_OUT_STEP, out_step, 0)


def _pool_kernel(p_ref, wp_ref, ps_ref, o_ref, buf_ref, inv_ref, y_ref, *, n_tok):
    g = pl.program_id(0)
    first_batch = pl.program_id(1) == 0
    for gi, w in enumerate(POOL_WINDOWS):
        @pl.when(g == gi)
        def _(w=w):
            _pool_body(w, first_batch, p_ref, wp_ref, ps_ref, o_ref, buf_ref, inv_ref, y_ref, n_tok)


def _pool(p, w_pool, pool_scale):
    b, l, _ = p.shape
    n_g = len(POOL_WINDOWS)
    max_w = max(POOL_WINDOWS)
    buf_rows = l + (max_w + 1) * GRID_W
    return pl.pallas_call(
        functools.partial(_pool_kernel, n_tok=l),
        grid=(n_g, b),
        in_specs=[pl.BlockSpec((1, l, POOL_GROUP), lambda g, bi: (bi, 0, g)),
                  pl.BlockSpec((1, POOL_GROUP, POOL_GROUP), lambda g, bi: (g, 0, 0)),
                  pl.BlockSpec((1, 1, POOL_GROUP), lambda g, bi: (g, 0, 0))],
        out_specs=pl.BlockSpec((1, l, POOL_GROUP), lambda g, bi: (bi, 0, g)),
        out_shape=jax.ShapeDtypeStruct((b, l, POOL_W), BF16),
        scratch_shapes=[pltpu.VMEM((buf_rows, LANES), F32), pltpu.VMEM((l, LANES), F32),
                        pltpu.VMEM((l, LANES), BF16)],
        compiler_params=_params("arbitrary", "arbitrary"),
        name="pool",
    )(p, w_pool, pool_scale.reshape(n_g, 1, POOL_GROUP))


ROUTER_ROWS = 48
ROUTER_GROUP_ROW0 = N_EXPERTS
TOP_K = 2
SORT_TM = 512
PIECE = 16
SLOT_ROWS = TOP_K * SORT_TM + N_EXPERTS * PIECE
META_W1_HI, META_W1_LO, META_W2_HI, META_W2_LO, META_E1, META_E2 = range(6)
INFO_DEST1, INFO_DEST2 = 6, 7


def _outproj_kernel(of_ref, ob_ref, g_ref, pooled_ref, x_ref, gn_ref, wout_ref, gt_ref, sh_ref, sc_ref, g2_ref,
                    wr_hi_ref, wr_lo_ref, x1_ref, xs_ref, info_ref, cnt_ref):
    o = of_ref[0].astype(F32) + ob_ref[0].astype(F32)
    gate = _silu(g_ref[0].astype(F32))
    parts = []
    for h in range(GLA_HEADS):
        sl = slice(h * GLA_DV, (h + 1) * GLA_DV)
        parts.append((_rms(o[:, sl]) * gn_ref[...] * gate[:, sl]).astype(BF16))
    o_n = jnp.concatenate(parts, axis=-1)
    out = _dot(o_n, wout_ref[0:V_W, :]) + _dot(pooled_ref[0], wout_ref[V_W:V_W + POOL_W, :])
    x1 = x_ref[0] + gt_ref[0] * out
    x1_ref[0] = x1
    h2 = _rms(x1) * g2_ref[...]
    h2 = h2 * (1.0 + sc_ref[0]) + sh_ref[0]
    h2_bf = h2.astype(BF16)
    h2_lo = (h2 - h2_bf.astype(F32)).astype(BF16)
    logits = (lax.dot_general(wr_hi_ref[...], h2_bf, _NT, preferred_element_type=F32)
              + lax.dot_general(wr_hi_ref[...], h2_lo, _NT, preferred_element_type=F32)
              + lax.dot_general(wr_lo_ref[...], h2_bf, _NT, preferred_element_type=F32))
    tm = logits.shape[1]
    row = lax.broadcasted_iota(jnp.int32, logits.shape, 0)
    neg = -jnp.inf
    big = jnp.int32(1 << 20)
    is_group = (row >= ROUTER_GROUP_ROW0) & (row < ROUTER_GROUP_ROW0 + N_GROUPS)
    gl = jnp.where(is_group, logits, neg)
    g_max = jnp.max(gl, axis=0, keepdims=True)
    g_idx = jnp.min(jnp.where(gl == g_max, row - ROUTER_GROUP_ROW0, big), axis=0, keepdims=True)
    g_prob = 1.0 / jnp.sum(jnp.where(is_group, jnp.exp(gl - g_max), 0.0), axis=0, keepdims=True)
    n_e = EXPERTS_PER_GROUP
    el = logits[0:n_e]
    for g in range(1, N_GROUPS):
        el = jnp.where(g_idx == g, logits[g * n_e:(g + 1) * n_e], el)
    r8 = lax.broadcasted_iota(jnp.int32, el.shape, 0)
    m1 = jnp.max(el, axis=0, keepdims=True)
    i1 = jnp.min(jnp.where(el == m1, r8, big), axis=0, keepdims=True)
    el2 = jnp.where(r8 == i1, neg, el)
    m2 = jnp.max(el2, axis=0, keepdims=True)
    i2 = jnp.min(jnp.where(el2 == m2, r8, big), axis=0, keepdims=True)
    r = jnp.exp(m2 - m1)
    w1 = g_prob / (1.0 + r)
    w2 = g_prob * r / (1.0 + r)

    e1 = g_idx * n_e + i1
    e2 = g_idx * n_e + i2
    r_e = lax.broadcasted_iota(jnp.int32, (N_EXPERTS, tm), 0)
    pick1 = r_e == e1
    pick2 = r_e == e2
    onehot = jnp.where(pick1 | pick2, 1.0, 0.0)
    t_r = lax.broadcasted_iota(jnp.int32, (tm, tm), 0)
    t_c = lax.broadcasted_iota(jnp.int32, (tm, tm), 1)
    rank = _dot(onehot.astype(BF16), jnp.where(t_r < t_c, 1.0, 0.0).astype(BF16))
    count = jnp.sum(onehot, axis=1, keepdims=True)
    n_pieces = jnp.floor((count + (PIECE - 1)) * (1.0 / PIECE))
    e_r = lax.broadcasted_iota(jnp.int32, (N_EXPERTS, N_EXPERTS), 0)
    e_c = lax.broadcasted_iota(jnp.int32, (N_EXPERTS, N_EXPERTS), 1)
    run_start = _dot(jnp.where(e_c < e_r, 1.0, 0.0).astype(BF16),
                     jnp.broadcast_to(n_pieces, (N_EXPERTS, LANES)).astype(BF16))[:, 0:1] * PIECE
    slot = rank + run_start
    dest1 = jnp.sum(jnp.where(pick1, slot, 0.0), axis=0, keepdims=True)
    dest2 = jnp.sum(jnp.where(pick2, slot, 0.0), axis=0, keepdims=True)
    w1_hi = w1.astype(BF16).astype(F32)
    w2_hi = w2.astype(BF16).astype(F32)
    info_rows = [w1_hi, (w1 - w1_hi).astype(BF16).astype(F32), w2_hi, (w2 - w2_hi).astype(BF16).astype(F32),
                 e1.astype(F32), e2.astype(F32), dest1, dest2]
    info_t = jnp.concatenate(info_rows + [jnp.zeros((LANES - len(info_rows), tm), F32)], axis=0)
    info = info_t.T
    info_ref[0] = info
    cnt_ref[0, 0] = jnp.broadcast_to(count, (N_EXPERTS, LANES)).astype(jnp.int32)
    s_id = lax.broadcasted_iota(jnp.int32, (SLOT_ROWS, tm), 0)
    perm_t = jnp.where((s_id == dest1.astype(jnp.int32)) | (s_id == dest2.astype(jnp.int32)),
                       1.0, 0.0).astype(BF16)
    lane = lax.broadcasted_iota(jnp.int32, info.shape, 1)
    meta = jnp.where(lane < INFO_DEST1, info, 0.0).astype(BF16)
    d = h2_bf.shape[1]
    xs_ref[0, 0, :, 0:d] = _dot(perm_t, h2_bf).astype(BF16)
    xs_ref[0, 0, :, d:d + LANES] = _dot(perm_t, meta).astype(BF16)


def _outproj(o_f, o_b, g, pooled, x, gn, w_out, gt_a, sh_m, sc_m, g2, wr_hi, wr_lo, tm):
    b, l, d = x.shape
    tok = lambda bi, i: (bi, i, 0)
    vec = lambda bi, i: (bi, 0, 0)
    fixed = lambda bi, i: (0, 0)
    return pl.pallas_call(
        _outproj_kernel,
        grid=(b, l // tm),
        in_specs=[pl.BlockSpec((1, tm, V_W), tok), pl.BlockSpec((1, tm, V_W), tok),
                  pl.BlockSpec((1, tm, V_W), tok), pl.BlockSpec((1, tm, POOL_W), tok),
                  pl.BlockSpec((1, tm, d), tok),
                  pl.BlockSpec((1, GLA_DV), fixed),
                  pl.BlockSpec(w_out.shape, fixed),
                  pl.BlockSpec((1, 1, d), vec), pl.BlockSpec((1, 1, d), vec), pl.BlockSpec((1, 1, d), vec),
                  pl.BlockSpec((1, d), fixed),
                  pl.BlockSpec(wr_hi.shape, fixed), pl.BlockSpec(wr_lo.shape, fixed)],
        out_specs=[pl.BlockSpec((1, tm, d), tok),
                   pl.BlockSpec((1, 1, SLOT_ROWS, d + LANES), lambda bi, i: (bi, i, 0, 0)),
                   pl.BlockSpec((1, tm, LANES), tok),
                   pl.BlockSpec((1, 1, N_EXPERTS, LANES), lambda bi, i: (bi, i, 0, 0))],
        out_shape=[jax.ShapeDtypeStruct((b, l, d), F32),
                   jax.ShapeDtypeStruct((b, l // tm, SLOT_ROWS, d + LANES), BF16),
                   jax.ShapeDtypeStruct((b, l, LANES), F32),
                   jax.ShapeDtypeStruct((b, l // tm, N_EXPERTS, LANES), jnp.int32)],
        compiler_params=_params("arbitrary", "arbitrary"),
        name="outproj_router",
    )(o_f, o_b, g, pooled, x, gn, w_out, gt_a, sh_m, sc_m, g2, wr_hi, wr_lo)


EXPERT_TM = 512
PIECES_PER_STEP = EXPERT_TM // PIECE


def _expert_steps(n_tok):
    pieces = TOP_K * n_tok // PIECE + (n_tok // SORT_TM) * N_EXPERTS
    return -(-pieces // PIECES_PER_STEP) + N_EXPERTS


def _experts_kernel(exp_ref, valid_ref, src_ref, xs_hbm, wi_ref, wo_ref, ys_ref, xbuf, wi_bf, wo_bf, sem):
    i = pl.program_id(0)
    n_steps = pl.num_programs(0)
    slot = i % 2
    d = ys_ref.shape[1]

    def piece_copy(s, j, src_row):
        return pltpu.make_async_copy(xs_hbm.at[pl.ds(src_row, PIECE), :], xbuf.at[s, pl.ds(j * PIECE, PIECE), :],
                                     sem.at[s])

    def fetch(step, s):
        for j in range(PIECES_PER_STEP):
            piece_copy(s, j, pl.multiple_of(src_ref[step * PIECES_PER_STEP + j], PIECE)).start()

    @pl.when(i == 0)
    def _():
        fetch(0, 0)

    @pl.when(i + 1 < n_steps)
    def _():
        fetch(i + 1, 1 - slot)

    expert = exp_ref[i]

    @pl.when((i == 0) | (exp_ref[jnp.maximum(i - 1, 0)] != expert))
    def _():
        wi_bf[...] = wi_ref[0].astype(BF16)
        wo_bf[...] = wo_ref[0].astype(BF16)

    for j in range(PIECES_PER_STEP):
        piece_copy(slot, j, 0).wait()

    @pl.when(valid_ref[i] == 1)
    def _():
        x = xbuf[slot, :, 0:d]
        meta = xbuf[slot, :, d:d + LANES].astype(F32)
        lane = lax.broadcasted_iota(jnp.int32, meta.shape, 1)

        def pick(*lanes):
            sel = lane == lanes[0]
            for ln in lanes[1:]:
                sel = sel | (lane == ln)
            return jnp.sum(jnp.where(sel, meta, 0.0), axis=-1, keepdims=True)

        is_first = pick(META_E1) == expert.astype(F32)
        gate = jnp.where(is_first, pick(META_W1_HI, META_W1_LO), pick(META_W2_HI, META_W2_LO))
        au = _dot(x, wi_bf[...])
        hidden = (_silu(au[:, :D_EXPERT]) * au[:, D_EXPERT:] * gate).astype(BF16)
        ys_ref[...] = _dot(hidden, wo_bf[...]).astype(BF16)

    @pl.when(valid_ref[i] == 0)
    def _():
        ys_ref[...] = jnp.zeros_like(ys_ref)


def _experts(xs, step_expert, step_valid, piece_src, wi, wo):
    n_steps = step_expert.shape[0]
    width = xs.shape[1]
    d = wo.shape[2]
    return pl.pallas_call(
        _experts_kernel,
        grid_spec=pltpu.PrefetchScalarGridSpec(
            num_scalar_prefetch=3,
            grid=(n_steps,),
            in_specs=[pl.BlockSpec(memory_space=pl.ANY),
                      pl.BlockSpec((1,) + wi.shape[1:], lambda i, e, v, s: (e[i], 0, 0)),
                      pl.BlockSpec((1,) + wo.shape[1:], lambda i, e, v, s: (e[i], 0, 0))],
            out_specs=pl.BlockSpec((EXPERT_TM, d), lambda i, e, v, s: (i, 0)),
            scratch_shapes=[pltpu.VMEM((2, EXPERT_TM, width), BF16), pltpu.VMEM(wi.shape[1:], BF16),
                            pltpu.VMEM(wo.shape[1:], BF16), pltpu.SemaphoreType.DMA((2,))]),
        out_shape=jax.ShapeDtypeStruct((n_steps * EXPERT_TM, d), BF16),
        compiler_params=_params("arbitrary"),
        name="moe_experts",
    )(step_expert, step_valid, piece_src, xs, wi, wo)


def _combine_kernel(pos_ref, n_ref, tot_ref, ys_hbm, info_ref, x1_ref, gt_ref, gf_ref, o_ref, stage, sem):
    i = pl.program_id(0)
    n_steps = pl.num_programs(0)
    tm = SORT_TM

    def piece_copy(s, src_row, dst_row):
        return pltpu.make_async_copy(ys_hbm.at[pl.ds(src_row, PIECE), :], stage.at[s, pl.ds(dst_row, PIECE), :],
                                     sem.at[s])

    def fetch(tile, s):
        def per_expert(e, dst):
            n_e = n_ref[tile * N_EXPERTS + e]
            src0 = pos_ref[tile * N_EXPERTS + e]

            def issue(j, carry):
                piece_copy(s, pl.multiple_of(src0 + j * PIECE, PIECE), pl.multiple_of(dst + j * PIECE, PIECE)).start()
                return carry
            lax.fori_loop(0, n_e, issue, 0)
            return dst + n_e * PIECE
        lax.fori_loop(0, N_EXPERTS, per_expert, jnp.int32(0))

    @pl.when(i == 0)
    def _():
        fetch(0, 0)

    @pl.when(i + 1 < n_steps)
    def _():
        fetch(i + 1, (i + 1) % 2)

    slot = i % 2
    n_tile = tot_ref[i]

    def wait_body(j, carry):
        piece_copy(slot, 0, 0).wait()
        return carry
    lax.fori_loop(0, n_tile, wait_body, 0)

    def zero_body(j, carry):
        stage[slot, pl.ds(pl.multiple_of(j * PIECE, PIECE), PIECE), :] = jnp.zeros((PIECE, stage.shape[2]), BF16)
        return carry
    lax.fori_loop(n_tile, SLOT_ROWS // PIECE, zero_body, 0)

    ys = stage[slot]
    info = info_ref[...]
    lane = lax.broadcasted_iota(jnp.int32, info.shape, 1)
    dest1 = jnp.sum(jnp.where(lane == INFO_DEST1, info, 0.0), axis=-1, keepdims=True).astype(jnp.int32)
    dest2 = jnp.sum(jnp.where(lane == INFO_DEST2, info, 0.0), axis=-1, keepdims=True).astype(jnp.int32)
    slot_id = lax.broadcasted_iota(jnp.int32, (tm, SLOT_ROWS), 1)
    perm = jnp.where((slot_id == dest1) | (slot_id == dest2), 1.0, 0.0).astype(BF16)
    y = _dot(perm, ys)
    x2 = x1_ref[...] + gt_ref[0] * y
    o_ref[...] = _rms(x2) * gf_ref[...]


def _combine(ys, run_pos, run_pieces, tile_pieces, info, x1, gt_m, gf, tiles_per_batch):
    t, d = x1.shape
    n_tiles = t // SORT_TM
    tok = lambda i, *_: (i, 0)
    return pl.pallas_call(
        _combine_kernel,
        grid_spec=pltpu.PrefetchScalarGridSpec(
            num_scalar_prefetch=3,
            grid=(n_tiles,),
            in_specs=[pl.BlockSpec(memory_space=pl.ANY),
                      pl.BlockSpec((SORT_TM, LANES), tok),
                      pl.BlockSpec((SORT_TM, d), tok),
                      pl.BlockSpec((1, 1, d), lambda i, *_: (i // tiles_per_batch, 0, 0)),
                      pl.BlockSpec((1, d), lambda i, *_: (0, 0))],
            out_specs=pl.BlockSpec((SORT_TM, d), tok),
            scratch_shapes=[pltpu.VMEM((2, SLOT_ROWS, d), BF16), pltpu.SemaphoreType.DMA((2,))]),
        out_shape=jax.ShapeDtypeStruct((t, d), F32),
        compiler_params=_params("arbitrary"),
        name="moe_combine",
    )(run_pos, run_pieces, tile_pieces, ys, info, x1, gt_m, gf)


def _sparse_moe(xs, info, counts, x1, wi, wo, gt_m, gf):
    b, l, d = x1.shape
    t = b * l
    n_tiles = t // SORT_TM
    i32 = jnp.int32

    cnt = counts[:, :, :, 0].reshape(n_tiles, N_EXPERTS)
    run_pieces = (cnt + (PIECE - 1)) // PIECE
    run_slot = (jnp.cumsum(run_pieces, axis=1) - run_pieces) * PIECE
    before = jnp.cumsum(run_pieces, axis=0) - run_pieces
    expert_pieces = jnp.sum(run_pieces, axis=0)
    steps_e = (expert_pieces + PIECES_PER_STEP - 1) // PIECES_PER_STEP
    step_end = jnp.cumsum(steps_e)
    region = (step_end - steps_e) * PIECES_PER_STEP
    run_pos = (region[None, :] + before) * PIECE

    n_steps = _expert_steps(t)
    step = jnp.arange(n_steps, dtype=i32)
    step_expert = jnp.minimum(jnp.sum((step[:, None] >= step_end[None, :]).astype(i32), axis=1), N_EXPERTS - 1)
    step_valid = (step < step_end[-1]).astype(i32)
    piece = jnp.arange(n_steps * PIECES_PER_STEP, dtype=i32)
    piece_expert = jnp.repeat(step_expert, PIECES_PER_STEP)
    k_local = piece - region[piece_expert]
    ends_t = jnp.cumsum(run_pieces, axis=0).T[piece_expert]
    tile_of = jnp.minimum(jnp.sum((ends_t <= k_local[:, None]).astype(i32), axis=1), n_tiles - 1)
    j_in_run = k_local - before[tile_of, piece_expert]
    real = (k_local < expert_pieces[piece_expert]) & (jnp.repeat(step_valid, PIECES_PER_STEP) == 1)
    zero_piece = SLOT_ROWS - PIECE
    piece_src = jnp.where(real, tile_of * SLOT_ROWS + run_slot[tile_of, piece_expert] + j_in_run * PIECE, zero_piece)

    flat = lambda a: a.reshape(-1).astype(i32)
    ys = _experts(xs.reshape(n_tiles * SLOT_ROWS, d + LANES), step_expert, step_valid, flat(piece_src), wi, wo)
    out = _combine(ys, flat(run_pos), flat(run_pieces), flat(jnp.sum(run_pieces, axis=1)), info.reshape(t, LANES),
                   x1.reshape(t, d), gt_m, gf, l // SORT_TM)
    return out.reshape(b, l, d)


def kernel(x, c, ctx, c_ctx, w_ada, b_ada, norm1_g, w_in, w_decay, b_decay, gla_norm_g, w_pool, pool_scale, w_out,
           norm2_g, w_router_group, w_router_expert, w_expert_in, w_expert_out, final_norm_g):
    assert w_ada.shape[0] == 1, "single-layer trunk"
    b, l, d = x.shape
    off_a = 2 * QK_W + 2 * V_W
    a_cols = 2 * GATE_RANK
    off_p = off_a + a_cols

    c8 = jnp.zeros((8, d), F32).at[:b].set(c).at[b].set(c_ctx)
    mod = _ada_mod(c8, w_ada[0], b_ada[0])
    sh_a, sc_a, gt_a, sh_m, sc_m, gt_m = [m.reshape(8, 1, d) for m in jnp.split(mod, 6, axis=-1)]
    ctx_rows = lambda m: jnp.broadcast_to(m[b:b + 1], (b, 1, d))

    w = w_in[0]
    w_r = jnp.concatenate([w[:, :off_a], w[:, off_p:off_p + POOL_W], w[:, off_a:off_p],
                           jnp.zeros((d, LANES - a_cols), F32)], axis=1).astype(BF16)
    wdec = jnp.zeros((LANES, 2 * QK_W), F32)
    wdec = wdec.at[:GATE_RANK, :QK_W].set(w_decay[0, 0]).at[GATE_RANK:a_cols, QK_W:].set(w_decay[0, 1]).astype(BF16)
    bdec = b_decay[0].reshape(1, 2 * QK_W)
    g1 = norm1_g[0].reshape(1, d)

    q_c, k_c, v_c, _, la_c, _ = _inproj(ctx, ctx_rows(sh_a), ctx_rows(sc_a), g1, w_r, wdec, bdec, tm=ctx.shape[1])
    zero_s = jnp.zeros((b, QK_W, GLA_DV), F32)
    _, _, s_f, s_b = _gla(q_c, k_c, v_c, la_c, zero_s, zero_s, tb=ctx.shape[1])

    q, k, v, g, la, p = _inproj(x, sh_a[:b], sc_a[:b], g1, w_r, wdec, bdec, tm=1024)
    o_f, o_b, _, _ = _gla(q, k, v, la, s_f, s_b, tb=1024)
    pooled = _pool(p, w_pool[0], pool_scale[0])

    w_router = jnp.zeros((ROUTER_ROWS, d), F32)
    w_router = w_router.at[:N_EXPERTS].set(w_router_expert[0].T)
    w_router = w_router.at[ROUTER_GROUP_ROW0:ROUTER_GROUP_ROW0 + N_GROUPS].set(w_router_group[0].T)
    wr_hi = w_router.astype(BF16)
    wr_lo = (w_router - wr_hi.astype(F32)).astype(BF16)
    x1, xs, info, counts = _outproj(o_f, o_b, g, pooled, x, gla_norm_g[0].reshape(1, GLA_DV), w_out[0].astype(BF16),
                                    gt_a[:b], sh_m[:b], sc_m[:b], norm2_g[0].reshape(1, d), wr_hi, wr_lo,
                                    tm=SORT_TM)

    return _sparse_moe(xs, info, counts, x1, w_expert_in[0], w_expert_out[0], gt_m[:b], final_norm_g.reshape(1, d))
```

```python
import functools

import jax
import jax.numpy as jnp
from jax import lax
from jax.experimental import pallas as pl
from jax.experimental.pallas import tpu as pltpu

F32 = jnp.float32
BF16 = jnp.bfloat16

GRID_W = 64
GRID_W_LOG2 = 6
assert 1 << GRID_W_LOG2 == GRID_W
GLA_HEADS = 4
GLA_DK = 64
GLA_DV = 128
GATE_RANK = 16
GATE_NORMALIZER = 16.0
CHUNK = 64
POOL_WINDOWS = (2, 4, 8, 16)
POOL_GROUP = 128
N_GROUPS = 4
EXPERTS_PER_GROUP = 8
N_EXPERTS = N_GROUPS * EXPERTS_PER_GROUP
D_EXPERT = 256
EPS = 1e-6

QK_W = GLA_HEADS * GLA_DK
V_W = GLA_HEADS * GLA_DV
POOL_W = POOL_GROUP * len(POOL_WINDOWS)
LANES = 128
VMEM_LIMIT = 48 * 1024 * 1024

_NT = (((1,), (1,)), ((), ()))


def _dot(a, b):
    return jnp.dot(a, b, preferred_element_type=F32)


def _split_bf16(x):
    hi = x.astype(BF16)
    lo = (x - hi.astype(F32)).astype(BF16)
    return hi, lo


def _rms(x):
    return x * lax.rsqrt(jnp.mean(x * x, axis=-1, keepdims=True) + EPS)


def _silu(x):
    return x / (1.0 + jnp.exp(-x))


def _params(*sem):
    return pltpu.CompilerParams(dimension_semantics=sem, vmem_limit_bytes=VMEM_LIMIT)


def _ada_kernel(c_ref, w_ref, b_ref, o_ref):
    s = _silu(c_ref[...]).astype(BF16)
    o_ref[...] = _dot(s, w_ref[...].astype(BF16)) + b_ref[...]


def _ada_mod(c8, w_ada, b_ada):
    rows, d = c8.shape
    n = w_ada.shape[1]
    tn = 1024
    return pl.pallas_call(
        _ada_kernel,
        grid=(n // tn,),
        in_specs=[pl.BlockSpec((rows, d), lambda j: (0, 0)),
                  pl.BlockSpec((d, tn), lambda j: (0, j)),
                  pl.BlockSpec((1, tn), lambda j: (0, j))],
        out_specs=pl.BlockSpec((rows, tn), lambda j: (0, j)),
        out_shape=jax.ShapeDtypeStruct((rows, n), F32),
        compiler_params=_params("arbitrary"),
        name="ada_mod",
    )(c8, w_ada, b_ada.reshape(1, n))


def _inproj_kernel(x_ref, sh_ref, sc_ref, g1_ref, w_ref, wdec_ref, bdec_ref,
                   q_ref, k_ref, v_ref, g_ref, la_ref, p_ref):
    x = x_ref[0]
    h = _rms(x) * g1_ref[...]
    h = h * (1.0 + sc_ref[0]) + sh_ref[0]
    hb = h.astype(BF16)
    o = 0
    q_ref[0] = (_dot(hb, w_ref[:, o:o + QK_W]) * (GLA_DK ** -0.5)).astype(BF16)
    o += QK_W
    k_ref[0] = _dot(hb, w_ref[:, o:o + QK_W]).astype(BF16)
    o += QK_W
    v_ref[0] = _dot(hb, w_ref[:, o:o + V_W]).astype(BF16)
    o += V_W
    g_ref[0] = _dot(hb, w_ref[:, o:o + V_W]).astype(BF16)
    o += V_W
    p_ref[0] = _dot(hb, w_ref[:, o:o + POOL_W])
    o += POOL_W
    a_low = _dot(hb, w_ref[:, o:o + LANES])
    z = _dot(a_low.astype(BF16), wdec_ref[...]) + bdec_ref[...]
    log_sig = jnp.minimum(z, 0.0) - jnp.log(1.0 + jnp.exp(-jnp.abs(z)))
    la_ref[0] = log_sig / GATE_NORMALIZER


def _inproj(x, shift, scale, g1, w_r, wdec, bdec, tm):
    b, l, d = x.shape
    wcols = w_r.shape[1]
    tok = lambda bi, i: (bi, i, 0)
    vec = lambda bi, i: (bi, 0, 0)
    fixed = lambda bi, i: (0, 0)
    outs = [(QK_W, BF16), (QK_W, BF16), (V_W, BF16), (V_W, BF16), (2 * QK_W, F32), (POOL_W, F32)]
    return pl.pallas_call(
        _inproj_kernel,
        grid=(b, l // tm),
        in_specs=[pl.BlockSpec((1, tm, d), tok),
                  pl.BlockSpec((1, 1, d), vec),
                  pl.BlockSpec((1, 1, d), vec),
                  pl.BlockSpec((1, d), fixed),
                  pl.BlockSpec((d, wcols), fixed),
                  pl.BlockSpec(wdec.shape, fixed),
                  pl.BlockSpec(bdec.shape, fixed)],
        out_specs=[pl.BlockSpec((1, tm, w), tok) for w, _ in outs],
        out_shape=[jax.ShapeDtypeStruct((b, l, w), dt) for w, dt in outs],
        compiler_params=_params("arbitrary", "arbitrary"),
        name="inproj",
    )(x, shift, scale, g1, w_r, wdec, bdec)


GLA_GROUP = 256
CHUNKS_PER_GROUP = GLA_GROUP // CHUNK
CHUNK_LOG2 = 6
assert 1 << CHUNK_LOG2 == CHUNK


def _head_masks(rows):
    lane = lax.broadcasted_iota(jnp.int32, (rows, QK_W), 1)
    return [(lane >= h * GLA_DK) & (lane < (h + 1) * GLA_DK) for h in range(GLA_HEADS)]


def _gla_bulk(d, g, reverse, q_ref, k_ref, v_ref, la_ref, qd_scr, oi_scr, ds_scr, dc_scr):
    n = GLA_GROUP
    r = pl.multiple_of(g * n, n)
    i0 = lax.broadcasted_iota(jnp.int32, (n, n), 0)
    i1 = lax.broadcasted_iota(jnp.int32, (n, n), 1)
    causal = ((i0 >> CHUNK_LOG2) == (i1 >> CHUNK_LOG2)) & ((i1 >= i0) if reverse else (i1 <= i0))
    tri = jnp.where(causal, 1.0, 0.0).astype(BF16)
    la_hi, la_lo = _split_bf16(la_ref[0, pl.ds(r, n), :])
    b = _dot(tri, la_hi) + _dot(tri, la_lo)
    last_row = 0 if reverse else CHUNK - 1
    b_end = b.reshape(CHUNKS_PER_GROUP, CHUNK, QK_W)[:, last_row:last_row + 1, :]
    b_last = jnp.broadcast_to(b_end, (CHUNKS_PER_GROUP, CHUNK, QK_W)).reshape(n, QK_W)
    q = q_ref[0, pl.ds(r, n), :].astype(F32)
    k = k_ref[0, pl.ds(r, n), :].astype(F32)
    qd = (q * jnp.exp(b)).astype(BF16)
    kd = (k * jnp.exp(-b)).astype(BF16)
    k_st = k * jnp.exp(b_last - b)
    qd_scr[d, pl.ds(r, n), :] = qd
    v = v_ref[0, pl.ds(r, n), :]
    for h, m in enumerate(_head_masks(n)):
        scores = lax.dot_general(jnp.where(m, qd, jnp.zeros_like(qd)), kd, _NT, preferred_element_type=F32)
        probs = jnp.where(causal, scores, 0.0).astype(BF16)
        oi_scr[d, pl.ds(r, n), h * GLA_DV:(h + 1) * GLA_DV] = _dot(probs, v[:, h * GLA_DV:(h + 1) * GLA_DV])
    decay = jnp.exp(b_end)
    for c in range(CHUNKS_PER_GROUP):
        ci = g * CHUNKS_PER_GROUP + c
        k_st_t = k_st[c * CHUNK:(c + 1) * CHUNK].T.astype(BF16)
        for h in range(GLA_HEADS):
            ds_scr[d, ci, h * GLA_DK:(h + 1) * GLA_DK, :] = _dot(
                k_st_t[h * GLA_DK:(h + 1) * GLA_DK], v[c * CHUNK:(c + 1) * CHUNK, h * GLA_DV:(h + 1) * GLA_DV])
        dc_scr[d, ci] = jnp.broadcast_to(decay[c], (LANES, QK_W)).T


def _gla_kernel(qf_ref, kf_ref, vf_ref, laf_ref, qb_ref, kb_ref, vb_ref, lab_ref, s0f_ref, s0b_ref,
                of_ref, ob_ref, sf_out_ref, sb_out_ref, s_ref, qd_scr, oi_scr, ds_scr, dc_scr, st_scr, *, n_chunks):
    j = pl.program_id(1)

    @pl.when(j == 0)
    def _():
        s_ref[0] = s0f_ref[0]
        s_ref[1] = s0b_ref[0]

    def bulk(g, carry):
        _gla_bulk(0, g, False, qf_ref, kf_ref, vf_ref, laf_ref, qd_scr, oi_scr, ds_scr, dc_scr)
        _gla_bulk(1, g, True, qb_ref, kb_ref, vb_ref, lab_ref, qd_scr, oi_scr, ds_scr, dc_scr)
        return carry
    lax.fori_loop(0, n_chunks // CHUNKS_PER_GROUP, bulk, 0)

    def recur(i, carry):
        for d, ci in ((0, i), (1, n_chunks - 1 - i)):
            s = s_ref[d]
            st_scr[d, ci] = s.astype(BF16)
            s_ref[d] = dc_scr[d, ci] * s + ds_scr[d, ci]
        return carry
    lax.fori_loop(0, n_chunks, recur, 0)

    masks = _head_masks(CHUNK)

    def inter(g, carry):
        for d, o_ref in ((0, of_ref), (1, ob_ref)):
            for c in range(CHUNKS_PER_GROUP):
                ci = g * CHUNKS_PER_GROUP + c
                r = pl.multiple_of(ci * CHUNK, CHUNK)
                qd = qd_scr[d, pl.ds(r, CHUNK), :]
                lhs = jnp.concatenate([jnp.where(m, qd, jnp.zeros_like(qd)) for m in masks], axis=0)
                from_state = _dot(lhs, st_scr[d, ci])
                for h in range(GLA_HEADS):
                    cols = slice(h * GLA_DV, (h + 1) * GLA_DV)
                    o_ref[0, pl.ds(r, CHUNK), cols] = (oi_scr[d, pl.ds(r, CHUNK), cols]
                                                       + from_state[h * CHUNK:(h + 1) * CHUNK]).astype(BF16)
        return carry
    lax.fori_loop(0, n_chunks // CHUNKS_PER_GROUP, inter, 0)

    @pl.when(j == pl.num_programs(1) - 1)
    def _():
        sf_out_ref[0] = s_ref[0]
        sb_out_ref[0] = s_ref[1]


def _gla(q, k, v, la, s0f, s0b, tb):
    b, l, _ = q.shape
    assert tb % GLA_GROUP == 0 and l % tb == 0
    nb = l // tb
    n_chunks = tb // CHUNK
    fwd = lambda bi, j: (bi, j, 0)
    bwd = lambda bi, j: (bi, nb - 1 - j, 0)
    fwd_la = lambda bi, j: (bi, j, 0)
    bwd_la = lambda bi, j: (bi, nb - 1 - j, 1)
    st = lambda bi, j: (bi, 0, 0)
    state_shape = (b, QK_W, GLA_DV)
    return pl.pallas_call(
        functools.partial(_gla_kernel, n_chunks=tb // CHUNK),
        grid=(b, nb),
        in_specs=[pl.BlockSpec((1, tb, QK_W), fwd), pl.BlockSpec((1, tb, QK_W), fwd),
                  pl.BlockSpec((1, tb, V_W), fwd), pl.BlockSpec((1, tb, QK_W), fwd_la),
                  pl.BlockSpec((1, tb, QK_W), bwd), pl.BlockSpec((1, tb, QK_W), bwd),
                  pl.BlockSpec((1, tb, V_W), bwd), pl.BlockSpec((1, tb, QK_W), bwd_la),
                  pl.BlockSpec((1, QK_W, GLA_DV), st), pl.BlockSpec((1, QK_W, GLA_DV), st)],
        out_specs=[pl.BlockSpec((1, tb, V_W), fwd), pl.BlockSpec((1, tb, V_W), bwd),
                   pl.BlockSpec((1, QK_W, GLA_DV), st), pl.BlockSpec((1, QK_W, GLA_DV), st)],
        out_shape=[jax.ShapeDtypeStruct((b, l, V_W), BF16), jax.ShapeDtypeStruct((b, l, V_W), BF16),
                   jax.ShapeDtypeStruct(state_shape, F32), jax.ShapeDtypeStruct(state_shape, F32)],
        scratch_shapes=[pltpu.VMEM((2, QK_W, GLA_DV), F32),
                        pltpu.VMEM((2, tb, QK_W), BF16),
                        pltpu.VMEM((2, tb, V_W), F32),
                        pltpu.VMEM((2, n_chunks, QK_W, GLA_DV), F32),
                        pltpu.VMEM((2, n_chunks, QK_W, GLA_DV), F32),
                        pltpu.VMEM((2, n_chunks, QK_W, GLA_DV), BF16)],
        compiler_params=_params("arbitrary", "arbitrary"),
        name="gla",
    )(q, k, v, la, q, k, v, la, s0f, s0b)


POOL_BLK = 256
POOL_COL_STEP = 1024
POOL_ROW_STEP = 256
POOL_OUT_STEP = 1024


def _pool_body(w, first_batch, p_ref, wp_ref, ps_ref, o_ref, buf_ref, inv_ref, y_ref, n_tok):
    lo = w // 2
    hi = w - 1 - lo
    pad = lo * GRID_W
    n_rows = n_tok // GRID_W
    blk = POOL_BLK

    @pl.when(first_batch)
    def _():
        zeros = jnp.zeros((GRID_W * 8, LANES), F32)
        buf_ref[0:pad, :] = zeros[0:pad]
        buf_ref[pad + n_tok:pad + n_tok + hi * GRID_W + GRID_W, :] = zeros[0:hi * GRID_W + GRID_W]

        def inv_step(t, carry):
            r = pl.multiple_of(t * POOL_OUT_STEP, POOL_OUT_STEP)
            tok = r + lax.broadcasted_iota(jnp.int32, (POOL_OUT_STEP, LANES), 0)
            g_row = tok >> GRID_W_LOG2
            g_col = tok & (GRID_W - 1)
            cnt_c = jnp.minimum(g_col + hi, GRID_W - 1) - jnp.maximum(g_col - lo, 0) + 1
            cnt_r = jnp.minimum(g_row + hi, n_rows - 1) - jnp.maximum(g_row - lo, 0) + 1
            inv_ref[pl.ds(r, POOL_OUT_STEP), :] = 1.0 / (cnt_c * cnt_r).astype(F32)
            return carry
        lax.fori_loop(0, n_tok // POOL_OUT_STEP, inv_step, 0)

    i0 = lax.broadcasted_iota(jnp.int32, (blk, blk), 0)
    i1 = lax.broadcasted_iota(jnp.int32, (blk, blk), 1)
    same_row = (i0 >> GRID_W_LOG2) == (i1 >> GRID_W_LOG2)
    band = jnp.where(same_row & (i1 - i0 >= -lo) & (i1 - i0 <= hi), 1.0, 0.0).astype(BF16)

    def col_step(t, carry):
        for u in range(POOL_COL_STEP // blk):
            r = pl.multiple_of(t * POOL_COL_STEP + u * blk, blk)
            x_hi, x_lo = _split_bf16(p_ref[0, pl.ds(r, blk), :])
            s = _dot(band, jnp.concatenate([x_hi, x_lo], axis=-1))
            buf_ref[pl.ds(pad + r, blk), :] = s[:, :LANES] + s[:, LANES:]
        return carry
    lax.fori_loop(0, n_tok // POOL_COL_STEP, col_step, 0)

    def row_step(t, carry):
        r = pl.multiple_of(t * POOL_ROW_STEP, POOL_ROW_STEP)
        acc = buf_ref[pl.ds(r, POOL_ROW_STEP), :]
        for d in range(1, w):
            acc = acc + buf_ref[pl.ds(r + d * GRID_W, POOL_ROW_STEP), :]
        m = acc * inv_ref[pl.ds(r, POOL_ROW_STEP), :]
        y_ref[pl.ds(r, POOL_ROW_STEP), :] = (m - p_ref[0, pl.ds(r, POOL_ROW_STEP), :]).astype(BF16)
        return carry
    lax.fori_loop(0, n_tok // POOL_ROW_STEP, row_step, 0)

    wp = wp_ref[0].astype(BF16)
    scale = ps_ref[0]

    def out_step(t, carry):
        r = pl.multiple_of(t * POOL_OUT_STEP, POOL_OUT_STEP)
        o_ref[0, pl.ds(r, POOL_OUT_STEP), :] = (_dot(y_ref[pl.ds(r, POOL_OUT_STEP), :], wp) * scale).astype(BF16)
        return carry
    lax.fori_loop(0, n_tok // POOL_OUT_STEP, out_step, 0)


def _pool_kernel(p_ref, wp_ref, ps_ref, o_ref, buf_ref, inv_ref, y_ref, *, n_tok):
    g = pl.program_id(0)
    first_batch = pl.program_id(1) == 0
    for gi, w in enumerate(POOL_WINDOWS):
        @pl.when(g == gi)
        def _(w=w):
            _pool_body(w, first_batch, p_ref, wp_ref, ps_ref, o_ref, buf_ref, inv_ref, y_ref, n_tok)


def _pool(p, w_pool, pool_scale):
    b, l, _ = p.shape
    n_g = len(POOL_WINDOWS)
    max_w = max(POOL_WINDOWS)
    buf_rows = l + (max_w + 1) * GRID_W
    return pl.pallas_call(
        functools.partial(_pool_kernel, n_tok=l),
        grid=(n_g, b),
        in_specs=[pl.BlockSpec((1, l, POOL_GROUP), lambda g, bi: (bi, 0, g)),
                  pl.BlockSpec((1, POOL_GROUP, POOL_GROUP), lambda g, bi: (g, 0, 0)),
                  pl.BlockSpec((1, 1, POOL_GROUP), lambda g, bi: (g, 0, 0))],
        out_specs=pl.BlockSpec((1, l, POOL_GROUP), lambda g, bi: (bi, 0, g)),
        out_shape=jax.ShapeDtypeStruct((b, l, POOL_W), BF16),
        scratch_shapes=[pltpu.VMEM((buf_rows, LANES), F32), pltpu.VMEM((l, LANES), F32),
                        pltpu.VMEM((l, LANES), BF16)],
        compiler_params=_params("arbitrary", "arbitrary"),
        name="pool",
    )(p, w_pool, pool_scale.reshape(n_g, 1, POOL_GROUP))


ROUTER_ROWS = 48
ROUTER_GROUP_ROW0 = N_EXPERTS
TOP_K = 2
SORT_TM = 512
PIECE = 16
SLOT_ROWS = TOP_K * SORT_TM + N_EXPERTS * PIECE
META_W1_HI, META_W1_LO, META_W2_HI, META_W2_LO, META_E1, META_E2 = range(6)
INFO_DEST1, INFO_DEST2 = 6, 7


def _outproj_kernel(of_ref, ob_ref, g_ref, pooled_ref, x_ref, gn_ref, wout_ref, gt_ref, sh_ref, sc_ref, g2_ref,
                    wr_hi_ref, wr_lo_ref, x1_ref, xs_ref, info_ref, cnt_ref):
    o = of_ref[0].astype(F32) + ob_ref[0].astype(F32)
    gate = _silu(g_ref[0].astype(F32))
    parts = []
    for h in range(GLA_HEADS):
        sl = slice(h * GLA_DV, (h + 1) * GLA_DV)
        parts.append((_rms(o[:, sl]) * gn_ref[...] * gate[:, sl]).astype(BF16))
    o_n = jnp.concatenate(parts, axis=-1)
    out = _dot(o_n, wout_ref[0:V_W, :]) + _dot(pooled_ref[0], wout_ref[V_W:V_W + POOL_W, :])
    x1 = x_ref[0] + gt_ref[0] * out
    x1_ref[0] = x1
    h2 = _rms(x1) * g2_ref[...]
    h2 = h2 * (1.0 + sc_ref[0]) + sh_ref[0]
    h2_bf = h2.astype(BF16)
    h2_lo = (h2 - h2_bf.astype(F32)).astype(BF16)
    logits = (lax.dot_general(wr_hi_ref[...], h2_bf, _NT, preferred_element_type=F32)
              + lax.dot_general(wr_hi_ref[...], h2_lo, _NT, preferred_element_type=F32)
              + lax.dot_general(wr_lo_ref[...], h2_bf, _NT, preferred_element_type=F32))
    tm = logits.shape[1]
    row = lax.broadcasted_iota(jnp.int32, logits.shape, 0)
    neg = -jnp.inf
    big = jnp.int32(1 << 20)
    is_group = (row >= ROUTER_GROUP_ROW0) & (row < ROUTER_GROUP_ROW0 + N_GROUPS)
    gl = jnp.where(is_group, logits, neg)
    g_max = jnp.max(gl, axis=0, keepdims=True)
    g_idx = jnp.min(jnp.where(gl == g_max, row - ROUTER_GROUP_ROW0, big), axis=0, keepdims=True)
    g_prob = 1.0 / jnp.sum(jnp.where(is_group, jnp.exp(gl - g_max), 0.0), axis=0, keepdims=True)
    n_e = EXPERTS_PER_GROUP
    el = logits[0:n_e]
    for g in range(1, N_GROUPS):
        el = jnp.where(g_idx == g, logits[g * n_e:(g + 1) * n_e], el)
    r8 = lax.broadcasted_iota(jnp.int32, el.shape, 0)
    m1 = jnp.max(el, axis=0, keepdims=True)
    i1 = jnp.min(jnp.where(el == m1, r8, big), axis=0, keepdims=True)
    el2 = jnp.where(r8 == i1, neg, el)
    m2 = jnp.max(el2, axis=0, keepdims=True)
    i2 = jnp.min(jnp.where(el2 == m2, r8, big), axis=0, keepdims=True)
    r = jnp.exp(m2 - m1)
    w1 = g_prob / (1.0 + r)
    w2 = g_prob * r / (1.0 + r)

    e1 = g_idx * n_e + i1
    e2 = g_idx * n_e + i2
    r_e = lax.broadcasted_iota(jnp.int32, (N_EXPERTS, tm), 0)
    pick1 = r_e == e1
    pick2 = r_e == e2
    onehot = jnp.where(pick1 | pick2, 1.0, 0.0)
    t_r = lax.broadcasted_iota(jnp.int32, (tm, tm), 0)
    t_c = lax.broadcasted_iota(jnp.int32, (tm, tm), 1)
    rank = _dot(onehot.astype(BF16), jnp.where(t_r < t_c, 1.0, 0.0).astype(BF16))
    count = jnp.sum(onehot, axis=1, keepdims=True)
    n_pieces = jnp.floor((count + (PIECE - 1)) * (1.0 / PIECE))
    e_r = lax.broadcasted_iota(jnp.int32, (N_EXPERTS, N_EXPERTS), 0)
    e_c = lax.broadcasted_iota(jnp.int32, (N_EXPERTS, N_EXPERTS), 1)
    run_start = _dot(jnp.where(e_c < e_r, 1.0, 0.0).astype(BF16),
                     jnp.broadcast_to(n_pieces, (N_EXPERTS, LANES)).astype(BF16))[:, 0:1] * PIECE
    slot = rank + run_start
    dest1 = jnp.sum(jnp.where(pick1, slot, 0.0), axis=0, keepdims=True)
    dest2 = jnp.sum(jnp.where(pick2, slot, 0.0), axis=0, keepdims=True)
    w1_hi = w1.astype(BF16).astype(F32)
    w2_hi = w2.astype(BF16).astype(F32)
    info_rows = [w1_hi, (w1 - w1_hi).astype(BF16).astype(F32), w2_hi, (w2 - w2_hi).astype(BF16).astype(F32),
                 e1.astype(F32), e2.astype(F32), dest1, dest2]
    info_t = jnp.concatenate(info_rows + [jnp.zeros((LANES - len(info_rows), tm), F32)], axis=0)
    info = info_t.T
    info_ref[0] = info
    cnt_ref[0, 0] = jnp.broadcast_to(count, (N_EXPERTS, LANES)).astype(jnp.int32)
    s_id = lax.broadcasted_iota(jnp.int32, (SLOT_ROWS, tm), 0)
    perm_t = jnp.where((s_id == dest1.astype(jnp.int32)) | (s_id == dest2.astype(jnp.int32)),
                       1.0, 0.0).astype(BF16)
    lane = lax.broadcasted_iota(jnp.int32, info.shape, 1)
    meta = jnp.where(lane < INFO_DEST1, info, 0.0).astype(BF16)
    d = h2_bf.shape[1]
    xs_ref[0, 0, :, 0:d] = _dot(perm_t, h2_bf).astype(BF16)
    xs_ref[0, 0, :, d:d + LANES] = _dot(perm_t, meta).astype(BF16)


def _outproj(o_f, o_b, g, pooled, x, gn, w_out, gt_a, sh_m, sc_m, g2, wr_hi, wr_lo, tm):
    b, l, d = x.shape
    tok = lambda bi, i: (bi, i, 0)
    vec = lambda bi, i: (bi, 0, 0)
    fixed = lambda bi, i: (0, 0)
    return pl.pallas_call(
        _outproj_kernel,
        grid=(b, l // tm),
        in_specs=[pl.BlockSpec((1, tm, V_W), tok), pl.BlockSpec((1, tm, V_W), tok),
                  pl.BlockSpec((1, tm, V_W), tok), pl.BlockSpec((1, tm, POOL_W), tok),
                  pl.BlockSpec((1, tm, d), tok),
                  pl.BlockSpec((1, GLA_DV), fixed),
                  pl.BlockSpec(w_out.shape, fixed),
                  pl.BlockSpec((1, 1, d), vec), pl.BlockSpec((1, 1, d), vec), pl.BlockSpec((1, 1, d), vec),
                  pl.BlockSpec((1, d), fixed),
                  pl.BlockSpec(wr_hi.shape, fixed), pl.BlockSpec(wr_lo.shape, fixed)],
        out_specs=[pl.BlockSpec((1, tm, d), tok),
                   pl.BlockSpec((1, 1, SLOT_ROWS, d + LANES), lambda bi, i: (bi, i, 0, 0)),
                   pl.BlockSpec((1, tm, LANES), tok),
                   pl.BlockSpec((1, 1, N_EXPERTS, LANES), lambda bi, i: (bi, i, 0, 0))],
        out_shape=[jax.ShapeDtypeStruct((b, l, d), F32),
                   jax.ShapeDtypeStruct((b, l // tm, SLOT_ROWS, d + LANES), BF16),
                   jax.ShapeDtypeStruct((b, l, LANES), F32),
                   jax.ShapeDtypeStruct((b, l // tm, N_EXPERTS, LANES), jnp.int32)],
        compiler_params=_params("arbitrary", "arbitrary"),
        name="outproj_router",
    )(o_f, o_b, g, pooled, x, gn, w_out, gt_a, sh_m, sc_m, g2, wr_hi, wr_lo)


EXPERT_TM = 512
PIECES_PER_STEP = EXPERT_TM // PIECE
GATHER_SLOTS = 4


def _expert_steps(n_tok):
    pieces = TOP_K * n_tok // PIECE + (n_tok // SORT_TM) * N_EXPERTS
    return -(-pieces // PIECES_PER_STEP) + N_EXPERTS


def _experts_kernel(exp_ref, valid_ref, src_ref, xs_hbm, wi_ref, wo_ref, ys_ref, xbuf, wi_bf, wo_bf, sem):
    i = pl.program_id(0)
    n_steps = pl.num_programs(0)
    slot = i % GATHER_SLOTS
    d = ys_ref.shape[1]

    def piece_copy(s, j, src_row):
        return pltpu.make_async_copy(xs_hbm.at[pl.ds(src_row, PIECE), :], xbuf.at[s, pl.ds(j * PIECE, PIECE), :],
                                     sem.at[s])

    def fetch(step):
        @pl.when((step < n_steps) & (valid_ref[jnp.minimum(step, n_steps - 1)] == 1))
        def _():
            for j in range(PIECES_PER_STEP):
                piece_copy(step % GATHER_SLOTS, j,
                           pl.multiple_of(src_ref[step * PIECES_PER_STEP + j], PIECE)).start()

    @pl.when(i == 0)
    def _():
        for ahead in range(GATHER_SLOTS - 1):
            fetch(jnp.int32(ahead))

    fetch(i + GATHER_SLOTS - 1)
    expert = exp_ref[i]

    @pl.when((i == 0) | (exp_ref[jnp.maximum(i - 1, 0)] != expert))
    def _():
        wi_bf[...] = wi_ref[0].astype(BF16)
        wo_bf[...] = wo_ref[0].astype(BF16)

    @pl.when(valid_ref[i] == 1)
    def _():
        for j in range(PIECES_PER_STEP):
            piece_copy(slot, j, 0).wait()
        x = xbuf[slot, :, 0:d]
        meta = xbuf[slot, :, d:d + LANES].astype(F32)
        lane = lax.broadcasted_iota(jnp.int32, meta.shape, 1)

        def pick(*lanes):
            sel = lane == lanes[0]
            for ln in lanes[1:]:
                sel = sel | (lane == ln)
            return jnp.sum(jnp.where(sel, meta, 0.0), axis=-1, keepdims=True)

        is_first = pick(META_E1) == expert.astype(F32)
        gate = jnp.where(is_first, pick(META_W1_HI, META_W1_LO), pick(META_W2_HI, META_W2_LO))
        au = _dot(x, wi_bf[...])
        hidden = (_silu(au[:, :D_EXPERT]) * au[:, D_EXPERT:] * gate).astype(BF16)
        ys_ref[...] = _dot(hidden, wo_bf[...]).astype(BF16)

    @pl.when(valid_ref[i] == 0)
    def _():
        ys_ref[...] = jnp.zeros_like(ys_ref)


def _experts(xs, step_expert, step_valid, piece_src, wi, wo):
    n_steps = step_expert.shape[0]
    width = xs.shape[1]
    d = wo.shape[2]
    return pl.pallas_call(
        _experts_kernel,
        grid_spec=pltpu.PrefetchScalarGridSpec(
            num_scalar_prefetch=3,
            grid=(n_steps,),
            in_specs=[pl.BlockSpec(memory_space=pl.ANY),
                      pl.BlockSpec((1,) + wi.shape[1:], lambda i, e, v, s: (e[i], 0, 0)),
                      pl.BlockSpec((1,) + wo.shape[1:], lambda i, e, v, s: (e[i], 0, 0))],
            out_specs=pl.BlockSpec((EXPERT_TM, d), lambda i, e, v, s: (i, 0)),
            scratch_shapes=[pltpu.VMEM((GATHER_SLOTS, EXPERT_TM, width), BF16), pltpu.VMEM(wi.shape[1:], BF16),
                            pltpu.VMEM(wo.shape[1:], BF16), pltpu.SemaphoreType.DMA((GATHER_SLOTS,))]),
        out_shape=jax.ShapeDtypeStruct((n_steps * EXPERT_TM, d), BF16),
        compiler_params=_params("arbitrary"),
        name="moe_experts",
    )(step_expert, step_valid, piece_src, xs, wi, wo)


def _combine_kernel(pos_ref, n_ref, tot_ref, ys_hbm, info_ref, x1_ref, gt_ref, gf_ref, o_ref, stage, sem):
    i = pl.program_id(0)
    n_steps = pl.num_programs(0)
    tm = SORT_TM

    def piece_copy(s, src_row, dst_row):
        return pltpu.make_async_copy(ys_hbm.at[pl.ds(src_row, PIECE), :], stage.at[s, pl.ds(dst_row, PIECE), :],
                                     sem.at[s])

    def fetch(tile, s):
        def per_expert(e, dst):
            n_e = n_ref[tile * N_EXPERTS + e]
            src0 = pos_ref[tile * N_EXPERTS + e]

            def issue(j, carry):
                piece_copy(s, pl.multiple_of(src0 + j * PIECE, PIECE), pl.multiple_of(dst + j * PIECE, PIECE)).start()
                return carry
            lax.fori_loop(0, n_e, issue, 0)
            return dst + n_e * PIECE
        lax.fori_loop(0, N_EXPERTS, per_expert, jnp.int32(0))

    @pl.when(i == 0)
    def _():
        fetch(0, 0)

    @pl.when(i + 1 < n_steps)
    def _():
        fetch(i + 1, (i + 1) % 2)

    slot = i % 2
    n_tile = tot_ref[i]

    def wait_body(j, carry):
        piece_copy(slot, 0, 0).wait()
        return carry
    lax.fori_loop(0, n_tile, wait_body, 0)

    def zero_body(j, carry):
        stage[slot, pl.ds(pl.multiple_of(j * PIECE, PIECE), PIECE), :] = jnp.zeros((PIECE, stage.shape[2]), BF16)
        return carry
    lax.fori_loop(n_tile, SLOT_ROWS // PIECE, zero_body, 0)

    ys = stage[slot]
    info = info_ref[...]
    lane = lax.broadcasted_iota(jnp.int32, info.shape, 1)
    dest1 = jnp.sum(jnp.where(lane == INFO_DEST1, info, 0.0), axis=-1, keepdims=True).astype(jnp.int32)
    dest2 = jnp.sum(jnp.where(lane == INFO_DEST2, info, 0.0), axis=-1, keepdims=True).astype(jnp.int32)
    slot_id = lax.broadcasted_iota(jnp.int32, (tm, SLOT_ROWS), 1)
    perm = jnp.where((slot_id == dest1) | (slot_id == dest2), 1.0, 0.0).astype(BF16)
    y = _dot(perm, ys)
    x2 = x1_ref[...] + gt_ref[0] * y
    o_ref[...] = _rms(x2) * gf_ref[...]


def _combine(ys, run_pos, run_pieces, tile_pieces, info, x1, gt_m, gf, tiles_per_batch):
    t, d = x1.shape
    n_tiles = t // SORT_TM
    tok = lambda i, *_: (i, 0)
    return pl.pallas_call(
        _combine_kernel,
        grid_spec=pltpu.PrefetchScalarGridSpec(
            num_scalar_prefetch=3,
            grid=(n_tiles,),
            in_specs=[pl.BlockSpec(memory_space=pl.ANY),
                      pl.BlockSpec((SORT_TM, LANES), tok),
                      pl.BlockSpec((SORT_TM, d), tok),
                      pl.BlockSpec((1, 1, d), lambda i, *_: (i // tiles_per_batch, 0, 0)),
                      pl.BlockSpec((1, d), lambda i, *_: (0, 0))],
            out_specs=pl.BlockSpec((SORT_TM, d), tok),
            scratch_shapes=[pltpu.VMEM((2, SLOT_ROWS, d), BF16), pltpu.SemaphoreType.DMA((2,))]),
        out_shape=jax.ShapeDtypeStruct((t, d), F32),
        compiler_params=_params("arbitrary"),
        name="moe_combine",
    )(run_pos, run_pieces, tile_pieces, ys, info, x1, gt_m, gf)


def _sparse_moe(xs, info, counts, x1, wi, wo, gt_m, gf):
    b, l, d = x1.shape
    t = b * l
    n_tiles = t // SORT_TM
    i32 = jnp.int32

    cnt = counts[:, :, :, 0].reshape(n_tiles, N_EXPERTS)
    run_pieces = (cnt + (PIECE - 1)) // PIECE
    run_slot = (jnp.cumsum(run_pieces, axis=1) - run_pieces) * PIECE
    before = jnp.cumsum(run_pieces, axis=0) - run_pieces
    expert_pieces = jnp.sum(run_pieces, axis=0)
    steps_e = (expert_pieces + PIECES_PER_STEP - 1) // PIECES_PER_STEP
    step_end = jnp.cumsum(steps_e)
    region = (step_end - steps_e) * PIECES_PER_STEP
    run_pos = (region[None, :] + before) * PIECE

    n_steps = _expert_steps(t)
    step = jnp.arange(n_steps, dtype=i32)
    step_expert = jnp.minimum(jnp.sum((step[:, None] >= step_end[None, :]).astype(i32), axis=1), N_EXPERTS - 1)
    step_valid = (step < step_end[-1]).astype(i32)
    lookup = lambda onehot, table: jnp.dot(onehot, table.astype(F32), precision=lax.Precision.HIGHEST)
    piece = jnp.arange(n_steps * PIECES_PER_STEP, dtype=i32)
    is_expert = (jnp.repeat(step_expert, PIECES_PER_STEP)[:, None] == jnp.arange(N_EXPERTS, dtype=i32)).astype(F32)
    k_local = piece.astype(F32) - lookup(is_expert, region)
    ends = lookup(is_expert, jnp.cumsum(run_pieces, axis=0).T)
    tile_of = jnp.minimum(jnp.sum((ends <= k_local[:, None]).astype(i32), axis=1), n_tiles - 1)
    is_tile = (tile_of[:, None] == jnp.arange(n_tiles, dtype=i32)).astype(F32)
    j_in_run = k_local - jnp.sum(is_tile * lookup(is_expert, before.T), axis=1)
    slot_start = jnp.sum(is_tile * lookup(is_expert, run_slot.T), axis=1)
    real = (k_local < lookup(is_expert, expert_pieces)) & (jnp.repeat(step_valid, PIECES_PER_STEP) == 1)
    zero_piece = SLOT_ROWS - PIECE
    piece_src = jnp.where(real, tile_of * SLOT_ROWS + (slot_start + j_in_run * PIECE).astype(i32), zero_piece)

    flat = lambda a: a.reshape(-1).astype(i32)
    ys = _experts(xs.reshape(n_tiles * SLOT_ROWS, d + LANES), step_expert, step_valid, flat(piece_src), wi, wo)
    out = _combine(ys, flat(run_pos), flat(run_pieces), flat(jnp.sum(run_pieces, axis=1)), info.reshape(t, LANES),
                   x1.reshape(t, d), gt_m, gf, l // SORT_TM)
    return out.reshape(b, l, d)


def kernel(x, c, ctx, c_ctx, w_ada, b_ada, norm1_g, w_in, w_decay, b_decay, gla_norm_g, w_pool, pool_scale, w_out,
           norm2_g, w_router_group, w_router_expert, w_expert_in, w_expert_out, final_norm_g):
    assert w_ada.shape[0] == 1, "single-layer trunk"
    b, l, d = x.shape
    off_a = 2 * QK_W + 2 * V_W
    a_cols = 2 * GATE_RANK
    off_p = off_a + a_cols

    c8 = jnp.zeros((8, d), F32).at[:b].set(c).at[b].set(c_ctx)
    mod = _ada_mod(c8, w_ada[0], b_ada[0])
    sh_a, sc_a, gt_a, sh_m, sc_m, gt_m = [m.reshape(8, 1, d) for m in jnp.split(mod, 6, axis=-1)]
    ctx_rows = lambda m: jnp.broadcast_to(m[b:b + 1], (b, 1, d))

    w = w_in[0]
    w_r = jnp.concatenate([w[:, :off_a], w[:, off_p:off_p + POOL_W], w[:, off_a:off_p],
                           jnp.zeros((d, LANES - a_cols), F32)], axis=1).astype(BF16)
    wdec = jnp.zeros((LANES, 2 * QK_W), F32)
    wdec = wdec.at[:GATE_RANK, :QK_W].set(w_decay[0, 0]).at[GATE_RANK:a_cols, QK_W:].set(w_decay[0, 1]).astype(BF16)
    bdec = b_decay[0].reshape(1, 2 * QK_W)
    g1 = norm1_g[0].reshape(1, d)

    q_c, k_c, v_c, _, la_c, _ = _inproj(ctx, ctx_rows(sh_a), ctx_rows(sc_a), g1, w_r, wdec, bdec, tm=ctx.shape[1])
    zero_s = jnp.zeros((b, QK_W, GLA_DV), F32)
    _, _, s_f, s_b = _gla(q_c, k_c, v_c, la_c, zero_s, zero_s, tb=ctx.shape[1])

    q, k, v, g, la, p = _inproj(x, sh_a[:b], sc_a[:b], g1, w_r, wdec, bdec, tm=1024)
    o_f, o_b, _, _ = _gla(q, k, v, la, s_f, s_b, tb=1024)
    pooled = _pool(p, w_pool[0], pool_scale[0])

    w_router = jnp.zeros((ROUTER_ROWS, d), F32)
    w_router = w_router.at[:N_EXPERTS].set(w_router_expert[0].T)
    w_router = w_router.at[ROUTER_GROUP_ROW0:ROUTER_GROUP_ROW0 + N_GROUPS].set(w_router_group[0].T)
    wr_hi = w_router.astype(BF16)
    wr_lo = (w_router - wr_hi.astype(F32)).astype(BF16)
    x1, xs, info, counts = _outproj(o_f, o_b, g, pooled, x, gla_norm_g[0].reshape(1, GLA_DV), w_out[0].astype(BF16),
                                    gt_a[:b], sh_m[:b], sc_m[:b], norm2_g[0].reshape(1, d), wr_hi, wr_lo,
                                    tm=SORT_TM)

    return _sparse_moe(xs, info, counts, x1, w_expert_in[0], w_expert_out[0], gt_m[:b], final_norm_g.reshape(1, d))
```

```python
import functools

import jax
import jax.numpy as jnp
from jax import lax
from jax.experimental import pallas as pl
from jax.experimental.pallas import tpu as pltpu

F32 = jnp.float32
BF16 = jnp.bfloat16

GRID_W = 64
GRID_W_LOG2 = 6
assert 1 << GRID_W_LOG2 == GRID_W
GLA_HEADS = 4
GLA_DK = 64
GLA_DV = 128
GATE_RANK = 16
GATE_NORMALIZER = 16.0
CHUNK = 64
POOL_WINDOWS = (2, 4, 8, 16)
POOL_GROUP = 128
N_GROUPS = 4
EXPERTS_PER_GROUP = 8
N_EXPERTS = N_GROUPS * EXPERTS_PER_GROUP
D_EXPERT = 256
EPS = 1e-6

QK_W = GLA_HEADS * GLA_DK
V_W = GLA_HEADS * GLA_DV
POOL_W = POOL_GROUP * len(POOL_WINDOWS)
LANES = 128
VMEM_LIMIT = 48 * 1024 * 1024

_NT = (((1,), (1,)), ((), ()))


def _dot(a, b):
    return jnp.dot(a, b, preferred_element_type=F32)


def _split_bf16(x):
    hi = x.astype(BF16)
    lo = (x - hi.astype(F32)).astype(BF16)
    return hi, lo


def _rms(x):
    return x * lax.rsqrt(jnp.mean(x * x, axis=-1, keepdims=True) + EPS)


def _silu(x):
    return x / (1.0 + jnp.exp(-x))


def _params(*sem):
    return pltpu.CompilerParams(dimension_semantics=sem, vmem_limit_bytes=VMEM_LIMIT)


def _ada_kernel(c_ref, w_ref, b_ref, o_ref):
    s = _silu(c_ref[...]).astype(BF16)
    o_ref[...] = _dot(s, w_ref[...].astype(BF16)) + b_ref[...]


def _ada_mod(c8, w_ada, b_ada):
    rows, d = c8.shape
    n = w_ada.shape[1]
    tn = 1024
    return pl.pallas_call(
        _ada_kernel,
        grid=(n // tn,),
        in_specs=[pl.BlockSpec((rows, d), lambda j: (0, 0)),
                  pl.BlockSpec((d, tn), lambda j: (0, j)),
                  pl.BlockSpec((1, tn), lambda j: (0, j))],
        out_specs=pl.BlockSpec((rows, tn), lambda j: (0, j)),
        out_shape=jax.ShapeDtypeStruct((rows, n), F32),
        compiler_params=_params("arbitrary"),
        name="ada_mod",
    )(c8, w_ada, b_ada.reshape(1, n))


def _inproj_kernel(x_ref, sh_ref, sc_ref, g1_ref, w_ref, wdec_ref, bdec_ref,
                   q_ref, k_ref, v_ref, g_ref, la_ref, p_ref):
    x = x_ref[0]
    h = _rms(x) * g1_ref[...]
    h = h * (1.0 + sc_ref[0]) + sh_ref[0]
    hb = h.astype(BF16)
    o = 0
    q_ref[0] = (_dot(hb, w_ref[:, o:o + QK_W]) * (GLA_DK ** -0.5)).astype(BF16)
    o += QK_W
    k_ref[0] = _dot(hb, w_ref[:, o:o + QK_W]).astype(BF16)
    o += QK_W
    v_ref[0] = _dot(hb, w_ref[:, o:o + V_W]).astype(BF16)
    o += V_W
    g_ref[0] = _dot(hb, w_ref[:, o:o + V_W]).astype(BF16)
    o += V_W
    p_ref[0] = _dot(hb, w_ref[:, o:o + POOL_W])
    o += POOL_W
    a_low = _dot(hb, w_ref[:, o:o + LANES])
    z = _dot(a_low.astype(BF16), wdec_ref[...]) + bdec_ref[...]
    log_sig = jnp.minimum(z, 0.0) - jnp.log(1.0 + jnp.exp(-jnp.abs(z)))
    la_ref[0] = log_sig / GATE_NORMALIZER


def _inproj(x, shift, scale, g1, w_r, wdec, bdec, tm):
    b, l, d = x.shape
    wcols = w_r.shape[1]
    tok = lambda bi, i: (bi, i, 0)
    vec = lambda bi, i: (bi, 0, 0)
    fixed = lambda bi, i: (0, 0)
    outs = [(QK_W, BF16), (QK_W, BF16), (V_W, BF16), (V_W, BF16), (2 * QK_W, F32), (POOL_W, F32)]
    return pl.pallas_call(
        _inproj_kernel,
        grid=(b, l // tm),
        in_specs=[pl.BlockSpec((1, tm, d), tok),
                  pl.BlockSpec((1, 1, d), vec),
                  pl.BlockSpec((1, 1, d), vec),
                  pl.BlockSpec((1, d), fixed),
                  pl.BlockSpec((d, wcols), fixed),
                  pl.BlockSpec(wdec.shape, fixed),
                  pl.BlockSpec(bdec.shape, fixed)],
        out_specs=[pl.BlockSpec((1, tm, w), tok) for w, _ in outs],
        out_shape=[jax.ShapeDtypeStruct((b, l, w), dt) for w, dt in outs],
        compiler_params=_params("arbitrary", "arbitrary"),
        name="inproj",
    )(x, shift, scale, g1, w_r, wdec, bdec)


GLA_GROUP = 256
CHUNKS_PER_GROUP = GLA_GROUP // CHUNK
CHUNK_LOG2 = 6
assert 1 << CHUNK_LOG2 == CHUNK


def _head_masks(rows):
    lane = lax.broadcasted_iota(jnp.int32, (rows, QK_W), 1)
    return [(lane >= h * GLA_DK) & (lane < (h + 1) * GLA_DK) for h in range(GLA_HEADS)]


def _gla_bulk(d, g, reverse, q_ref, k_ref, v_ref, la_ref, qd_scr, oi_scr, ds_scr, dc_scr):
    n = GLA_GROUP
    r = pl.multiple_of(g * n, n)
    i0 = lax.broadcasted_iota(jnp.int32, (n, n), 0)
    i1 = lax.broadcasted_iota(jnp.int32, (n, n), 1)
    causal = ((i0 >> CHUNK_LOG2) == (i1 >> CHUNK_LOG2)) & ((i1 >= i0) if reverse else (i1 <= i0))
    tri = jnp.where(causal, 1.0, 0.0).astype(BF16)
    la_hi, la_lo = _split_bf16(la_ref[0, pl.ds(r, n), :])
    b = _dot(tri, la_hi) + _dot(tri, la_lo)
    last_row = 0 if reverse else CHUNK - 1
    b_end = b.reshape(CHUNKS_PER_GROUP, CHUNK, QK_W)[:, last_row:last_row + 1, :]
    b_last = jnp.broadcast_to(b_end, (CHUNKS_PER_GROUP, CHUNK, QK_W)).reshape(n, QK_W)
    q = q_ref[0, pl.ds(r, n), :].astype(F32)
    k = k_ref[0, pl.ds(r, n), :].astype(F32)
    qd = (q * jnp.exp(b)).astype(BF16)
    kd = (k * jnp.exp(-b)).astype(BF16)
    k_st = k * jnp.exp(b_last - b)
    qd_scr[d, pl.ds(r, n), :] = qd
    v = v_ref[0, pl.ds(r, n), :]
    for h, m in enumerate(_head_masks(n)):
        scores = lax.dot_general(jnp.where(m, qd, jnp.zeros_like(qd)), kd, _NT, preferred_element_type=F32)
        probs = jnp.where(causal, scores, 0.0).astype(BF16)
        oi_scr[d, pl.ds(r, n), h * GLA_DV:(h + 1) * GLA_DV] = _dot(probs, v[:, h * GLA_DV:(h + 1) * GLA_DV])
    decay = jnp.exp(b_end)
    for c in range(CHUNKS_PER_GROUP):
        ci = g * CHUNKS_PER_GROUP + c
        k_st_t = k_st[c * CHUNK:(c + 1) * CHUNK].T.astype(BF16)
        for h in range(GLA_HEADS):
            ds_scr[d, ci, h * GLA_DK:(h + 1) * GLA_DK, :] = _dot(
                k_st_t[h * GLA_DK:(h + 1) * GLA_DK], v[c * CHUNK:(c + 1) * CHUNK, h * GLA_DV:(h + 1) * GLA_DV])
        dc_scr[d, ci] = jnp.broadcast_to(decay[c], (LANES, QK_W)).T


def _gla_kernel(qf_ref, kf_ref, vf_ref, laf_ref, qb_ref, kb_ref, vb_ref, lab_ref, s0f_ref, s0b_ref,
                of_ref, ob_ref, sf_out_ref, sb_out_ref, s_ref, qd_scr, oi_scr, ds_scr, dc_scr, st_scr, *, n_chunks):
    j = pl.program_id(1)

    @pl.when(j == 0)
    def _():
        s_ref[0] = s0f_ref[0]
        s_ref[1] = s0b_ref[0]

    def bulk(g, carry):
        _gla_bulk(0, g, False, qf_ref, kf_ref, vf_ref, laf_ref, qd_scr, oi_scr, ds_scr, dc_scr)
        _gla_bulk(1, g, True, qb_ref, kb_ref, vb_ref, lab_ref, qd_scr, oi_scr, ds_scr, dc_scr)
        return carry
    lax.fori_loop(0, n_chunks // CHUNKS_PER_GROUP, bulk, 0)

    def recur(i, carry):
        for d, ci in ((0, i), (1, n_chunks - 1 - i)):
            s = s_ref[d]
            st_scr[d, ci] = s.astype(BF16)
            s_ref[d] = dc_scr[d, ci] * s + ds_scr[d, ci]
        return carry
    lax.fori_loop(0, n_chunks, recur, 0)

    masks = _head_masks(CHUNK)

    def inter(g, carry):
        for d, o_ref in ((0, of_ref), (1, ob_ref)):
            for c in range(CHUNKS_PER_GROUP):
                ci = g * CHUNKS_PER_GROUP + c
                r = pl.multiple_of(ci * CHUNK, CHUNK)
                qd = qd_scr[d, pl.ds(r, CHUNK), :]
                lhs = jnp.concatenate([jnp.where(m, qd, jnp.zeros_like(qd)) for m in masks], axis=0)
                from_state = _dot(lhs, st_scr[d, ci])
                for h in range(GLA_HEADS):
                    cols = slice(h * GLA_DV, (h + 1) * GLA_DV)
                    o_ref[0, pl.ds(r, CHUNK), cols] = (oi_scr[d, pl.ds(r, CHUNK), cols]
                                                       + from_state[h * CHUNK:(h + 1) * CHUNK]).astype(BF16)
        return carry
    lax.fori_loop(0, n_chunks // CHUNKS_PER_GROUP, inter, 0)

    @pl.when(j == pl.num_programs(1) - 1)
    def _():
        sf_out_ref[0] = s_ref[0]
        sb_out_ref[0] = s_ref[1]


def _gla(q, k, v, la, s0f, s0b, tb):
    b, l, _ = q.shape
    assert tb % GLA_GROUP == 0 and l % tb == 0
    nb = l // tb
    n_chunks = tb // CHUNK
    fwd = lambda bi, j: (bi, j, 0)
    bwd = lambda bi, j: (bi, nb - 1 - j, 0)
    fwd_la = lambda bi, j: (bi, j, 0)
    bwd_la = lambda bi, j: (bi, nb - 1 - j, 1)
    st = lambda bi, j: (bi, 0, 0)
    state_shape = (b, QK_W, GLA_DV)
    return pl.pallas_call(
        functools.partial(_gla_kernel, n_chunks=tb // CHUNK),
        grid=(b, nb),
        in_specs=[pl.BlockSpec((1, tb, QK_W), fwd), pl.BlockSpec((1, tb, QK_W), fwd),
                  pl.BlockSpec((1, tb, V_W), fwd), pl.BlockSpec((1, tb, QK_W), fwd_la),
                  pl.BlockSpec((1, tb, QK_W), bwd), pl.BlockSpec((1, tb, QK_W), bwd),
                  pl.BlockSpec((1, tb, V_W), bwd), pl.BlockSpec((1, tb, QK_W), bwd_la),
                  pl.BlockSpec((1, QK_W, GLA_DV), st), pl.BlockSpec((1, QK_W, GLA_DV), st)],
        out_specs=[pl.BlockSpec((1, tb, V_W), fwd), pl.BlockSpec((1, tb, V_W), bwd),
                   pl.BlockSpec((1, QK_W, GLA_DV), st), pl.BlockSpec((1, QK_W, GLA_DV), st)],
        out_shape=[jax.ShapeDtypeStruct((b, l, V_W), BF16), jax.ShapeDtypeStruct((b, l, V_W), BF16),
                   jax.ShapeDtypeStruct(state_shape, F32), jax.ShapeDtypeStruct(state_shape, F32)],
        scratch_shapes=[pltpu.VMEM((2, QK_W, GLA_DV), F32),
                        pltpu.VMEM((2, tb, QK_W), BF16),
                        pltpu.VMEM((2, tb, V_W), F32),
                        pltpu.VMEM((2, n_chunks, QK_W, GLA_DV), F32),
                        pltpu.VMEM((2, n_chunks, QK_W, GLA_DV), F32),
                        pltpu.VMEM((2, n_chunks, QK_W, GLA_DV), BF16)],
        compiler_params=_params("arbitrary", "arbitrary"),
        name="gla",
    )(q, k, v, la, q, k, v, la, s0f, s0b)


POOL_BLK = 256
POOL_COL_STEP = 1024
POOL_ROW_STEP = 256
POOL_OUT_STEP = 1024


def _pool_body(w, first_batch, p_ref, wp_ref, ps_ref, o_ref, buf_ref, inv_ref, y_ref, n_tok):
    lo = w // 2
    hi = w - 1 - lo
    pad = lo * GRID_W
    n_rows = n_tok // GRID_W
    blk = POOL_BLK

    @pl.when(first_batch)
    def _():
        zeros = jnp.zeros((GRID_W * 8, LANES), F32)
        buf_ref[0:pad, :] = zeros[0:pad]
        buf_ref[pad + n_tok:pad + n_tok + hi * GRID_W + GRID_W, :] = zeros[0:hi * GRID_W + GRID_W]

        def inv_step(t, carry):
            r = pl.multiple_of(t * POOL_OUT_STEP, POOL_OUT_STEP)
            tok = r + lax.broadcasted_iota(jnp.int32, (POOL_OUT_STEP, LANES), 0)
            g_row = tok >> GRID_W_LOG2
            g_col = tok & (GRID_W - 1)
            cnt_c = jnp.minimum(g_col + hi, GRID_W - 1) - jnp.maximum(g_col - lo, 0) + 1
            cnt_r = jnp.minimum(g_row + hi, n_rows - 1) - jnp.maximum(g_row - lo, 0) + 1
            inv_ref[pl.ds(r, POOL_OUT_STEP), :] = 1.0 / (cnt_c * cnt_r).astype(F32)
            return carry
        lax.fori_loop(0, n_tok // POOL_OUT_STEP, inv_step, 0)

    i0 = lax.broadcasted_iota(jnp.int32, (blk, blk), 0)
    i1 = lax.broadcasted_iota(jnp.int32, (blk, blk), 1)
    same_row = (i0 >> GRID_W_LOG2) == (i1 >> GRID_W_LOG2)
    band = jnp.where(same_row & (i1 - i0 >= -lo) & (i1 - i0 <= hi), 1.0, 0.0).astype(BF16)

    def col_step(t, carry):
        for u in range(POOL_COL_STEP // blk):
            r = pl.multiple_of(t * POOL_COL_STEP + u * blk, blk)
            x_hi, x_lo = _split_bf16(p_ref[0, pl.ds(r, blk), :])
            s = _dot(band, jnp.concatenate([x_hi, x_lo], axis=-1))
            buf_ref[pl.ds(pad + r, blk), :] = s[:, :LANES] + s[:, LANES:]
        return carry
    lax.fori_loop(0, n_tok // POOL_COL_STEP, col_step, 0)

    def row_step(t, carry):
        r = pl.multiple_of(t * POOL_ROW_STEP, POOL_ROW_STEP)
        acc = buf_ref[pl.ds(r, POOL_ROW_STEP), :]
        for d in range(1, w):
            acc = acc + buf_ref[pl.ds(r + d * GRID_W, POOL_ROW_STEP), :]
        m = acc * inv_ref[pl.ds(r, POOL_ROW_STEP), :]
        y_ref[pl.ds(r, POOL_ROW_STEP), :] = (m - p_ref[0, pl.ds(r, POOL_ROW_STEP), :]).astype(BF16)
        return carry
    lax.fori_loop(0, n_tok // POOL_ROW_STEP, row_step, 0)

    wp = wp_ref[0].astype(BF16)
    scale = ps_ref[0]

    def out_step(t, carry):
        r = pl.multiple_of(t * POOL_OUT_STEP, POOL_OUT_STEP)
        o_ref[0, pl.ds(r, POOL_OUT_STEP), :] = (_dot(y_ref[pl.ds(r, POOL_OUT_STEP), :], wp) * scale).astype(BF16)
        return carry
    lax.fori_loop(0, n_tok // POOL_OUT_STEP, out_step, 0)


def _pool_kernel(p_ref, wp_ref, ps_ref, o_ref, buf_ref, inv_ref, y_ref, *, n_tok):
    g = pl.program_id(0)
    first_batch = pl.program_id(1) == 0
    for gi, w in enumerate(POOL_WINDOWS):
        @pl.when(g == gi)
        def _(w=w):
            _pool_body(w, first_batch, p_ref, wp_ref, ps_ref, o_ref, buf_ref, inv_ref, y_ref, n_tok)


def _pool(p, w_pool, pool_scale):
    b, l, _ = p.shape
    n_g = len(POOL_WINDOWS)
    max_w = max(POOL_WINDOWS)
    buf_rows = l + (max_w + 1) * GRID_W
    return pl.pallas_call(
        functools.partial(_pool_kernel, n_tok=l),
        grid=(n_g, b),
        in_specs=[pl.BlockSpec((1, l, POOL_GROUP), lambda g, bi: (bi, 0, g)),
                  pl.BlockSpec((1, POOL_GROUP, POOL_GROUP), lambda g, bi: (g, 0, 0)),
                  pl.BlockSpec((1, 1, POOL_GROUP), lambda g, bi: (g, 0, 0))],
        out_specs=pl.BlockSpec((1, l, POOL_GROUP), lambda g, bi: (bi, 0, g)),
        out_shape=jax.ShapeDtypeStruct((b, l, POOL_W), BF16),
        scratch_shapes=[pltpu.VMEM((buf_rows, LANES), F32), pltpu.VMEM((l, LANES), F32),
                        pltpu.VMEM((l, LANES), BF16)],
        compiler_params=_params("arbitrary", "arbitrary"),
        name="pool",
    )(p, w_pool, pool_scale.reshape(n_g, 1, POOL_GROUP))


ROUTER_ROWS = 48
ROUTER_GROUP_ROW0 = N_EXPERTS
TOP_K = 2
SORT_TM = 512
PIECE = 16
SLOT_ROWS = TOP_K * SORT_TM + N_EXPERTS * PIECE
META_W1_HI, META_W1_LO, META_W2_HI, META_W2_LO, META_E1, META_E2 = range(6)
INFO_DEST1, INFO_DEST2 = 6, 7


def _outproj_kernel(of_ref, ob_ref, g_ref, pooled_ref, x_ref, gn_ref, wout_ref, gt_ref, sh_ref, sc_ref, g2_ref,
                    wr_hi_ref, wr_lo_ref, x1_ref, xs_ref, info_ref, cnt_ref):
    o = of_ref[0].astype(F32) + ob_ref[0].astype(F32)
    gate = _silu(g_ref[0].astype(F32))
    parts = []
    for h in range(GLA_HEADS):
        sl = slice(h * GLA_DV, (h + 1) * GLA_DV)
        parts.append((_rms(o[:, sl]) * gn_ref[...] * gate[:, sl]).astype(BF16))
    o_n = jnp.concatenate(parts, axis=-1)
    out = _dot(o_n, wout_ref[0:V_W, :]) + _dot(pooled_ref[0], wout_ref[V_W:V_W + POOL_W, :])
    x1 = x_ref[0] + gt_ref[0] * out
    x1_ref[0] = x1
    h2 = _rms(x1) * g2_ref[...]
    h2 = h2 * (1.0 + sc_ref[0]) + sh_ref[0]
    h2_bf = h2.astype(BF16)
    h2_lo = (h2 - h2_bf.astype(F32)).astype(BF16)
    logits = (lax.dot_general(wr_hi_ref[...], h2_bf, _NT, preferred_element_type=F32)
              + lax.dot_general(wr_hi_ref[...], h2_lo, _NT, preferred_element_type=F32)
              + lax.dot_general(wr_lo_ref[...], h2_bf, _NT, preferred_element_type=F32))
    tm = logits.shape[1]
    row = lax.broadcasted_iota(jnp.int32, logits.shape, 0)
    neg = -jnp.inf
    big = jnp.int32(1 << 20)
    is_group = (row >= ROUTER_GROUP_ROW0) & (row < ROUTER_GROUP_ROW0 + N_GROUPS)
    gl = jnp.where(is_group, logits, neg)
    g_max = jnp.max(gl, axis=0, keepdims=True)
    g_idx = jnp.min(jnp.where(gl == g_max, row - ROUTER_GROUP_ROW0, big), axis=0, keepdims=True)
    g_prob = 1.0 / jnp.sum(jnp.where(is_group, jnp.exp(gl - g_max), 0.0), axis=0, keepdims=True)
    n_e = EXPERTS_PER_GROUP
    el = logits[0:n_e]
    for g in range(1, N_GROUPS):
        el = jnp.where(g_idx == g, logits[g * n_e:(g + 1) * n_e], el)
    r8 = lax.broadcasted_iota(jnp.int32, el.shape, 0)
    m1 = jnp.max(el, axis=0, keepdims=True)
    i1 = jnp.min(jnp.where(el == m1, r8, big), axis=0, keepdims=True)
    el2 = jnp.where(r8 == i1, neg, el)
    m2 = jnp.max(el2, axis=0, keepdims=True)
    i2 = jnp.min(jnp.where(el2 == m2, r8, big), axis=0, keepdims=True)
    r = jnp.exp(m2 - m1)
    w1 = g_prob / (1.0 + r)
    w2 = g_prob * r / (1.0 + r)

    e1 = g_idx * n_e + i1
    e2 = g_idx * n_e + i2
    r_e = lax.broadcasted_iota(jnp.int32, (N_EXPERTS, tm), 0)
    pick1 = r_e == e1
    pick2 = r_e == e2
    onehot = jnp.where(pick1 | pick2, 1.0, 0.0)
    t_r = lax.broadcasted_iota(jnp.int32, (tm, tm), 0)
    t_c = lax.broadcasted_iota(jnp.int32, (tm, tm), 1)
    rank = _dot(onehot.astype(BF16), jnp.where(t_r < t_c, 1.0, 0.0).astype(BF16))
    count = jnp.sum(onehot, axis=1, keepdims=True)
    n_pieces = jnp.floor((count + (PIECE - 1)) * (1.0 / PIECE))
    e_r = lax.broadcasted_iota(jnp.int32, (N_EXPERTS, N_EXPERTS), 0)
    e_c = lax.broadcasted_iota(jnp.int32, (N_EXPERTS, N_EXPERTS), 1)
    run_start = _dot(jnp.where(e_c < e_r, 1.0, 0.0).astype(BF16),
                     jnp.broadcast_to(n_pieces, (N_EXPERTS, LANES)).astype(BF16))[:, 0:1] * PIECE
    slot = rank + run_start
    dest1 = jnp.sum(jnp.where(pick1, slot, 0.0), axis=0, keepdims=True)
    dest2 = jnp.sum(jnp.where(pick2, slot, 0.0), axis=0, keepdims=True)
    w1_hi = w1.astype(BF16).astype(F32)
    w2_hi = w2.astype(BF16).astype(F32)
    info_rows = [w1_hi, (w1 - w1_hi).astype(BF16).astype(F32), w2_hi, (w2 - w2_hi).astype(BF16).astype(F32),
                 e1.astype(F32), e2.astype(F32), dest1, dest2]
    info_t = jnp.concatenate(info_rows + [jnp.zeros((LANES - len(info_rows), tm), F32)], axis=0)
    info = info_t.T
    info_ref[0] = info
    cnt_ref[0, 0] = jnp.broadcast_to(count, (N_EXPERTS, LANES)).astype(jnp.int32)
    s_id = lax.broadcasted_iota(jnp.int32, (SLOT_ROWS, tm), 0)
    perm_t = jnp.where((s_id == dest1.astype(jnp.int32)) | (s_id == dest2.astype(jnp.int32)),
                       1.0, 0.0).astype(BF16)
    lane = lax.broadcasted_iota(jnp.int32, info.shape, 1)
    meta = jnp.where(lane < INFO_DEST1, info, 0.0).astype(BF16)
    d = h2_bf.shape[1]
    xs_ref[0, 0, :, 0:d] = _dot(perm_t, h2_bf).astype(BF16)
    xs_ref[0, 0, :, d:d + LANES] = _dot(perm_t, meta).astype(BF16)


def _outproj(o_f, o_b, g, pooled, x, gn, w_out, gt_a, sh_m, sc_m, g2, wr_hi, wr_lo, tm):
    b, l, d = x.shape
    tok = lambda bi, i: (bi, i, 0)
    vec = lambda bi, i: (bi, 0, 0)
    fixed = lambda bi, i: (0, 0)
    return pl.pallas_call(
        _outproj_kernel,
        grid=(b, l // tm),
        in_specs=[pl.BlockSpec((1, tm, V_W), tok), pl.BlockSpec((1, tm, V_W), tok),
                  pl.BlockSpec((1, tm, V_W), tok), pl.BlockSpec((1, tm, POOL_W), tok),
                  pl.BlockSpec((1, tm, d), tok),
                  pl.BlockSpec((1, GLA_DV), fixed),
                  pl.BlockSpec(w_out.shape, fixed),
                  pl.BlockSpec((1, 1, d), vec), pl.BlockSpec((1, 1, d), vec), pl.BlockSpec((1, 1, d), vec),
                  pl.BlockSpec((1, d), fixed),
                  pl.BlockSpec(wr_hi.shape, fixed), pl.BlockSpec(wr_lo.shape, fixed)],
        out_specs=[pl.BlockSpec((1, tm, d), tok),
                   pl.BlockSpec((1, 1, SLOT_ROWS, d + LANES), lambda bi, i: (bi, i, 0, 0)),
                   pl.BlockSpec((1, tm, LANES), tok),
                   pl.BlockSpec((1, 1, N_EXPERTS, LANES), lambda bi, i: (bi, i, 0, 0))],
        out_shape=[jax.ShapeDtypeStruct((b, l, d), F32),
                   jax.ShapeDtypeStruct((b, l // tm, SLOT_ROWS, d + LANES), BF16),
                   jax.ShapeDtypeStruct((b, l, LANES), F32),
                   jax.ShapeDtypeStruct((b, l // tm, N_EXPERTS, LANES), jnp.int32)],
        compiler_params=_params("arbitrary", "arbitrary"),
        name="outproj_router",
    )(o_f, o_b, g, pooled, x, gn, w_out, gt_a, sh_m, sc_m, g2, wr_hi, wr_lo)


EXPERT_TM = 512
PIECES_PER_STEP = EXPERT_TM // PIECE
GATHER_SLOTS = 6
SLOT_PIECES = SLOT_ROWS // PIECE


def _expert_steps(n_tok):
    pieces = TOP_K * n_tok // PIECE + (n_tok // SORT_TM) * N_EXPERTS
    return -(-pieces // PIECES_PER_STEP) + N_EXPERTS + 1


def _experts_kernel(exp_ref, valid_ref, src_ref, xs_hbm, wi_ref, wo_ref, ys_ref, xbuf, wi_bf, wo_bf, sem):
    i = pl.program_id(0)
    n_steps = pl.num_programs(0)
    slot = i % GATHER_SLOTS
    d = ys_ref.shape[1]

    def piece_copy(s, j, src_row):
        return pltpu.make_async_copy(xs_hbm.at[pl.ds(src_row, PIECE), :], xbuf.at[s, pl.ds(j * PIECE, PIECE), :],
                                     sem.at[s])

    def fetch(step):
        @pl.when((step < n_steps) & (valid_ref[jnp.minimum(step, n_steps - 1)] == 1))
        def _():
            for j in range(PIECES_PER_STEP):
                piece_copy(step % GATHER_SLOTS, j,
                           pl.multiple_of(src_ref[step * PIECES_PER_STEP + j], PIECE)).start()

    @pl.when(i == 0)
    def _():
        for ahead in range(GATHER_SLOTS - 1):
            fetch(jnp.int32(ahead))

    fetch(i + GATHER_SLOTS - 1)
    expert = exp_ref[i]

    @pl.when((i == 0) | (exp_ref[jnp.maximum(i - 1, 0)] != expert))
    def _():
        wi_bf[...] = wi_ref[0].astype(BF16)
        wo_bf[...] = wo_ref[0].astype(BF16)

    @pl.when(valid_ref[i] == 1)
    def _():
        for j in range(PIECES_PER_STEP):
            piece_copy(slot, j, 0).wait()
        x = xbuf[slot, :, 0:d]
        meta = xbuf[slot, :, d:d + LANES].astype(F32)
        lane = lax.broadcasted_iota(jnp.int32, meta.shape, 1)

        def pick(*lanes):
            sel = lane == lanes[0]
            for ln in lanes[1:]:
                sel = sel | (lane == ln)
            return jnp.sum(jnp.where(sel, meta, 0.0), axis=-1, keepdims=True)

        is_first = pick(META_E1) == expert.astype(F32)
        gate = jnp.where(is_first, pick(META_W1_HI, META_W1_LO), pick(META_W2_HI, META_W2_LO))
        au = _dot(x, wi_bf[...])
        hidden = (_silu(au[:, :D_EXPERT]) * au[:, D_EXPERT:] * gate).astype(BF16)
        ys_ref[...] = _dot(hidden, wo_bf[...]).astype(BF16)

    @pl.when(valid_ref[i] == 0)
    def _():
        ys_ref[...] = jnp.zeros_like(ys_ref)


def _experts(xs, step_expert, step_valid, piece_src, wi, wo):
    n_steps = step_expert.shape[0]
    width = xs.shape[1]
    d = wo.shape[2]
    return pl.pallas_call(
        _experts_kernel,
        grid_spec=pltpu.PrefetchScalarGridSpec(
            num_scalar_prefetch=3,
            grid=(n_steps,),
            in_specs=[pl.BlockSpec(memory_space=pl.ANY),
                      pl.BlockSpec((1,) + wi.shape[1:], lambda i, e, v, s: (e[i], 0, 0)),
                      pl.BlockSpec((1,) + wo.shape[1:], lambda i, e, v, s: (e[i], 0, 0))],
            out_specs=pl.BlockSpec((EXPERT_TM, d), lambda i, e, v, s: (i, 0)),
            scratch_shapes=[pltpu.VMEM((GATHER_SLOTS, EXPERT_TM, width), BF16), pltpu.VMEM(wi.shape[1:], BF16),
                            pltpu.VMEM(wo.shape[1:], BF16), pltpu.SemaphoreType.DMA((GATHER_SLOTS,))]),
        out_shape=jax.ShapeDtypeStruct((n_steps * EXPERT_TM, d), BF16),
        compiler_params=_params("arbitrary"),
        name="moe_experts",
    )(step_expert, step_valid, piece_src, xs, wi, wo)


def _combine_kernel(src_ref, ys_hbm, info_ref, x1_ref, gt_ref, gf_ref, o_ref, stage, sem):
    i = pl.program_id(0)
    n_steps = pl.num_programs(0)
    tm = SORT_TM

    def piece_copy(s, src_row, dst_row):
        return pltpu.make_async_copy(ys_hbm.at[pl.ds(src_row, PIECE), :], stage.at[s, pl.ds(dst_row, PIECE), :],
                                     sem.at[s])

    def fetch(tile, s):
        for m in range(SLOT_PIECES):
            piece_copy(s, pl.multiple_of(src_ref[tile * SLOT_PIECES + m], PIECE), m * PIECE).start()

    def wait(s):
        for m in range(SLOT_PIECES):
            piece_copy(s, 0, m * PIECE).wait()

    @pl.when(i == 0)
    def _():
        fetch(0, 0)

    slot = i % 2
    fetch(jnp.minimum(i + 1, n_steps - 1), 1 - slot)
    wait(slot)
    ys = stage[slot]
    info = info_ref[...]
    lane = lax.broadcasted_iota(jnp.int32, info.shape, 1)
    dest1 = jnp.sum(jnp.where(lane == INFO_DEST1, info, 0.0), axis=-1, keepdims=True).astype(jnp.int32)
    dest2 = jnp.sum(jnp.where(lane == INFO_DEST2, info, 0.0), axis=-1, keepdims=True).astype(jnp.int32)
    slot_id = lax.broadcasted_iota(jnp.int32, (tm, SLOT_ROWS), 1)
    perm = jnp.where((slot_id == dest1) | (slot_id == dest2), 1.0, 0.0).astype(BF16)
    y = _dot(perm, ys)
    x2 = x1_ref[...] + gt_ref[0] * y
    o_ref[...] = _rms(x2) * gf_ref[...]

    @pl.when(i == n_steps - 1)
    def _():
        wait(1 - slot)


def _combine(ys, piece_src, info, x1, gt_m, gf, tiles_per_batch):
    t, d = x1.shape
    n_tiles = t // SORT_TM
    tok = lambda i, *_: (i, 0)
    return pl.pallas_call(
        _combine_kernel,
        grid_spec=pltpu.PrefetchScalarGridSpec(
            num_scalar_prefetch=1,
            grid=(n_tiles,),
            in_specs=[pl.BlockSpec(memory_space=pl.ANY),
                      pl.BlockSpec((SORT_TM, LANES), tok),
                      pl.BlockSpec((SORT_TM, d), tok),
                      pl.BlockSpec((1, 1, d), lambda i, *_: (i // tiles_per_batch, 0, 0)),
                      pl.BlockSpec((1, d), lambda i, *_: (0, 0))],
            out_specs=pl.BlockSpec((SORT_TM, d), tok),
            scratch_shapes=[pltpu.VMEM((2, SLOT_ROWS, d), BF16), pltpu.SemaphoreType.DMA((2,))]),
        out_shape=jax.ShapeDtypeStruct((t, d), F32),
        compiler_params=_params("arbitrary"),
        name="moe_combine",
    )(piece_src, ys, info, x1, gt_m, gf)


def _sparse_moe(xs, info, counts, x1, wi, wo, gt_m, gf):
    b, l, d = x1.shape
    t = b * l
    n_tiles = t // SORT_TM
    i32 = jnp.int32

    cnt = counts[:, :, :, 0].reshape(n_tiles, N_EXPERTS)
    run_pieces = (cnt + (PIECE - 1)) // PIECE
    run_slot = (jnp.cumsum(run_pieces, axis=1) - run_pieces) * PIECE
    before = jnp.cumsum(run_pieces, axis=0) - run_pieces
    expert_pieces = jnp.sum(run_pieces, axis=0)
    steps_e = (expert_pieces + PIECES_PER_STEP - 1) // PIECES_PER_STEP
    step_end = jnp.cumsum(steps_e)
    region = (step_end - steps_e) * PIECES_PER_STEP
    run_pos = (region[None, :] + before) * PIECE

    n_steps = _expert_steps(t)
    step = jnp.arange(n_steps, dtype=i32)
    step_expert = jnp.minimum(jnp.sum((step[:, None] >= step_end[None, :]).astype(i32), axis=1), N_EXPERTS - 1)
    step_valid = (step < step_end[-1]).astype(i32)
    lookup = lambda onehot, table: jnp.dot(onehot, table.astype(F32), precision=lax.Precision.HIGHEST)
    piece = jnp.arange(n_steps * PIECES_PER_STEP, dtype=i32)
    is_expert = (jnp.repeat(step_expert, PIECES_PER_STEP)[:, None] == jnp.arange(N_EXPERTS, dtype=i32)).astype(F32)
    k_local = piece.astype(F32) - lookup(is_expert, region)
    ends = lookup(is_expert, jnp.cumsum(run_pieces, axis=0).T)
    tile_of = jnp.minimum(jnp.sum((ends <= k_local[:, None]).astype(i32), axis=1), n_tiles - 1)
    is_tile = (tile_of[:, None] == jnp.arange(n_tiles, dtype=i32)).astype(F32)
    j_in_run = k_local - jnp.sum(is_tile * lookup(is_expert, before.T), axis=1)
    slot_start = jnp.sum(is_tile * lookup(is_expert, run_slot.T), axis=1)
    real = (k_local < lookup(is_expert, expert_pieces)) & (jnp.repeat(step_valid, PIECES_PER_STEP) == 1)
    zero_piece = SLOT_ROWS - PIECE
    piece_src = jnp.where(real, tile_of * SLOT_ROWS + (slot_start + j_in_run * PIECE).astype(i32), zero_piece)


    m = jnp.arange(SLOT_PIECES, dtype=i32)
    run_end = jnp.cumsum(run_pieces, axis=1)
    run_of = jnp.minimum(jnp.sum((run_end[:, None, :] <= m[None, :, None]).astype(i32), axis=-1), N_EXPERTS - 1)
    is_run = run_of[:, :, None] == jnp.arange(N_EXPERTS, dtype=i32)
    pos_m = jnp.sum(jnp.where(is_run, run_pos[:, None, :], 0), axis=-1)
    first_m = jnp.sum(jnp.where(is_run, (run_end - run_pieces)[:, None, :], 0), axis=-1)
    zero_row = (n_steps - 1) * EXPERT_TM
    back_src = jnp.where(m[None, :] < run_end[:, -1:], pos_m + (m[None, :] - first_m) * PIECE, zero_row)

    flat = lambda a: a.reshape(-1).astype(i32)
    ys = _experts(xs.reshape(n_tiles * SLOT_ROWS, d + LANES), step_expert, step_valid, flat(piece_src), wi, wo)
    out = _combine(ys, flat(back_src), info.reshape(t, LANES), x1.reshape(t, d), gt_m, gf, l // SORT_TM)
    return out.reshape(b, l, d)


def kernel(x, c, ctx, c_ctx, w_ada, b_ada, norm1_g, w_in, w_decay, b_decay, gla_norm_g, w_pool, pool_scale, w_out,
           norm2_g, w_router_group, w_router_expert, w_expert_in, w_expert_out, final_norm_g):
    assert w_ada.shape[0] == 1, "single-layer trunk"
    b, l, d = x.shape
    off_a = 2 * QK_W + 2 * V_W
    a_cols = 2 * GATE_RANK
    off_p = off_a + a_cols

    c8 = jnp.zeros((8, d), F32).at[:b].set(c).at[b].set(c_ctx)
    mod = _ada_mod(c8, w_ada[0], b_ada[0])
    sh_a, sc_a, gt_a, sh_m, sc_m, gt_m = [m.reshape(8, 1, d) for m in jnp.split(mod, 6, axis=-1)]
    ctx_rows = lambda m: jnp.broadcast_to(m[b:b + 1], (b, 1, d))

    w = w_in[0]
    w_r = jnp.concatenate([w[:, :off_a], w[:, off_p:off_p + POOL_W], w[:, off_a:off_p],
                           jnp.zeros((d, LANES - a_cols), F32)], axis=1).astype(BF16)
    wdec = jnp.zeros((LANES, 2 * QK_W), F32)
    wdec = wdec.at[:GATE_RANK, :QK_W].set(w_decay[0, 0]).at[GATE_RANK:a_cols, QK_W:].set(w_decay[0, 1]).astype(BF16)
    bdec = b_decay[0].reshape(1, 2 * QK_W)
    g1 = norm1_g[0].reshape(1, d)

    q_c, k_c, v_c, _, la_c, _ = _inproj(ctx, ctx_rows(sh_a), ctx_rows(sc_a), g1, w_r, wdec, bdec, tm=ctx.shape[1])
    zero_s = jnp.zeros((b, QK_W, GLA_DV), F32)
    _, _, s_f, s_b = _gla(q_c, k_c, v_c, la_c, zero_s, zero_s, tb=ctx.shape[1])

    q, k, v, g, la, p = _inproj(x, sh_a[:b], sc_a[:b], g1, w_r, wdec, bdec, tm=1024)
    o_f, o_b, _, _ = _gla(q, k, v, la, s_f, s_b, tb=1024)
    pooled = _pool(p, w_pool[0], pool_scale[0])

    w_router = jnp.zeros((ROUTER_ROWS, d), F32)
    w_router = w_router.at[:N_EXPERTS].set(w_router_expert[0].T)
    w_router = w_router.at[ROUTER_GROUP_ROW0:ROUTER_GROUP_ROW0 + N_GROUPS].set(w_router_group[0].T)
    wr_hi = w_router.astype(BF16)
    wr_lo = (w_router - wr_hi.astype(F32)).astype(BF16)
    x1, xs, info, counts = _outproj(o_f, o_b, g, pooled, x, gla_norm_g[0].reshape(1, GLA_DV), w_out[0].astype(BF16),
                                    gt_a[:b], sh_m[:b], sc_m[:b], norm2_g[0].reshape(1, d), wr_hi, wr_lo,
                                    tm=SORT_TM)

    return _sparse_moe(xs, info, counts, x1, w_expert_in[0], w_expert_out[0], gt_m[:b], final_norm_g.reshape(1, d))
```

```python
import functools

import jax
import jax.numpy as jnp
from jax import lax
from jax.experimental import pallas as pl
from jax.experimental.pallas import tpu as pltpu

F32 = jnp.float32
BF16 = jnp.bfloat16

GRID_W = 64
GRID_W_LOG2 = 6
assert 1 << GRID_W_LOG2 == GRID_W
GLA_HEADS = 4
GLA_DK = 64
GLA_DV = 128
GATE_RANK = 16
GATE_NORMALIZER = 16.0
CHUNK = 64
POOL_WINDOWS = (2, 4, 8, 16)
POOL_GROUP = 128
N_GROUPS = 4
EXPERTS_PER_GROUP = 8
N_EXPERTS = N_GROUPS * EXPERTS_PER_GROUP
D_EXPERT = 256
EPS = 1e-6

QK_W = GLA_HEADS * GLA_DK
V_W = GLA_HEADS * GLA_DV
POOL_W = POOL_GROUP * len(POOL_WINDOWS)
LANES = 128
VMEM_LIMIT = 48 * 1024 * 1024

_NT = (((1,), (1,)), ((), ()))


def _dot(a, b):
    return jnp.dot(a, b, preferred_element_type=F32)


def _split_bf16(x):
    hi = x.astype(BF16)
    lo = (x - hi.astype(F32)).astype(BF16)
    return hi, lo


def _rms(x):
    return x * lax.rsqrt(jnp.mean(x * x, axis=-1, keepdims=True) + EPS)


def _silu(x):
    return x / (1.0 + jnp.exp(-x))


def _params(*sem):
    return pltpu.CompilerParams(dimension_semantics=sem, vmem_limit_bytes=VMEM_LIMIT)


def _ada_kernel(c_ref, w_ref, b_ref, o_ref):
    s = _silu(c_ref[...]).astype(BF16)
    o_ref[...] = _dot(s, w_ref[...].astype(BF16)) + b_ref[...]


def _ada_mod(c8, w_ada, b_ada):
    rows, d = c8.shape
    n = w_ada.shape[1]
    tn = 1024
    return pl.pallas_call(
        _ada_kernel,
        grid=(n // tn,),
        in_specs=[pl.BlockSpec((rows, d), lambda j: (0, 0)),
                  pl.BlockSpec((d, tn), lambda j: (0, j)),
                  pl.BlockSpec((1, tn), lambda j: (0, j))],
        out_specs=pl.BlockSpec((rows, tn), lambda j: (0, j)),
        out_shape=jax.ShapeDtypeStruct((rows, n), F32),
        compiler_params=_params("arbitrary"),
        name="ada_mod",
    )(c8, w_ada, b_ada.reshape(1, n))


def _inproj_kernel(x_ref, sh_ref, sc_ref, g1_ref, w_ref, wdec_ref, bdec_ref,
                   q_ref, k_ref, v_ref, g_ref, la_ref, p_ref):
    x = x_ref[0]
    h = _rms(x) * g1_ref[...]
    h = h * (1.0 + sc_ref[0]) + sh_ref[0]
    hb = h.astype(BF16)
    o = 0
    q_ref[0] = (_dot(hb, w_ref[:, o:o + QK_W]) * (GLA_DK ** -0.5)).astype(BF16)
    o += QK_W
    k_ref[0] = _dot(hb, w_ref[:, o:o + QK_W]).astype(BF16)
    o += QK_W
    v_ref[0] = _dot(hb, w_ref[:, o:o + V_W]).astype(BF16)
    o += V_W
    g_ref[0] = _dot(hb, w_ref[:, o:o + V_W]).astype(BF16)
    o += V_W
    p_ref[0] = _dot(hb, w_ref[:, o:o + POOL_W])
    o += POOL_W
    a_low = _dot(hb, w_ref[:, o:o + LANES])
    z = _dot(a_low.astype(BF16), wdec_ref[...]) + bdec_ref[...]
    log_sig = jnp.minimum(z, 0.0) - jnp.log(1.0 + jnp.exp(-jnp.abs(z)))
    la_ref[0] = log_sig / GATE_NORMALIZER


def _inproj(x, shift, scale, g1, w_r, wdec, bdec, tm):
    b, l, d = x.shape
    wcols = w_r.shape[1]
    tok = lambda bi, i: (bi, i, 0)
    vec = lambda bi, i: (bi, 0, 0)
    fixed = lambda bi, i: (0, 0)
    outs = [(QK_W, BF16), (QK_W, BF16), (V_W, BF16), (V_W, BF16), (2 * QK_W, F32), (POOL_W, F32)]
    return pl.pallas_call(
        _inproj_kernel,
        grid=(b, l // tm),
        in_specs=[pl.BlockSpec((1, tm, d), tok),
                  pl.BlockSpec((1, 1, d), vec),
                  pl.BlockSpec((1, 1, d), vec),
                  pl.BlockSpec((1, d), fixed),
                  pl.BlockSpec((d, wcols), fixed),
                  pl.BlockSpec(wdec.shape, fixed),
                  pl.BlockSpec(bdec.shape, fixed)],
        out_specs=[pl.BlockSpec((1, tm, w), tok) for w, _ in outs],
        out_shape=[jax.ShapeDtypeStruct((b, l, w), dt) for w, dt in outs],
        compiler_params=_params("arbitrary", "arbitrary"),
        name="inproj",
    )(x, shift, scale, g1, w_r, wdec, bdec)


GLA_GROUP = 256
CHUNKS_PER_GROUP = GLA_GROUP // CHUNK
CHUNK_LOG2 = 6
assert 1 << CHUNK_LOG2 == CHUNK


def _head_masks(rows):
    lane = lax.broadcasted_iota(jnp.int32, (rows, QK_W), 1)
    return [(lane >= h * GLA_DK) & (lane < (h + 1) * GLA_DK) for h in range(GLA_HEADS)]


def _gla_bulk(d, g, reverse, q_ref, k_ref, v_ref, la_ref, qd_scr, oi_scr, ds_scr, dc_scr):
    n = GLA_GROUP
    r = pl.multiple_of(g * n, n)
    i0 = lax.broadcasted_iota(jnp.int32, (n, n), 0)
    i1 = lax.broadcasted_iota(jnp.int32, (n, n), 1)
    causal = ((i0 >> CHUNK_LOG2) == (i1 >> CHUNK_LOG2)) & ((i1 >= i0) if reverse else (i1 <= i0))
    tri = jnp.where(causal, 1.0, 0.0).astype(BF16)
    la_hi, la_lo = _split_bf16(la_ref[0, pl.ds(r, n), :])
    b = _dot(tri, la_hi) + _dot(tri, la_lo)
    last_row = 0 if reverse else CHUNK - 1
    b_end = b.reshape(CHUNKS_PER_GROUP, CHUNK, QK_W)[:, last_row:last_row + 1, :]
    b_last = jnp.broadcast_to(b_end, (CHUNKS_PER_GROUP, CHUNK, QK_W)).reshape(n, QK_W)
    q = q_ref[0, pl.ds(r, n), :].astype(F32)
    k = k_ref[0, pl.ds(r, n), :].astype(F32)
    qd = (q * jnp.exp(b)).astype(BF16)
    kd = (k * jnp.exp(-b)).astype(BF16)
    k_st = k * jnp.exp(b_last - b)
    qd_scr[d, pl.ds(r, n), :] = qd
    v = v_ref[0, pl.ds(r, n), :]
    for h, m in enumerate(_head_masks(n)):
        scores = lax.dot_general(jnp.where(m, qd, jnp.zeros_like(qd)), kd, _NT, preferred_element_type=F32)
        probs = jnp.where(causal, scores, 0.0).astype(BF16)
        oi_scr[d, pl.ds(r, n), h * GLA_DV:(h + 1) * GLA_DV] = _dot(probs, v[:, h * GLA_DV:(h + 1) * GLA_DV])
    decay = jnp.exp(b_end)
    for c in range(CHUNKS_PER_GROUP):
        ci = g * CHUNKS_PER_GROUP + c
        k_st_t = k_st[c * CHUNK:(c + 1) * CHUNK].T.astype(BF16)
        for h in range(GLA_HEADS):
            ds_scr[d, ci, h * GLA_DK:(h + 1) * GLA_DK, :] = _dot(
                k_st_t[h * GLA_DK:(h + 1) * GLA_DK], v[c * CHUNK:(c + 1) * CHUNK, h * GLA_DV:(h + 1) * GLA_DV])
        dc_scr[d, ci] = jnp.broadcast_to(decay[c], (LANES, QK_W)).T


def _gla_kernel(qf_ref, kf_ref, vf_ref, laf_ref, qb_ref, kb_ref, vb_ref, lab_ref, s0f_ref, s0b_ref,
                of_ref, ob_ref, sf_out_ref, sb_out_ref, s_ref, qd_scr, oi_scr, ds_scr, dc_scr, st_scr, *, n_chunks):
    j = pl.program_id(1)

    @pl.when(j == 0)
    def _():
        s_ref[0] = s0f_ref[0]
        s_ref[1] = s0b_ref[0]

    def bulk(g, carry):
        _gla_bulk(0, g, False, qf_ref, kf_ref, vf_ref, laf_ref, qd_scr, oi_scr, ds_scr, dc_scr)
        _gla_bulk(1, g, True, qb_ref, kb_ref, vb_ref, lab_ref, qd_scr, oi_scr, ds_scr, dc_scr)
        return carry
    lax.fori_loop(0, n_chunks // CHUNKS_PER_GROUP, bulk, 0)

    def recur(i, carry):
        for d, ci in ((0, i), (1, n_chunks - 1 - i)):
            s = s_ref[d]
            st_scr[d, ci] = s.astype(BF16)
            s_ref[d] = dc_scr[d, ci] * s + ds_scr[d, ci]
        return carry
    lax.fori_loop(0, n_chunks, recur, 0)

    masks = _head_masks(CHUNK)

    def inter(g, carry):
        for d, o_ref in ((0, of_ref), (1, ob_ref)):
            for c in range(CHUNKS_PER_GROUP):
                ci = g * CHUNKS_PER_GROUP + c
                r = pl.multiple_of(ci * CHUNK, CHUNK)
                qd = qd_scr[d, pl.ds(r, CHUNK), :]
                lhs = jnp.concatenate([jnp.where(m, qd, jnp.zeros_like(qd)) for m in masks], axis=0)
                from_state = _dot(lhs, st_scr[d, ci])
                for h in range(GLA_HEADS):
                    cols = slice(h * GLA_DV, (h + 1) * GLA_DV)
                    o_ref[0, pl.ds(r, CHUNK), cols] = (oi_scr[d, pl.ds(r, CHUNK), cols]
                                                       + from_state[h * CHUNK:(h + 1) * CHUNK]).astype(BF16)
        return carry
    lax.fori_loop(0, n_chunks // CHUNKS_PER_GROUP, inter, 0)

    @pl.when(j == pl.num_programs(1) - 1)
    def _():
        sf_out_ref[0] = s_ref[0]
        sb_out_ref[0] = s_ref[1]


def _gla(q, k, v, la, s0f, s0b, tb):
    b, l, _ = q.shape
    assert tb % GLA_GROUP == 0 and l % tb == 0
    nb = l // tb
    n_chunks = tb // CHUNK
    fwd = lambda bi, j: (bi, j, 0)
    bwd = lambda bi, j: (bi, nb - 1 - j, 0)
    fwd_la = lambda bi, j: (bi, j, 0)
    bwd_la = lambda bi, j: (bi, nb - 1 - j, 1)
    st = lambda bi, j: (bi, 0, 0)
    state_shape = (b, QK_W, GLA_DV)
    return pl.pallas_call(
        functools.partial(_gla_kernel, n_chunks=tb // CHUNK),
        grid=(b, nb),
        in_specs=[pl.BlockSpec((1, tb, QK_W), fwd), pl.BlockSpec((1, tb, QK_W), fwd),
                  pl.BlockSpec((1, tb, V_W), fwd), pl.BlockSpec((1, tb, QK_W), fwd_la),
                  pl.BlockSpec((1, tb, QK_W), bwd), pl.BlockSpec((1, tb, QK_W), bwd),
                  pl.BlockSpec((1, tb, V_W), bwd), pl.BlockSpec((1, tb, QK_W), bwd_la),
                  pl.BlockSpec((1, QK_W, GLA_DV), st), pl.BlockSpec((1, QK_W, GLA_DV), st)],
        out_specs=[pl.BlockSpec((1, tb, V_W), fwd), pl.BlockSpec((1, tb, V_W), bwd),
                   pl.BlockSpec((1, QK_W, GLA_DV), st), pl.BlockSpec((1, QK_W, GLA_DV), st)],
        out_shape=[jax.ShapeDtypeStruct((b, l, V_W), BF16), jax.ShapeDtypeStruct((b, l, V_W), BF16),
                   jax.ShapeDtypeStruct(state_shape, F32), jax.ShapeDtypeStruct(state_shape, F32)],
        scratch_shapes=[pltpu.VMEM((2, QK_W, GLA_DV), F32),
                        pltpu.VMEM((2, tb, QK_W), BF16),
                        pltpu.VMEM((2, tb, V_W), F32),
                        pltpu.VMEM((2, n_chunks, QK_W, GLA_DV), F32),
                        pltpu.VMEM((2, n_chunks, QK_W, GLA_DV), F32),
                        pltpu.VMEM((2, n_chunks, QK_W, GLA_DV), BF16)],
        compiler_params=_params("arbitrary", "arbitrary"),
        name="gla",
    )(q, k, v, la, q, k, v, la, s0f, s0b)


POOL_BLK = 256
POOL_COL_STEP = 1024
POOL_ROW_STEP = 256
POOL_OUT_STEP = 1024


def _pool_body(w, first_batch, p_ref, wp_ref, ps_ref, o_ref, buf_ref, inv_ref, y_ref, n_tok):
    lo = w // 2
    hi = w - 1 - lo
    pad = lo * GRID_W
    n_rows = n_tok // GRID_W
    blk = POOL_BLK

    @pl.when(first_batch)
    def _():
        zeros = jnp.zeros((GRID_W * 8, LANES), F32)
        buf_ref[0:pad, :] = zeros[0:pad]
        buf_ref[pad + n_tok:pad + n_tok + hi * GRID_W + GRID_W, :] = zeros[0:hi * GRID_W + GRID_W]

        def inv_step(t, carry):
            r = pl.multiple_of(t * POOL_OUT_STEP, POOL_OUT_STEP)
            tok = r + lax.broadcasted_iota(jnp.int32, (POOL_OUT_STEP, LANES), 0)
            g_row = tok >> GRID_W_LOG2
            g_col = tok & (GRID_W - 1)
            cnt_c = jnp.minimum(g_col + hi, GRID_W - 1) - jnp.maximum(g_col - lo, 0) + 1
            cnt_r = jnp.minimum(g_row + hi, n_rows - 1) - jnp.maximum(g_row - lo, 0) + 1
            inv_ref[pl.ds(r, POOL_OUT_STEP), :] = 1.0 / (cnt_c * cnt_r).astype(F32)
            return carry
        lax.fori_loop(0, n_tok // POOL_OUT_STEP, inv_step, 0)

    i0 = lax.broadcasted_iota(jnp.int32, (blk, blk), 0)
    i1 = lax.broadcasted_iota(jnp.int32, (blk, blk), 1)
    same_row = (i0 >> GRID_W_LOG2) == (i1 >> GRID_W_LOG2)
    band = jnp.where(same_row & (i1 - i0 >= -lo) & (i1 - i0 <= hi), 1.0, 0.0).astype(BF16)

    def col_step(t, carry):
        for u in range(POOL_COL_STEP // blk):
            r = pl.multiple_of(t * POOL_COL_STEP + u * blk, blk)
            x_hi, x_lo = _split_bf16(p_ref[0, pl.ds(r, blk), :])
            s = _dot(band, jnp.concatenate([x_hi, x_lo], axis=-1))
            buf_ref[pl.ds(pad + r, blk), :] = s[:, :LANES] + s[:, LANES:]
        return carry
    lax.fori_loop(0, n_tok // POOL_COL_STEP, col_step, 0)

    def row_step(t, carry):
        r = pl.multiple_of(t * POOL_ROW_STEP, POOL_ROW_STEP)
        acc = buf_ref[pl.ds(r, POOL_ROW_STEP), :]
        for d in range(1, w):
            acc = acc + buf_ref[pl.ds(r + d * GRID_W, POOL_ROW_STEP), :]
        m = acc * inv_ref[pl.ds(r, POOL_ROW_STEP), :]
        y_ref[pl.ds(r, POOL_ROW_STEP), :] = (m - p_ref[0, pl.ds(r, POOL_ROW_STEP), :]).astype(BF16)
        return carry
    lax.fori_loop(0, n_tok // POOL_ROW_STEP, row_step, 0)

    wp = wp_ref[0].astype(BF16)
    scale = ps_ref[0]

    def out_step(t, carry):
        r = pl.multiple_of(t * POOL_OUT_STEP, POOL_OUT_STEP)
        o_ref[0, pl.ds(r, POOL_OUT_STEP), :] = (_dot(y_ref[pl.ds(r, POOL_OUT_STEP), :], wp) * scale).astype(BF16)
        return carry
    lax.fori_loop(0, n_tok // POOL_OUT_STEP, out_step, 0)


def _pool_kernel(p_ref, wp_ref, ps_ref, o_ref, buf_ref, inv_ref, y_ref, *, n_tok):
    g = pl.program_id(0)
    first_batch = pl.program_id(1) == 0
    for gi, w in enumerate(POOL_WINDOWS):
        @pl.when(g == gi)
        def _(w=w):
            _pool_body(w, first_batch, p_ref, wp_ref, ps_ref, o_ref, buf_ref, inv_ref, y_ref, n_tok)


def _pool(p, w_pool, pool_scale):
    b, l, _ = p.shape
    n_g = len(POOL_WINDOWS)
    max_w = max(POOL_WINDOWS)
    buf_rows = l + (max_w + 1) * GRID_W
    return pl.pallas_call(
        functools.partial(_pool_kernel, n_tok=l),
        grid=(n_g, b),
        in_specs=[pl.BlockSpec((1, l, POOL_GROUP), lambda g, bi: (bi, 0, g)),
                  pl.BlockSpec((1, POOL_GROUP, POOL_GROUP), lambda g, bi: (g, 0, 0)),
                  pl.BlockSpec((1, 1, POOL_GROUP), lambda g, bi: (g, 0, 0))],
        out_specs=pl.BlockSpec((1, l, POOL_GROUP), lambda g, bi: (bi, 0, g)),
        out_shape=jax.ShapeDtypeStruct((b, l, POOL_W), BF16),
        scratch_shapes=[pltpu.VMEM((buf_rows, LANES), F32), pltpu.VMEM((l, LANES), F32),
                        pltpu.VMEM((l, LANES), BF16)],
        compiler_params=_params("arbitrary", "arbitrary"),
        name="pool",
    )(p, w_pool, pool_scale.reshape(n_g, 1, POOL_GROUP))


ROUTER_ROWS = 48
ROUTER_GROUP_ROW0 = N_EXPERTS
TOP_K = 2
SORT_TM = 512
PIECE = 16
SLOT_ROWS = TOP_K * SORT_TM + N_EXPERTS * PIECE
META_W1_HI, META_W1_LO, META_W2_HI, META_W2_LO, META_E1, META_E2 = range(6)
INFO_DEST1, INFO_DEST2 = 6, 7


def _outproj_kernel(of_ref, ob_ref, g_ref, pooled_ref, x_ref, gn_ref, wout_ref, gt_ref, sh_ref, sc_ref, g2_ref,
                    wr_hi_ref, wr_lo_ref, x1_ref, xs_ref, info_ref, cnt_ref):
    o = of_ref[0].astype(F32) + ob_ref[0].astype(F32)
    gate = _silu(g_ref[0].astype(F32))
    parts = []
    for h in range(GLA_HEADS):
        sl = slice(h * GLA_DV, (h + 1) * GLA_DV)
        parts.append((_rms(o[:, sl]) * gn_ref[...] * gate[:, sl]).astype(BF16))
    o_n = jnp.concatenate(parts, axis=-1)
    out = _dot(o_n, wout_ref[0:V_W, :]) + _dot(pooled_ref[0], wout_ref[V_W:V_W + POOL_W, :])
    x1 = x_ref[0] + gt_ref[0] * out
    x1_ref[0] = x1
    h2 = _rms(x1) * g2_ref[...]
    h2 = h2 * (1.0 + sc_ref[0]) + sh_ref[0]
    h2_bf = h2.astype(BF16)
    h2_lo = (h2 - h2_bf.astype(F32)).astype(BF16)
    logits = (lax.dot_general(wr_hi_ref[...], h2_bf, _NT, preferred_element_type=F32)
              + lax.dot_general(wr_hi_ref[...], h2_lo, _NT, preferred_element_type=F32)
              + lax.dot_general(wr_lo_ref[...], h2_bf, _NT, preferred_element_type=F32))
    tm = logits.shape[1]
    row = lax.broadcasted_iota(jnp.int32, logits.shape, 0)
    neg = -jnp.inf
    big = jnp.int32(1 << 20)
    is_group = (row >= ROUTER_GROUP_ROW0) & (row < ROUTER_GROUP_ROW0 + N_GROUPS)
    gl = jnp.where(is_group, logits, neg)
    g_max = jnp.max(gl, axis=0, keepdims=True)
    g_idx = jnp.min(jnp.where(gl == g_max, row - ROUTER_GROUP_ROW0, big), axis=0, keepdims=True)
    g_prob = 1.0 / jnp.sum(jnp.where(is_group, jnp.exp(gl - g_max), 0.0), axis=0, keepdims=True)
    n_e = EXPERTS_PER_GROUP
    el = logits[0:n_e]
    for g in range(1, N_GROUPS):
        el = jnp.where(g_idx == g, logits[g * n_e:(g + 1) * n_e], el)
    r8 = lax.broadcasted_iota(jnp.int32, el.shape, 0)
    m1 = jnp.max(el, axis=0, keepdims=True)
    i1 = jnp.min(jnp.where(el == m1, r8, big), axis=0, keepdims=True)
    el2 = jnp.where(r8 == i1, neg, el)
    m2 = jnp.max(el2, axis=0, keepdims=True)
    i2 = jnp.min(jnp.where(el2 == m2, r8, big), axis=0, keepdims=True)
    r = jnp.exp(m2 - m1)
    w1 = g_prob / (1.0 + r)
    w2 = g_prob * r / (1.0 + r)

    e1 = g_idx * n_e + i1
    e2 = g_idx * n_e + i2
    r_e = lax.broadcasted_iota(jnp.int32, (N_EXPERTS, tm), 0)
    pick1 = r_e == e1
    pick2 = r_e == e2
    onehot = jnp.where(pick1 | pick2, 1.0, 0.0)
    t_r = lax.broadcasted_iota(jnp.int32, (tm, tm), 0)
    t_c = lax.broadcasted_iota(jnp.int32, (tm, tm), 1)
    rank = _dot(onehot.astype(BF16), jnp.where(t_r < t_c, 1.0, 0.0).astype(BF16))
    count = jnp.sum(onehot, axis=1, keepdims=True)
    n_pieces = jnp.floor((count + (PIECE - 1)) * (1.0 / PIECE))
    e_r = lax.broadcasted_iota(jnp.int32, (N_EXPERTS, N_EXPERTS), 0)
    e_c = lax.broadcasted_iota(jnp.int32, (N_EXPERTS, N_EXPERTS), 1)
    run_start = _dot(jnp.where(e_c < e_r, 1.0, 0.0).astype(BF16),
                     jnp.broadcast_to(n_pieces, (N_EXPERTS, LANES)).astype(BF16))[:, 0:1] * PIECE
    slot = rank + run_start
    dest1 = jnp.sum(jnp.where(pick1, slot, 0.0), axis=0, keepdims=True)
    dest2 = jnp.sum(jnp.where(pick2, slot, 0.0), axis=0, keepdims=True)
    w1_hi = w1.astype(BF16).astype(F32)
    w2_hi = w2.astype(BF16).astype(F32)
    info_rows = [w1_hi, (w1 - w1_hi).astype(BF16).astype(F32), w2_hi, (w2 - w2_hi).astype(BF16).astype(F32),
                 e1.astype(F32), e2.astype(F32), dest1, dest2]
    info_t = jnp.concatenate(info_rows + [jnp.zeros((LANES - len(info_rows), tm), F32)], axis=0)
    info = info_t.T
    info_ref[0] = info
    cnt_ref[0, 0] = jnp.broadcast_to(count, (N_EXPERTS, LANES)).astype(jnp.int32)
    s_id = lax.broadcasted_iota(jnp.int32, (SLOT_ROWS, tm), 0)
    perm_t = jnp.where((s_id == dest1.astype(jnp.int32)) | (s_id == dest2.astype(jnp.int32)),
                       1.0, 0.0).astype(BF16)
    lane = lax.broadcasted_iota(jnp.int32, info.shape, 1)
    meta = jnp.where(lane < INFO_DEST1, info, 0.0).astype(BF16)
    d = h2_bf.shape[1]
    xs_ref[0, 0, :, 0:d] = _dot(perm_t, h2_bf).astype(BF16)
    xs_ref[0, 0, :, d:d + LANES] = _dot(perm_t, meta).astype(BF16)


def _outproj(o_f, o_b, g, pooled, x, gn, w_out, gt_a, sh_m, sc_m, g2, wr_hi, wr_lo, tm):
    b, l, d = x.shape
    tok = lambda bi, i: (bi, i, 0)
    vec = lambda bi, i: (bi, 0, 0)
    fixed = lambda bi, i: (0, 0)
    return pl.pallas_call(
        _outproj_kernel,
        grid=(b, l // tm),
        in_specs=[pl.BlockSpec((1, tm, V_W), tok), pl.BlockSpec((1, tm, V_W), tok),
                  pl.BlockSpec((1, tm, V_W), tok), pl.BlockSpec((1, tm, POOL_W), tok),
                  pl.BlockSpec((1, tm, d), tok),
                  pl.BlockSpec((1, GLA_DV), fixed),
                  pl.BlockSpec(w_out.shape, fixed),
                  pl.BlockSpec((1, 1, d), vec), pl.BlockSpec((1, 1, d), vec), pl.BlockSpec((1, 1, d), vec),
                  pl.BlockSpec((1, d), fixed),
                  pl.BlockSpec(wr_hi.shape, fixed), pl.BlockSpec(wr_lo.shape, fixed)],
        out_specs=[pl.BlockSpec((1, tm, d), tok),
                   pl.BlockSpec((1, 1, SLOT_ROWS, d + LANES), lambda bi, i: (bi, i, 0, 0)),
                   pl.BlockSpec((1, tm, LANES), tok),
                   pl.BlockSpec((1, 1, N_EXPERTS, LANES), lambda bi, i: (bi, i, 0, 0))],
        out_shape=[jax.ShapeDtypeStruct((b, l, d), F32),
                   jax.ShapeDtypeStruct((b, l // tm, SLOT_ROWS, d + LANES), BF16),
                   jax.ShapeDtypeStruct((b, l, LANES), F32),
                   jax.ShapeDtypeStruct((b, l // tm, N_EXPERTS, LANES), jnp.int32)],
        compiler_params=_params("arbitrary", "arbitrary"),
        name="outproj_router",
    )(o_f, o_b, g, pooled, x, gn, w_out, gt_a, sh_m, sc_m, g2, wr_hi, wr_lo)


EXPERT_TM = 512
PIECES_PER_STEP = EXPERT_TM // PIECE
GATHER_SLOTS = 6
SLOT_PIECES = SLOT_ROWS // PIECE


def _expert_steps(n_tok):
    pieces = TOP_K * n_tok // PIECE + (n_tok // SORT_TM) * N_EXPERTS
    return -(-pieces // PIECES_PER_STEP) + N_EXPERTS + 1


def _experts_kernel(exp_ref, valid_ref, src_ref, xs_hbm, wi_ref, wo_ref, ys_ref, xbuf, wi_bf, wo_bf, sem):
    i = pl.program_id(0)
    n_steps = pl.num_programs(0)
    slot = i % GATHER_SLOTS
    d = ys_ref.shape[1]

    def piece_copy(s, j, src_row):
        return pltpu.make_async_copy(xs_hbm.at[pl.ds(src_row, PIECE), :], xbuf.at[s, pl.ds(j * PIECE, PIECE), :],
                                     sem.at[s])

    def fetch(step):
        @pl.when((step < n_steps) & (valid_ref[jnp.minimum(step, n_steps - 1)] == 1))
        def _():
            for j in range(PIECES_PER_STEP):
                piece_copy(step % GATHER_SLOTS, j,
                           pl.multiple_of(src_ref[step * PIECES_PER_STEP + j], PIECE)).start(priority=j % 2)

    @pl.when(i == 0)
    def _():
        for ahead in range(GATHER_SLOTS - 1):
            fetch(jnp.int32(ahead))

    fetch(i + GATHER_SLOTS - 1)
    expert = exp_ref[i]

    @pl.when((i == 0) | (exp_ref[jnp.maximum(i - 1, 0)] != expert))
    def _():
        wi_bf[...] = wi_ref[0].astype(BF16)
        wo_bf[...] = wo_ref[0].astype(BF16)

    @pl.when(valid_ref[i] == 1)
    def _():
        for j in range(PIECES_PER_STEP):
            piece_copy(slot, j, 0).wait()
        x = xbuf[slot, :, 0:d]
        meta = xbuf[slot, :, d:d + LANES].astype(F32)
        lane = lax.broadcasted_iota(jnp.int32, meta.shape, 1)

        def pick(*lanes):
            sel = lane == lanes[0]
            for ln in lanes[1:]:
                sel = sel | (lane == ln)
            return jnp.sum(jnp.where(sel, meta, 0.0), axis=-1, keepdims=True)

        is_first = pick(META_E1) == expert.astype(F32)
        gate = jnp.where(is_first, pick(META_W1_HI, META_W1_LO), pick(META_W2_HI, META_W2_LO))
        au = _dot(x, wi_bf[...])
        hidden = (_silu(au[:, :D_EXPERT]) * au[:, D_EXPERT:] * gate).astype(BF16)
        ys_ref[...] = _dot(hidden, wo_bf[...]).astype(BF16)

    @pl.when(valid_ref[i] == 0)
    def _():
        ys_ref[...] = jnp.zeros_like(ys_ref)


def _experts(xs, step_expert, step_valid, piece_src, wi, wo):
    n_steps = step_expert.shape[0]
    width = xs.shape[1]
    d = wo.shape[2]
    return pl.pallas_call(
        _experts_kernel,
        grid_spec=pltpu.PrefetchScalarGridSpec(
            num_scalar_prefetch=3,
            grid=(n_steps,),
            in_specs=[pl.BlockSpec(memory_space=pl.ANY),
                      pl.BlockSpec((1,) + wi.shape[1:], lambda i, e, v, s: (e[i], 0, 0)),
                      pl.BlockSpec((1,) + wo.shape[1:], lambda i, e, v, s: (e[i], 0, 0))],
            out_specs=pl.BlockSpec((EXPERT_TM, d), lambda i, e, v, s: (i, 0)),
            scratch_shapes=[pltpu.VMEM((GATHER_SLOTS, EXPERT_TM, width), BF16), pltpu.VMEM(wi.shape[1:], BF16),
                            pltpu.VMEM(wo.shape[1:], BF16), pltpu.SemaphoreType.DMA((GATHER_SLOTS,))]),
        out_shape=jax.ShapeDtypeStruct((n_steps * EXPERT_TM, d), BF16),
        compiler_params=_params("arbitrary"),
        name="moe_experts",
    )(step_expert, step_valid, piece_src, xs, wi, wo)


def _combine_kernel(src_ref, ys_hbm, info_ref, x1_ref, gt_ref, gf_ref, o_ref, stage, sem):
    i = pl.program_id(0)
    n_steps = pl.num_programs(0)
    tm = SORT_TM

    def piece_copy(s, src_row, dst_row):
        return pltpu.make_async_copy(ys_hbm.at[pl.ds(src_row, PIECE), :], stage.at[s, pl.ds(dst_row, PIECE), :],
                                     sem.at[s])

    def fetch(tile, s):
        for m in range(SLOT_PIECES):
            piece_copy(s, pl.multiple_of(src_ref[tile * SLOT_PIECES + m], PIECE), m * PIECE).start(priority=m % 2)

    def wait(s):
        for m in range(SLOT_PIECES):
            piece_copy(s, 0, m * PIECE).wait()

    @pl.when(i == 0)
    def _():
        fetch(0, 0)

    slot = i % 2
    fetch(jnp.minimum(i + 1, n_steps - 1), 1 - slot)
    wait(slot)
    ys = stage[slot]
    info = info_ref[...]
    lane = lax.broadcasted_iota(jnp.int32, info.shape, 1)
    dest1 = jnp.sum(jnp.where(lane == INFO_DEST1, info, 0.0), axis=-1, keepdims=True).astype(jnp.int32)
    dest2 = jnp.sum(jnp.where(lane == INFO_DEST2, info, 0.0), axis=-1, keepdims=True).astype(jnp.int32)
    slot_id = lax.broadcasted_iota(jnp.int32, (tm, SLOT_ROWS), 1)
    perm = jnp.where((slot_id == dest1) | (slot_id == dest2), 1.0, 0.0).astype(BF16)
    y = _dot(perm, ys)
    x2 = x1_ref[...] + gt_ref[0] * y
    o_ref[...] = _rms(x2) * gf_ref[...]

    @pl.when(i == n_steps - 1)
    def _():
        wait(1 - slot)


def _combine(ys, piece_src, info, x1, gt_m, gf, tiles_per_batch):
    t, d = x1.shape
    n_tiles = t // SORT_TM
    tok = lambda i, *_: (i, 0)
    return pl.pallas_call(
        _combine_kernel,
        grid_spec=pltpu.PrefetchScalarGridSpec(
            num_scalar_prefetch=1,
            grid=(n_tiles,),
            in_specs=[pl.BlockSpec(memory_space=pl.ANY),
                      pl.BlockSpec((SORT_TM, LANES), tok),
                      pl.BlockSpec((SORT_TM, d), tok),
                      pl.BlockSpec((1, 1, d), lambda i, *_: (i // tiles_per_batch, 0, 0)),
                      pl.BlockSpec((1, d), lambda i, *_: (0, 0))],
            out_specs=pl.BlockSpec((SORT_TM, d), tok),
            scratch_shapes=[pltpu.VMEM((2, SLOT_ROWS, d), BF16), pltpu.SemaphoreType.DMA((2,))]),
        out_shape=jax.ShapeDtypeStruct((t, d), F32),
        compiler_params=_params("arbitrary"),
        name="moe_combine",
    )(piece_src, ys, info, x1, gt_m, gf)


def _sparse_moe(xs, info, counts, x1, wi, wo, gt_m, gf):
    b, l, d = x1.shape
    t = b * l
    n_tiles = t // SORT_TM
    i32 = jnp.int32

    cnt = counts[:, :, :, 0].reshape(n_tiles, N_EXPERTS)
    run_pieces = (cnt + (PIECE - 1)) // PIECE
    run_slot = (jnp.cumsum(run_pieces, axis=1) - run_pieces) * PIECE
    before = jnp.cumsum(run_pieces, axis=0) - run_pieces
    expert_pieces = jnp.sum(run_pieces, axis=0)
    steps_e = (expert_pieces + PIECES_PER_STEP - 1) // PIECES_PER_STEP
    step_end = jnp.cumsum(steps_e)
    region = (step_end - steps_e) * PIECES_PER_STEP
    run_pos = (region[None, :] + before) * PIECE

    n_steps = _expert_steps(t)
    step = jnp.arange(n_steps, dtype=i32)
    step_expert = jnp.minimum(jnp.sum((step[:, None] >= step_end[None, :]).astype(i32), axis=1), N_EXPERTS - 1)
    step_valid = (step < step_end[-1]).astype(i32)
    lookup = lambda onehot, table: jnp.dot(onehot, table.astype(F32), precision=lax.Precision.HIGHEST)
    piece = jnp.arange(n_steps * PIECES_PER_STEP, dtype=i32)
    is_expert = (jnp.repeat(step_expert, PIECES_PER_STEP)[:, None] == jnp.arange(N_EXPERTS, dtype=i32)).astype(F32)
    k_local = piece.astype(F32) - lookup(is_expert, region)
    ends = lookup(is_expert, jnp.cumsum(run_pieces, axis=0).T)
    tile_of = jnp.minimum(jnp.sum((ends <= k_local[:, None]).astype(i32), axis=1), n_tiles - 1)
    is_tile = (tile_of[:, None] == jnp.arange(n_tiles, dtype=i32)).astype(F32)
    j_in_run = k_local - jnp.sum(is_tile * lookup(is_expert, before.T), axis=1)
    slot_start = jnp.sum(is_tile * lookup(is_expert, run_slot.T), axis=1)
    real = (k_local < lookup(is_expert, expert_pieces)) & (jnp.repeat(step_valid, PIECES_PER_STEP) == 1)
    zero_piece = SLOT_ROWS - PIECE
    piece_src = jnp.where(real, tile_of * SLOT_ROWS + (slot_start + j_in_run * PIECE).astype(i32), zero_piece)


    m = jnp.arange(SLOT_PIECES, dtype=i32)
    run_end = jnp.cumsum(run_pieces, axis=1)
    run_of = jnp.minimum(jnp.sum((run_end[:, None, :] <= m[None, :, None]).astype(i32), axis=-1), N_EXPERTS - 1)
    is_run = run_of[:, :, None] == jnp.arange(N_EXPERTS, dtype=i32)
    pos_m = jnp.sum(jnp.where(is_run, run_pos[:, None, :], 0), axis=-1)
    first_m = jnp.sum(jnp.where(is_run, (run_end - run_pieces)[:, None, :], 0), axis=-1)
    zero_row = (n_steps - 1) * EXPERT_TM
    back_src = jnp.where(m[None, :] < run_end[:, -1:], pos_m + (m[None, :] - first_m) * PIECE, zero_row)

    flat = lambda a: a.reshape(-1).astype(i32)
    ys = _experts(xs.reshape(n_tiles * SLOT_ROWS, d + LANES), step_expert, step_valid, flat(piece_src), wi, wo)
    out = _combine(ys, flat(back_src), info.reshape(t, LANES), x1.reshape(t, d), gt_m, gf, l // SORT_TM)
    return out.reshape(b, l, d)


def kernel(x, c, ctx, c_ctx, w_ada, b_ada, norm1_g, w_in, w_decay, b_decay, gla_norm_g, w_pool, pool_scale, w_out,
           norm2_g, w_router_group, w_router_expert, w_expert_in, w_expert_out, final_norm_g):
    assert w_ada.shape[0] == 1, "single-layer trunk"
    b, l, d = x.shape
    off_a = 2 * QK_W + 2 * V_W
    a_cols = 2 * GATE_RANK
    off_p = off_a + a_cols

    c8 = jnp.zeros((8, d), F32).at[:b].set(c).at[b].set(c_ctx)
    mod = _ada_mod(c8, w_ada[0], b_ada[0])
    sh_a, sc_a, gt_a, sh_m, sc_m, gt_m = [m.reshape(8, 1, d) for m in jnp.split(mod, 6, axis=-1)]
    ctx_rows = lambda m: jnp.broadcast_to(m[b:b + 1], (b, 1, d))

    w = w_in[0]
    w_r = jnp.concatenate([w[:, :off_a], w[:, off_p:off_p + POOL_W], w[:, off_a:off_p],
                           jnp.zeros((d, LANES - a_cols), F32)], axis=1).astype(BF16)
    wdec = jnp.zeros((LANES, 2 * QK_W), F32)
    wdec = wdec.at[:GATE_RANK, :QK_W].set(w_decay[0, 0]).at[GATE_RANK:a_cols, QK_W:].set(w_decay[0, 1]).astype(BF16)
    bdec = b_decay[0].reshape(1, 2 * QK_W)
    g1 = norm1_g[0].reshape(1, d)

    q_c, k_c, v_c, _, la_c, _ = _inproj(ctx, ctx_rows(sh_a), ctx_rows(sc_a), g1, w_r, wdec, bdec, tm=ctx.shape[1])
    zero_s = jnp.zeros((b, QK_W, GLA_DV), F32)
    _, _, s_f, s_b = _gla(q_c, k_c, v_c, la_c, zero_s, zero_s, tb=ctx.shape[1])

    q, k, v, g, la, p = _inproj(x, sh_a[:b], sc_a[:b], g1, w_r, wdec, bdec, tm=1024)
    o_f, o_b, _, _ = _gla(q, k, v, la, s_f, s_b, tb=1024)
    pooled = _pool(p, w_pool[0], pool_scale[0])

    w_router = jnp.zeros((ROUTER_ROWS, d), F32)
    w_router = w_router.at[:N_EXPERTS].set(w_router_expert[0].T)
    w_router = w_router.at[ROUTER_GROUP_ROW0:ROUTER_GROUP_ROW0 + N_GROUPS].set(w_router_group[0].T)
    wr_hi = w_router.astype(BF16)
    wr_lo = (w_router - wr_hi.astype(F32)).astype(BF16)
    x1, xs, info, counts = _outproj(o_f, o_b, g, pooled, x, gla_norm_g[0].reshape(1, GLA_DV), w_out[0].astype(BF16),
                                    gt_a[:b], sh_m[:b], sc_m[:b], norm2_g[0].reshape(1, d), wr_hi, wr_lo,
                                    tm=SORT_TM)

    return _sparse_moe(xs, info, counts, x1, w_expert_in[0], w_expert_out[0], gt_m[:b], final_norm_g.reshape(1, d))
```

```python
import functools

import jax
import jax.numpy as jnp
from jax import lax
from jax.experimental import pallas as pl
from jax.experimental.pallas import tpu as pltpu

F32 = jnp.float32
BF16 = jnp.bfloat16

GRID_W = 64
GRID_W_LOG2 = 6
assert 1 << GRID_W_LOG2 == GRID_W
GLA_HEADS = 4
GLA_DK = 64
GLA_DV = 128
GATE_RANK = 16
GATE_NORMALIZER = 16.0
CHUNK = 64
POOL_WINDOWS = (2, 4, 8, 16)
POOL_GROUP = 128
N_GROUPS = 4
EXPERTS_PER_GROUP = 8
N_EXPERTS = N_GROUPS * EXPERTS_PER_GROUP
D_EXPERT = 256
EPS = 1e-6

QK_W = GLA_HEADS * GLA_DK
V_W = GLA_HEADS * GLA_DV
POOL_W = POOL_GROUP * len(POOL_WINDOWS)
LANES = 128
VMEM_LIMIT = 48 * 1024 * 1024

_NT = (((1,), (1,)), ((), ()))


def _dot(a, b):
    return jnp.dot(a, b, preferred_element_type=F32)


def _split_bf16(x):
    hi = x.astype(BF16)
    lo = (x - hi.astype(F32)).astype(BF16)
    return hi, lo


def _rms(x):
    return x * lax.rsqrt(jnp.mean(x * x, axis=-1, keepdims=True) + EPS)


def _silu(x):
    return x / (1.0 + jnp.exp(-x))


def _params(*sem):
    return pltpu.CompilerParams(dimension_semantics=sem, vmem_limit_bytes=VMEM_LIMIT)


def _ada_kernel(c_ref, w_ref, b_ref, o_ref):
    s = _silu(c_ref[...]).astype(BF16)
    o_ref[...] = _dot(s, w_ref[...].astype(BF16)) + b_ref[...]


def _ada_mod(c8, w_ada, b_ada):
    rows, d = c8.shape
    n = w_ada.shape[1]
    tn = 1024
    return pl.pallas_call(
        _ada_kernel,
        grid=(n // tn,),
        in_specs=[pl.BlockSpec((rows, d), lambda j: (0, 0)),
                  pl.BlockSpec((d, tn), lambda j: (0, j)),
                  pl.BlockSpec((1, tn), lambda j: (0, j))],
        out_specs=pl.BlockSpec((rows, tn), lambda j: (0, j)),
        out_shape=jax.ShapeDtypeStruct((rows, n), F32),
        compiler_params=_params("arbitrary"),
        name="ada_mod",
    )(c8, w_ada, b_ada.reshape(1, n))


def _inproj_kernel(x_ref, sh_ref, sc_ref, g1_ref, w_ref, wdec_ref, bdec_ref,
                   q_ref, k_ref, v_ref, g_ref, la_ref, p_ref):
    x = x_ref[0]
    h = _rms(x) * g1_ref[...]
    h = h * (1.0 + sc_ref[0]) + sh_ref[0]
    hb = h.astype(BF16)
    o = 0
    q_ref[0] = (_dot(hb, w_ref[:, o:o + QK_W]) * (GLA_DK ** -0.5)).astype(BF16)
    o += QK_W
    k_ref[0] = _dot(hb, w_ref[:, o:o + QK_W]).astype(BF16)
    o += QK_W
    v_ref[0] = _dot(hb, w_ref[:, o:o + V_W]).astype(BF16)
    o += V_W
    g_ref[0] = _dot(hb, w_ref[:, o:o + V_W]).astype(BF16)
    o += V_W
    p_ref[0] = _dot(hb, w_ref[:, o:o + POOL_W])
    o += POOL_W
    a_low = _dot(hb, w_ref[:, o:o + LANES])
    z = _dot(a_low.astype(BF16), wdec_ref[...]) + bdec_ref[...]
    log_sig = jnp.minimum(z, 0.0) - jnp.log(1.0 + jnp.exp(-jnp.abs(z)))
    la_ref[0] = log_sig / GATE_NORMALIZER


def _inproj(x, shift, scale, g1, w_r, wdec, bdec, tm):
    b, l, d = x.shape
    wcols = w_r.shape[1]
    tok = lambda bi, i: (bi, i, 0)
    vec = lambda bi, i: (bi, 0, 0)
    fixed = lambda bi, i: (0, 0)
    outs = [(QK_W, BF16), (QK_W, BF16), (V_W, BF16), (V_W, BF16), (2 * QK_W, F32), (POOL_W, F32)]
    return pl.pallas_call(
        _inproj_kernel,
        grid=(b, l // tm),
        in_specs=[pl.BlockSpec((1, tm, d), tok),
                  pl.BlockSpec((1, 1, d), vec),
                  pl.BlockSpec((1, 1, d), vec),
                  pl.BlockSpec((1, d), fixed),
                  pl.BlockSpec((d, wcols), fixed),
                  pl.BlockSpec(wdec.shape, fixed),
                  pl.BlockSpec(bdec.shape, fixed)],
        out_specs=[pl.BlockSpec((1, tm, w), tok) for w, _ in outs],
        out_shape=[jax.ShapeDtypeStruct((b, l, w), dt) for w, dt in outs],
        compiler_params=_params("arbitrary", "arbitrary"),
        name="inproj",
    )(x, shift, scale, g1, w_r, wdec, bdec)


GLA_GROUP = 256
CHUNKS_PER_GROUP = GLA_GROUP // CHUNK
CHUNK_LOG2 = 6
assert 1 << CHUNK_LOG2 == CHUNK


def _head_masks(rows):
    lane = lax.broadcasted_iota(jnp.int32, (rows, QK_W), 1)
    return [(lane >= h * GLA_DK) & (lane < (h + 1) * GLA_DK) for h in range(GLA_HEADS)]


def _gla_bulk(items, qd_scr, oi_scr, ds_scr, dc_scr):
    n = GLA_GROUP
    i0 = lax.broadcasted_iota(jnp.int32, (n, n), 0)
    i1 = lax.broadcasted_iota(jnp.int32, (n, n), 1)
    same_chunk = (i0 >> CHUNK_LOG2) == (i1 >> CHUNK_LOG2)
    causal = {False: same_chunk & (i1 <= i0), True: same_chunk & (i1 >= i0)}
    tri = {rev: jnp.where(m, 1.0, 0.0).astype(BF16) for rev, m in causal.items()}
    masks = _head_masks(n)
    rows = [pl.multiple_of(g * n, n) for _, _, g, *_ in items]

    cum = []
    for (d, rev, g, q_ref, k_ref, v_ref, la_ref), r in zip(items, rows):
        la_hi, la_lo = _split_bf16(la_ref[0, pl.ds(r, n), :])
        cum.append(_dot(tri[rev], la_hi) + _dot(tri[rev], la_lo))

    decayed = []
    for (d, rev, g, q_ref, k_ref, v_ref, la_ref), r, b in zip(items, rows, cum):
        last_row = 0 if rev else CHUNK - 1
        b_end = b.reshape(CHUNKS_PER_GROUP, CHUNK, QK_W)[:, last_row:last_row + 1, :]
        b_last = jnp.broadcast_to(b_end, (CHUNKS_PER_GROUP, CHUNK, QK_W)).reshape(n, QK_W)
        k = k_ref[0, pl.ds(r, n), :].astype(F32)
        qd = (q_ref[0, pl.ds(r, n), :].astype(F32) * jnp.exp(b)).astype(BF16)
        qd_scr[d, pl.ds(r, n), :] = qd
        decayed.append((qd, (k * jnp.exp(-b)).astype(BF16), k * jnp.exp(b_last - b), jnp.exp(b_end)))

    scores = [[lax.dot_general(jnp.where(m, qd, jnp.zeros_like(qd)), kd, _NT, preferred_element_type=F32)
               for m in masks] for qd, kd, _, _ in decayed]
    for (d, rev, g, q_ref, k_ref, v_ref, la_ref), r, sc in zip(items, rows, scores):
        for h in range(GLA_HEADS):
            cols = slice(h * GLA_DV, (h + 1) * GLA_DV)
            probs = jnp.where(causal[rev], sc[h], 0.0).astype(BF16)
            oi_scr[d, pl.ds(r, n), cols] = _dot(probs, v_ref[0, pl.ds(r, n), cols])

    for (d, rev, g, q_ref, k_ref, v_ref, la_ref), r, (_, _, k_st, decay) in zip(items, rows, decayed):
        for c in range(CHUNKS_PER_GROUP):
            ci = g * CHUNKS_PER_GROUP + c
            k_st_t = k_st[c * CHUNK:(c + 1) * CHUNK].T.astype(BF16)
            for h in range(GLA_HEADS):
                v_ch = v_ref[0, pl.ds(r + c * CHUNK, CHUNK), h * GLA_DV:(h + 1) * GLA_DV]
                ds_scr[d, ci, h * GLA_DK:(h + 1) * GLA_DK, :] = _dot(k_st_t[h * GLA_DK:(h + 1) * GLA_DK], v_ch)
            dc_scr[d, ci] = jnp.broadcast_to(decay[c], (LANES, QK_W)).T


def _gla_kernel(qf_ref, kf_ref, vf_ref, laf_ref, qb_ref, kb_ref, vb_ref, lab_ref, s0f_ref, s0b_ref,
                of_ref, ob_ref, sf_out_ref, sb_out_ref, s_ref, qd_scr, oi_scr, ds_scr, dc_scr, st_scr, *, n_chunks):
    j = pl.program_id(1)

    @pl.when(j == 0)
    def _():
        s_ref[0] = s0f_ref[0]
        s_ref[1] = s0b_ref[0]

    n_groups = n_chunks // CHUNKS_PER_GROUP
    unroll = 2 if n_groups % 2 == 0 else 1

    def bulk(it, carry):
        items = []
        for u in range(unroll):
            g = it * unroll + u
            items.append((0, False, g, qf_ref, kf_ref, vf_ref, laf_ref))
            items.append((1, True, g, qb_ref, kb_ref, vb_ref, lab_ref))
        _gla_bulk(items, qd_scr, oi_scr, ds_scr, dc_scr)
        return carry
    lax.fori_loop(0, n_groups // unroll, bulk, 0)

    def recur(i, carry):
        for d, ci in ((0, i), (1, n_chunks - 1 - i)):
            s = s_ref[d]
            st_scr[d, ci] = s.astype(BF16)
            s_ref[d] = dc_scr[d, ci] * s + ds_scr[d, ci]
        return carry
    lax.fori_loop(0, n_chunks, recur, 0)

    masks = _head_masks(CHUNK)

    def inter(it, carry):
        work = [(d, o_ref, (it * unroll + u) * CHUNKS_PER_GROUP + c)
                for u in range(unroll) for d, o_ref in ((0, of_ref), (1, ob_ref)) for c in range(CHUNKS_PER_GROUP)]
        from_state = []
        for d, o_ref, ci in work:
            qd = qd_scr[d, pl.ds(pl.multiple_of(ci * CHUNK, CHUNK), CHUNK), :]
            lhs = jnp.concatenate([jnp.where(m, qd, jnp.zeros_like(qd)) for m in masks], axis=0)
            from_state.append(_dot(lhs, st_scr[d, ci]))
        for (d, o_ref, ci), fs in zip(work, from_state):
            r = pl.multiple_of(ci * CHUNK, CHUNK)
            for h in range(GLA_HEADS):
                cols = slice(h * GLA_DV, (h + 1) * GLA_DV)
                o_ref[0, pl.ds(r, CHUNK), cols] = (oi_scr[d, pl.ds(r, CHUNK), cols]
                                                   + fs[h * CHUNK:(h + 1) * CHUNK]).astype(BF16)
        return carry
    lax.fori_loop(0, n_groups // unroll, inter, 0)

    @pl.when(j == pl.num_programs(1) - 1)
    def _():
        sf_out_ref[0] = s_ref[0]
        sb_out_ref[0] = s_ref[1]


def _gla(q, k, v, la, s0f, s0b, tb):
    b, l, _ = q.shape
    assert tb % GLA_GROUP == 0 and l % tb == 0
    nb = l // tb
    n_chunks = tb // CHUNK
    fwd = lambda bi, j: (bi, j, 0)
    bwd = lambda bi, j: (bi, nb - 1 - j, 0)
    fwd_la = lambda bi, j: (bi, j, 0)
    bwd_la = lambda bi, j: (bi, nb - 1 - j, 1)
    st = lambda bi, j: (bi, 0, 0)
    state_shape = (b, QK_W, GLA_DV)
    return pl.pallas_call(
        functools.partial(_gla_kernel, n_chunks=tb // CHUNK),
        grid=(b, nb),
        in_specs=[pl.BlockSpec((1, tb, QK_W), fwd), pl.BlockSpec((1, tb, QK_W), fwd),
                  pl.BlockSpec((1, tb, V_W), fwd), pl.BlockSpec((1, tb, QK_W), fwd_la),
                  pl.BlockSpec((1, tb, QK_W), bwd), pl.BlockSpec((1, tb, QK_W), bwd),
                  pl.BlockSpec((1, tb, V_W), bwd), pl.BlockSpec((1, tb, QK_W), bwd_la),
                  pl.BlockSpec((1, QK_W, GLA_DV), st), pl.BlockSpec((1, QK_W, GLA_DV), st)],
        out_specs=[pl.BlockSpec((1, tb, V_W), fwd), pl.BlockSpec((1, tb, V_W), bwd),
                   pl.BlockSpec((1, QK_W, GLA_DV), st), pl.BlockSpec((1, QK_W, GLA_DV), st)],
        out_shape=[jax.ShapeDtypeStruct((b, l, V_W), BF16), jax.ShapeDtypeStruct((b, l, V_W), BF16),
                   jax.ShapeDtypeStruct(state_shape, F32), jax.ShapeDtypeStruct(state_shape, F32)],
        scratch_shapes=[pltpu.VMEM((2, QK_W, GLA_DV), F32),
                        pltpu.VMEM((2, tb, QK_W), BF16),
                        pltpu.VMEM((2, tb, V_W), F32),
                        pltpu.VMEM((2, n_chunks, QK_W, GLA_DV), F32),
                        pltpu.VMEM((2, n_chunks, QK_W, GLA_DV), F32),
                        pltpu.VMEM((2, n_chunks, QK_W, GLA_DV), BF16)],
        compiler_params=_params("arbitrary", "arbitrary"),
        name="gla",
    )(q, k, v, la, q, k, v, la, s0f, s0b)


POOL_BLK = 256
POOL_COL_STEP = 1024
POOL_ROW_STEP = 256
POOL_OUT_STEP = 1024


def _pool_body(w, first_batch, p_ref, wp_ref, ps_ref, o_ref, buf_ref, inv_ref, y_ref, n_tok):
    lo = w // 2
    hi = w - 1 - lo
    pad = lo * GRID_W
    n_rows = n_tok // GRID_W
    blk = POOL_BLK

    @pl.when(first_batch)
    def _():
        zeros = jnp.zeros((GRID_W * 8, LANES), F32)
        buf_ref[0:pad, :] = zeros[0:pad]
        buf_ref[pad + n_tok:pad + n_tok + hi * GRID_W + GRID_W, :] = zeros[0:hi * GRID_W + GRID_W]

        def inv_step(t, carry):
            r = pl.multiple_of(t * POOL_OUT_STEP, POOL_OUT_STEP)
            tok = r + lax.broadcasted_iota(jnp.int32, (POOL_OUT_STEP, LANES), 0)
            g_row = tok >> GRID_W_LOG2
            g_col = tok & (GRID_W - 1)
            cnt_c = jnp.minimum(g_col + hi, GRID_W - 1) - jnp.maximum(g_col - lo, 0) + 1
            cnt_r = jnp.minimum(g_row + hi, n_rows - 1) - jnp.maximum(g_row - lo, 0) + 1
            inv_ref[pl.ds(r, POOL_OUT_STEP), :] = 1.0 / (cnt_c * cnt_r).astype(F32)
            return carry
        lax.fori_loop(0, n_tok // POOL_OUT_STEP, inv_step, 0)

    i0 = lax.broadcasted_iota(jnp.int32, (blk, blk), 0)
    i1 = lax.broadcasted_iota(jnp.int32, (blk, blk), 1)
    same_row = (i0 >> GRID_W_LOG2) == (i1 >> GRID_W_LOG2)
    band = jnp.where(same_row & (i1 - i0 >= -lo) & (i1 - i0 <= hi), 1.0, 0.0).astype(BF16)

    def col_step(t, carry):
        rows = [pl.multiple_of(t * POOL_COL_STEP + u * blk, blk) for u in range(POOL_COL_STEP // blk)]
        sums = []
        for r in rows:
            sums.append(_dot(band, jnp.concatenate(_split_bf16(p_ref[0, pl.ds(r, blk), :]), axis=-1)))
        for r, s in zip(rows, sums):
            buf_ref[pl.ds(pad + r, blk), :] = s[:, :LANES] + s[:, LANES:]
        return carry
    lax.fori_loop(0, n_tok // POOL_COL_STEP, col_step, 0)

    def row_step(t, carry):
        r = pl.multiple_of(t * POOL_ROW_STEP, POOL_ROW_STEP)
        acc = buf_ref[pl.ds(r, POOL_ROW_STEP), :]
        for d in range(1, w):
            acc = acc + buf_ref[pl.ds(r + d * GRID_W, POOL_ROW_STEP), :]
        m = acc * inv_ref[pl.ds(r, POOL_ROW_STEP), :]
        y_ref[pl.ds(r, POOL_ROW_STEP), :] = (m - p_ref[0, pl.ds(r, POOL_ROW_STEP), :]).astype(BF16)
        return carry
    lax.fori_loop(0, n_tok // POOL_ROW_STEP, row_step, 0)

    wp = wp_ref[0].astype(BF16)
    scale = ps_ref[0]

    def out_step(t, carry):
        r = pl.multiple_of(t * POOL_OUT_STEP, POOL_OUT_STEP)
        o_ref[0, pl.ds(r, POOL_OUT_STEP), :] = (_dot(y_ref[pl.ds(r, POOL_OUT_STEP), :], wp) * scale).astype(BF16)
        return carry
    lax.fori_loop(0, n_tok // POOL_OUT_STEP, out_step, 0)


def _pool_kernel(p_ref, wp_ref, ps_ref, o_ref, buf_ref, inv_ref, y_ref, *, n_tok):
    g = pl.program_id(0)
    first_batch = pl.program_id(1) == 0
    for gi, w in enumerate(POOL_WINDOWS):
        @pl.when(g == gi)
        def _(w=w):
            _pool_body(w, first_batch, p_ref, wp_ref, ps_ref, o_ref, buf_ref, inv_ref, y_ref, n_tok)


def _pool(p, w_pool, pool_scale):
    b, l, _ = p.shape
    n_g = len(POOL_WINDOWS)
    max_w = max(POOL_WINDOWS)
    buf_rows = l + (max_w + 1) * GRID_W
    return pl.pallas_call(
        functools.partial(_pool_kernel, n_tok=l),
        grid=(n_g, b),
        in_specs=[pl.BlockSpec((1, l, POOL_GROUP), lambda g, bi: (bi, 0, g)),
                  pl.BlockSpec((1, POOL_GROUP, POOL_GROUP), lambda g, bi: (g, 0, 0)),
                  pl.BlockSpec((1, 1, POOL_GROUP), lambda g, bi: (g, 0, 0))],
        out_specs=pl.BlockSpec((1, l, POOL_GROUP), lambda g, bi: (bi, 0, g)),
        out_shape=jax.ShapeDtypeStruct((b, l, POOL_W), BF16),
        scratch_shapes=[pltpu.VMEM((buf_rows, LANES), F32), pltpu.VMEM((l, LANES), F32),
                        pltpu.VMEM((l, LANES), BF16)],
        compiler_params=_params("arbitrary", "arbitrary"),
        name="pool",
    )(p, w_pool, pool_scale.reshape(n_g, 1, POOL_GROUP))


ROUTER_ROWS = 48
ROUTER_GROUP_ROW0 = N_EXPERTS
TOP_K = 2
SORT_TM = 512
PIECE = 16
SLOT_ROWS = TOP_K * SORT_TM + N_EXPERTS * PIECE
META_W1_HI, META_W1_LO, META_W2_HI, META_W2_LO, META_E1, META_E2 = range(6)
INFO_DEST1, INFO_DEST2 = 6, 7


def _outproj_kernel(of_ref, ob_ref, g_ref, pooled_ref, x_ref, gn_ref, wout_ref, gt_ref, sh_ref, sc_ref, g2_ref,
                    wr_hi_ref, wr_lo_ref, x1_ref, xs_ref, info_ref, cnt_ref):
    o = of_ref[0].astype(F32) + ob_ref[0].astype(F32)
    gate = _silu(g_ref[0].astype(F32))
    parts = []
    for h in range(GLA_HEADS):
        sl = slice(h * GLA_DV, (h + 1) * GLA_DV)
        parts.append((_rms(o[:, sl]) * gn_ref[...] * gate[:, sl]).astype(BF16))
    o_n = jnp.concatenate(parts, axis=-1)
    out = _dot(o_n, wout_ref[0:V_W, :]) + _dot(pooled_ref[0], wout_ref[V_W:V_W + POOL_W, :])
    x1 = x_ref[0] + gt_ref[0] * out
    x1_ref[0] = x1
    h2 = _rms(x1) * g2_ref[...]
    h2 = h2 * (1.0 + sc_ref[0]) + sh_ref[0]
    h2_bf = h2.astype(BF16)
    h2_lo = (h2 - h2_bf.astype(F32)).astype(BF16)
    logits = (lax.dot_general(wr_hi_ref[...], h2_bf, _NT, preferred_element_type=F32)
              + lax.dot_general(wr_hi_ref[...], h2_lo, _NT, preferred_element_type=F32)
              + lax.dot_general(wr_lo_ref[...], h2_bf, _NT, preferred_element_type=F32))
    tm = logits.shape[1]
    row = lax.broadcasted_iota(jnp.int32, logits.shape, 0)
    neg = -jnp.inf
    big = jnp.int32(1 << 20)
    is_group = (row >= ROUTER_GROUP_ROW0) & (row < ROUTER_GROUP_ROW0 + N_GROUPS)
    gl = jnp.where(is_group, logits, neg)
    g_max = jnp.max(gl, axis=0, keepdims=True)
    g_idx = jnp.min(jnp.where(gl == g_max, row - ROUTER_GROUP_ROW0, big), axis=0, keepdims=True)
    g_prob = 1.0 / jnp.sum(jnp.where(is_group, jnp.exp(gl - g_max), 0.0), axis=0, keepdims=True)
    n_e = EXPERTS_PER_GROUP
    el = logits[0:n_e]
    for g in range(1, N_GROUPS):
        el = jnp.where(g_idx == g, logits[g * n_e:(g + 1) * n_e], el)
    r8 = lax.broadcasted_iota(jnp.int32, el.shape, 0)
    m1 = jnp.max(el, axis=0, keepdims=True)
    i1 = jnp.min(jnp.where(el == m1, r8, big), axis=0, keepdims=True)
    el2 = jnp.where(r8 == i1, neg, el)
    m2 = jnp.max(el2, axis=0, keepdims=True)
    i2 = jnp.min(jnp.where(el2 == m2, r8, big), axis=0, keepdims=True)
    r = jnp.exp(m2 - m1)
    w1 = g_prob / (1.0 + r)
    w2 = g_prob * r / (1.0 + r)

    e1 = g_idx * n_e + i1
    e2 = g_idx * n_e + i2
    r_e = lax.broadcasted_iota(jnp.int32, (N_EXPERTS, tm), 0)
    pick1 = r_e == e1
    pick2 = r_e == e2
    onehot = jnp.where(pick1 | pick2, 1.0, 0.0)
    t_r = lax.broadcasted_iota(jnp.int32, (tm, tm), 0)
    t_c = lax.broadcasted_iota(jnp.int32, (tm, tm), 1)
    rank = _dot(onehot.astype(BF16), jnp.where(t_r < t_c, 1.0, 0.0).astype(BF16))
    count = jnp.sum(onehot, axis=1, keepdims=True)
    n_pieces = jnp.floor((count + (PIECE - 1)) * (1.0 / PIECE))
    e_r = lax.broadcasted_iota(jnp.int32, (N_EXPERTS, N_EXPERTS), 0)
    e_c = lax.broadcasted_iota(jnp.int32, (N_EXPERTS, N_EXPERTS), 1)
    run_start = _dot(jnp.where(e_c < e_r, 1.0, 0.0).astype(BF16),
                     jnp.broadcast_to(n_pieces, (N_EXPERTS, LANES)).astype(BF16))[:, 0:1] * PIECE
    slot = rank + run_start
    dest1 = jnp.sum(jnp.where(pick1, slot, 0.0), axis=0, keepdims=True)
    dest2 = jnp.sum(jnp.where(pick2, slot, 0.0), axis=0, keepdims=True)
    w1_hi = w1.astype(BF16).astype(F32)
    w2_hi = w2.astype(BF16).astype(F32)
    info_rows = [w1_hi, (w1 - w1_hi).astype(BF16).astype(F32), w2_hi, (w2 - w2_hi).astype(BF16).astype(F32),
                 e1.astype(F32), e2.astype(F32), dest1, dest2]
    info_t = jnp.concatenate(info_rows + [jnp.zeros((LANES - len(info_rows), tm), F32)], axis=0)
    info = info_t.T
    info_ref[0] = info
    cnt_ref[0] = jnp.broadcast_to(count, (N_EXPERTS, LANES)).astype(jnp.int32)
    s_id = lax.broadcasted_iota(jnp.int32, (SLOT_ROWS, tm), 0)
    perm_t = jnp.where((s_id == dest1.astype(jnp.int32)) | (s_id == dest2.astype(jnp.int32)),
                       1.0, 0.0).astype(BF16)
    lane = lax.broadcasted_iota(jnp.int32, info.shape, 1)
    meta = jnp.where(lane < INFO_DEST1, info, 0.0).astype(BF16)
    d = h2_bf.shape[1]
    xs_ref[0, :, 0:d] = _dot(perm_t, h2_bf).astype(BF16)
    xs_ref[0, :, d:d + LANES] = _dot(perm_t, meta).astype(BF16)


def _outproj(o_f, o_b, g, pooled, x, gn, w_out, gt_a, sh_m, sc_m, g2, wr_hi, wr_lo, tm):
    b, l, d = x.shape
    per_batch = l // tm
    n_tiles = b * per_batch
    tok = lambda s: (s // per_batch, s % per_batch, 0)
    vec = lambda s: (s // per_batch, 0, 0)
    fixed = lambda s: (0, 0)
    return pl.pallas_call(
        _outproj_kernel,
        grid=(n_tiles,),
        in_specs=[pl.BlockSpec((1, tm, V_W), tok), pl.BlockSpec((1, tm, V_W), tok),
                  pl.BlockSpec((1, tm, V_W), tok), pl.BlockSpec((1, tm, POOL_W), tok),
                  pl.BlockSpec((1, tm, d), tok),
                  pl.BlockSpec((1, GLA_DV), fixed),
                  pl.BlockSpec(w_out.shape, fixed),
                  pl.BlockSpec((1, 1, d), vec), pl.BlockSpec((1, 1, d), vec), pl.BlockSpec((1, 1, d), vec),
                  pl.BlockSpec((1, d), fixed),
                  pl.BlockSpec(wr_hi.shape, fixed), pl.BlockSpec(wr_lo.shape, fixed)],
        out_specs=[pl.BlockSpec((1, tm, d), tok),
                   pl.BlockSpec((1, SLOT_ROWS, d + LANES), lambda s: (s, 0, 0)),
                   pl.BlockSpec((1, tm, LANES), tok),
                   pl.BlockSpec((1, N_EXPERTS, LANES), lambda s: (s, 0, 0))],
        out_shape=[jax.ShapeDtypeStruct((b, l, d), F32),
                   jax.ShapeDtypeStruct((n_tiles, SLOT_ROWS, d + LANES), BF16),
                   jax.ShapeDtypeStruct((b, l, LANES), F32),
                   jax.ShapeDtypeStruct((n_tiles, N_EXPERTS, LANES), jnp.int32)],
        compiler_params=_params("arbitrary"),
        name="outproj_router",
    )(o_f, o_b, g, pooled, x, gn, w_out, gt_a, sh_m, sc_m, g2, wr_hi, wr_lo)


EXPERT_TM = 512
PIECES_PER_STEP = EXPERT_TM // PIECE
GATHER_SLOTS = 6
SLOT_PIECES = SLOT_ROWS // PIECE


def _expert_steps(n_tok):
    pieces = TOP_K * n_tok // PIECE + (n_tok // SORT_TM) * N_EXPERTS
    return -(-pieces // PIECES_PER_STEP) + N_EXPERTS + 1


def _experts_kernel(exp_ref, valid_ref, src_ref, xs_hbm, wi_ref, wo_ref, ys_ref, xbuf, wi_bf, wo_bf, sem):
    i = pl.program_id(0)
    n_steps = pl.num_programs(0)
    slot = i % GATHER_SLOTS
    d = ys_ref.shape[1]

    def piece_copy(s, j, src_row):
        return pltpu.make_async_copy(xs_hbm.at[pl.ds(src_row, PIECE), :], xbuf.at[s, pl.ds(j * PIECE, PIECE), :],
                                     sem.at[s])

    def fetch(step):
        @pl.when((step < n_steps) & (valid_ref[jnp.minimum(step, n_steps - 1)] == 1))
        def _():
            for j in range(PIECES_PER_STEP):
                piece_copy(step % GATHER_SLOTS, j,
                           pl.multiple_of(src_ref[step * PIECES_PER_STEP + j], PIECE)).start(priority=j % 2)

    @pl.when(i == 0)
    def _():
        for ahead in range(GATHER_SLOTS - 1):
            fetch(jnp.int32(ahead))

    fetch(i + GATHER_SLOTS - 1)
    expert = exp_ref[i]

    @pl.when((i == 0) | (exp_ref[jnp.maximum(i - 1, 0)] != expert))
    def _():
        wi_bf[...] = wi_ref[0].astype(BF16)
        wo_bf[...] = wo_ref[0].astype(BF16)

    @pl.when(valid_ref[i] == 1)
    def _():
        for j in range(PIECES_PER_STEP):
            piece_copy(slot, j, 0).wait()
        x = xbuf[slot, :, 0:d]
        meta = xbuf[slot, :, d:d + LANES].astype(F32)
        lane = lax.broadcasted_iota(jnp.int32, meta.shape, 1)

        def pick(*lanes):
            sel = lane == lanes[0]
            for ln in lanes[1:]:
                sel = sel | (lane == ln)
            return jnp.sum(jnp.where(sel, meta, 0.0), axis=-1, keepdims=True)

        is_first = pick(META_E1) == expert.astype(F32)
        gate = jnp.where(is_first, pick(META_W1_HI, META_W1_LO), pick(META_W2_HI, META_W2_LO))
        au = _dot(x, wi_bf[...])
        hidden = (_silu(au[:, :D_EXPERT]) * au[:, D_EXPERT:] * gate).astype(BF16)
        ys_ref[...] = _dot(hidden, wo_bf[...]).astype(BF16)

    @pl.when(valid_ref[i] == 0)
    def _():
        ys_ref[...] = jnp.zeros_like(ys_ref)


def _experts(xs, step_expert, step_valid, piece_src, wi, wo):
    n_steps = step_expert.shape[0]
    width = xs.shape[1]
    d = wo.shape[2]
    return pl.pallas_call(
        _experts_kernel,
        grid_spec=pltpu.PrefetchScalarGridSpec(
            num_scalar_prefetch=3,
            grid=(n_steps,),
            in_specs=[pl.BlockSpec(memory_space=pl.ANY),
                      pl.BlockSpec((1,) + wi.shape[1:], lambda i, e, v, s: (e[i], 0, 0)),
                      pl.BlockSpec((1,) + wo.shape[1:], lambda i, e, v, s: (e[i], 0, 0))],
            out_specs=pl.BlockSpec((EXPERT_TM, d), lambda i, e, v, s: (i, 0)),
            scratch_shapes=[pltpu.VMEM((GATHER_SLOTS, EXPERT_TM, width), BF16), pltpu.VMEM(wi.shape[1:], BF16),
                            pltpu.VMEM(wo.shape[1:], BF16), pltpu.SemaphoreType.DMA((GATHER_SLOTS,))]),
        out_shape=jax.ShapeDtypeStruct((n_steps * EXPERT_TM, d), BF16),
        compiler_params=_params("arbitrary"),
        name="moe_experts",
    )(step_expert, step_valid, piece_src, xs, wi, wo)


def _combine_kernel(src_ref, ys_hbm, info_ref, x1_ref, gt_ref, gf_ref, o_ref, stage, sem):
    i = pl.program_id(0)
    n_steps = pl.num_programs(0)
    tm = SORT_TM

    def piece_copy(s, src_row, dst_row):
        return pltpu.make_async_copy(ys_hbm.at[pl.ds(src_row, PIECE), :], stage.at[s, pl.ds(dst_row, PIECE), :],
                                     sem.at[s])

    def fetch(tile, s):
        for m in range(SLOT_PIECES):
            piece_copy(s, pl.multiple_of(src_ref[tile * SLOT_PIECES + m], PIECE), m * PIECE).start(priority=m % 2)

    def wait(s):
        for m in range(SLOT_PIECES):
            piece_copy(s, 0, m * PIECE).wait()

    @pl.when(i == 0)
    def _():
        fetch(0, 0)

    slot = i % 2
    fetch(jnp.minimum(i + 1, n_steps - 1), 1 - slot)
    wait(slot)
    ys = stage[slot]
    info = info_ref[...]
    lane = lax.broadcasted_iota(jnp.int32, info.shape, 1)
    dest1 = jnp.sum(jnp.where(lane == INFO_DEST1, info, 0.0), axis=-1, keepdims=True).astype(jnp.int32)
    dest2 = jnp.sum(jnp.where(lane == INFO_DEST2, info, 0.0), axis=-1, keepdims=True).astype(jnp.int32)
    slot_id = lax.broadcasted_iota(jnp.int32, (tm, SLOT_ROWS), 1)
    perm = jnp.where((slot_id == dest1) | (slot_id == dest2), 1.0, 0.0).astype(BF16)
    y = _dot(perm, ys)
    x2 = x1_ref[...] + gt_ref[0] * y
    o_ref[...] = _rms(x2) * gf_ref[...]

    @pl.when(i == n_steps - 1)
    def _():
        wait(1 - slot)


def _combine(ys, piece_src, info, x1, gt_m, gf, tiles_per_batch):
    t, d = x1.shape
    n_tiles = t // SORT_TM
    tok = lambda i, *_: (i, 0)
    return pl.pallas_call(
        _combine_kernel,
        grid_spec=pltpu.PrefetchScalarGridSpec(
            num_scalar_prefetch=1,
            grid=(n_tiles,),
            in_specs=[pl.BlockSpec(memory_space=pl.ANY),
                      pl.BlockSpec((SORT_TM, LANES), tok),
                      pl.BlockSpec((SORT_TM, d), tok),
                      pl.BlockSpec((1, 1, d), lambda i, *_: (i // tiles_per_batch, 0, 0)),
                      pl.BlockSpec((1, d), lambda i, *_: (0, 0))],
            out_specs=pl.BlockSpec((SORT_TM, d), tok),
            scratch_shapes=[pltpu.VMEM((2, SLOT_ROWS, d), BF16), pltpu.SemaphoreType.DMA((2,))]),
        out_shape=jax.ShapeDtypeStruct((t, d), F32),
        compiler_params=_params("arbitrary"),
        name="moe_combine",
    )(piece_src, ys, info, x1, gt_m, gf)


def _sparse_moe(xs, info, counts, x1, wi, wo, gt_m, gf):
    b, l, d = x1.shape
    t = b * l
    n_tiles = t // SORT_TM
    i32 = jnp.int32

    cnt = counts[:, :, 0]
    run_pieces = (cnt + (PIECE - 1)) // PIECE
    run_slot = (jnp.cumsum(run_pieces, axis=1) - run_pieces) * PIECE
    before = jnp.cumsum(run_pieces, axis=0) - run_pieces
    expert_pieces = jnp.sum(run_pieces, axis=0)
    steps_e = (expert_pieces + PIECES_PER_STEP - 1) // PIECES_PER_STEP
    step_end = jnp.cumsum(steps_e)
    region = (step_end - steps_e) * PIECES_PER_STEP
    run_pos = (region[None, :] + before) * PIECE

    n_steps = _expert_steps(t)
    step = jnp.arange(n_steps, dtype=i32)
    step_expert = jnp.minimum(jnp.sum((step[:, None] >= step_end[None, :]).astype(i32), axis=1), N_EXPERTS - 1)
    step_valid = (step < step_end[-1]).astype(i32)
    lookup = lambda onehot, table: jnp.dot(onehot, table.astype(F32), precision=lax.Precision.HIGHEST)
    piece = jnp.arange(n_steps * PIECES_PER_STEP, dtype=i32)
    is_expert = (jnp.repeat(step_expert, PIECES_PER_STEP)[:, None] == jnp.arange(N_EXPERTS, dtype=i32)).astype(F32)
    k_local = piece.astype(F32) - lookup(is_expert, region)
    ends = lookup(is_expert, jnp.cumsum(run_pieces, axis=0).T)
    tile_of = jnp.minimum(jnp.sum((ends <= k_local[:, None]).astype(i32), axis=1), n_tiles - 1)
    is_tile = (tile_of[:, None] == jnp.arange(n_tiles, dtype=i32)).astype(F32)
    j_in_run = k_local - jnp.sum(is_tile * lookup(is_expert, before.T), axis=1)
    slot_start = jnp.sum(is_tile * lookup(is_expert, run_slot.T), axis=1)
    real = (k_local < lookup(is_expert, expert_pieces)) & (jnp.repeat(step_valid, PIECES_PER_STEP) == 1)
    zero_piece = SLOT_ROWS - PIECE
    piece_src = jnp.where(real, tile_of * SLOT_ROWS + (slot_start + j_in_run * PIECE).astype(i32), zero_piece)


    m = jnp.arange(SLOT_PIECES, dtype=i32)
    run_end = jnp.cumsum(run_pieces, axis=1)
    run_of = jnp.minimum(jnp.sum((run_end[:, None, :] <= m[None, :, None]).astype(i32), axis=-1), N_EXPERTS - 1)
    is_run = run_of[:, :, None] == jnp.arange(N_EXPERTS, dtype=i32)
    pos_m = jnp.sum(jnp.where(is_run, run_pos[:, None, :], 0), axis=-1)
    first_m = jnp.sum(jnp.where(is_run, (run_end - run_pieces)[:, None, :], 0), axis=-1)
    zero_row = (n_steps - 1) * EXPERT_TM
    back_src = jnp.where(m[None, :] < run_end[:, -1:], pos_m + (m[None, :] - first_m) * PIECE, zero_row)

    flat = lambda a: a.reshape(-1).astype(i32)
    ys = _experts(xs.reshape(n_tiles * SLOT_ROWS, d + LANES), step_expert, step_valid, flat(piece_src), wi, wo)
    out = _combine(ys, flat(back_src), info.reshape(t, LANES), x1.reshape(t, d), gt_m, gf, l // SORT_TM)
    return out.reshape(b, l, d)


def kernel(x, c, ctx, c_ctx, w_ada, b_ada, norm1_g, w_in, w_decay, b_decay, gla_norm_g, w_pool, pool_scale, w_out,
           norm2_g, w_router_group, w_router_expert, w_expert_in, w_expert_out, final_norm_g):
    assert w_ada.shape[0] == 1, "single-layer trunk"
    b, l, d = x.shape
    off_a = 2 * QK_W + 2 * V_W
    a_cols = 2 * GATE_RANK
    off_p = off_a + a_cols

    c8 = jnp.zeros((8, d), F32).at[:b].set(c).at[b].set(c_ctx)
    mod = _ada_mod(c8, w_ada[0], b_ada[0])
    sh_a, sc_a, gt_a, sh_m, sc_m, gt_m = [m.reshape(8, 1, d) for m in jnp.split(mod, 6, axis=-1)]
    ctx_rows = lambda m: jnp.broadcast_to(m[b:b + 1], (b, 1, d))

    w = w_in[0]
    w_r = jnp.concatenate([w[:, :off_a], w[:, off_p:off_p + POOL_W], w[:, off_a:off_p],
                           jnp.zeros((d, LANES - a_cols), F32)], axis=1).astype(BF16)
    wdec = jnp.zeros((LANES, 2 * QK_W), F32)
    wdec = wdec.at[:GATE_RANK, :QK_W].set(w_decay[0, 0]).at[GATE_RANK:a_cols, QK_W:].set(w_decay[0, 1]).astype(BF16)
    bdec = b_decay[0].reshape(1, 2 * QK_W)
    g1 = norm1_g[0].reshape(1, d)

    q_c, k_c, v_c, _, la_c, _ = _inproj(ctx, ctx_rows(sh_a), ctx_rows(sc_a), g1, w_r, wdec, bdec, tm=ctx.shape[1])
    zero_s = jnp.zeros((b, QK_W, GLA_DV), F32)
    _, _, s_f, s_b = _gla(q_c, k_c, v_c, la_c, zero_s, zero_s, tb=ctx.shape[1])

    q, k, v, g, la, p = _inproj(x, sh_a[:b], sc_a[:b], g1, w_r, wdec, bdec, tm=1024)
    o_f, o_b, _, _ = _gla(q, k, v, la, s_f, s_b, tb=1024)
    pooled = _pool(p, w_pool[0], pool_scale[0])

    w_router = jnp.zeros((ROUTER_ROWS, d), F32)
    w_router = w_router.at[:N_EXPERTS].set(w_router_expert[0].T)
    w_router = w_router.at[ROUTER_GROUP_ROW0:ROUTER_GROUP_ROW0 + N_GROUPS].set(w_router_group[0].T)
    wr_hi = w_router.astype(BF16)
    wr_lo = (w_router - wr_hi.astype(F32)).astype(BF16)
    x1, xs, info, counts = _outproj(o_f, o_b, g, pooled, x, gla_norm_g[0].reshape(1, GLA_DV), w_out[0].astype(BF16),
                                    gt_a[:b], sh_m[:b], sc_m[:b], norm2_g[0].reshape(1, d), wr_hi, wr_lo,
                                    tm=SORT_TM)

    return _sparse_moe(xs, info, counts, x1, w_expert_in[0], w_expert_out[0], gt_m[:b], final_norm_g.reshape(1, d))
```

```python
import functools

import jax
import jax.numpy as jnp
from jax import lax
from jax.experimental import pallas as pl
from jax.experimental.pallas import tpu as pltpu

F32 = jnp.float32
BF16 = jnp.bfloat16

GRID_W = 64
GRID_W_LOG2 = 6
assert 1 << GRID_W_LOG2 == GRID_W
GLA_HEADS = 4
GLA_DK = 64
GLA_DV = 128
GATE_RANK = 16
GATE_NORMALIZER = 16.0
CHUNK = 64
POOL_WINDOWS = (2, 4, 8, 16)
POOL_GROUP = 128
N_GROUPS = 4
EXPERTS_PER_GROUP = 8
N_EXPERTS = N_GROUPS * EXPERTS_PER_GROUP
D_EXPERT = 256
EPS = 1e-6

QK_W = GLA_HEADS * GLA_DK
V_W = GLA_HEADS * GLA_DV
POOL_W = POOL_GROUP * len(POOL_WINDOWS)
LANES = 128
VMEM_LIMIT = 48 * 1024 * 1024

_NT = (((1,), (1,)), ((), ()))


def _dot(a, b):
    return jnp.dot(a, b, preferred_element_type=F32)


def _split_bf16(x):
    hi = x.astype(BF16)
    lo = (x - hi.astype(F32)).astype(BF16)
    return hi, lo


def _rms(x):
    return x * lax.rsqrt(jnp.mean(x * x, axis=-1, keepdims=True) + EPS)


def _silu(x):
    return x / (1.0 + jnp.exp(-x))


def _params(*sem):
    return pltpu.CompilerParams(dimension_semantics=sem, vmem_limit_bytes=VMEM_LIMIT)


def _ada_kernel(c_ref, w_ref, b_ref, o_ref):
    s = _silu(c_ref[...]).astype(BF16)
    o_ref[...] = _dot(s, w_ref[...].astype(BF16)) + b_ref[...]


def _ada_mod(c8, w_ada, b_ada):
    rows, d = c8.shape
    n = w_ada.shape[1]
    tn = 1024
    return pl.pallas_call(
        _ada_kernel,
        grid=(n // tn,),
        in_specs=[pl.BlockSpec((rows, d), lambda j: (0, 0)),
                  pl.BlockSpec((d, tn), lambda j: (0, j)),
                  pl.BlockSpec((1, tn), lambda j: (0, j))],
        out_specs=pl.BlockSpec((rows, tn), lambda j: (0, j)),
        out_shape=jax.ShapeDtypeStruct((rows, n), F32),
        compiler_params=_params("arbitrary"),
        name="ada_mod",
    )(c8, w_ada, b_ada.reshape(1, n))


def _inproj_kernel(x_ref, sh_ref, sc_ref, g1_ref, w_ref, wdec_ref, bdec_ref,
                   q_ref, k_ref, v_ref, g_ref, la_ref, p_ref):
    x = x_ref[0]
    h = _rms(x) * g1_ref[...]
    h = h * (1.0 + sc_ref[0]) + sh_ref[0]
    hb = h.astype(BF16)
    o = 0
    q_ref[0] = (_dot(hb, w_ref[:, o:o + QK_W]) * (GLA_DK ** -0.5)).astype(BF16)
    o += QK_W
    k_ref[0] = _dot(hb, w_ref[:, o:o + QK_W]).astype(BF16)
    o += QK_W
    v_ref[0] = _dot(hb, w_ref[:, o:o + V_W]).astype(BF16)
    o += V_W
    g_ref[0] = _dot(hb, w_ref[:, o:o + V_W]).astype(BF16)
    o += V_W
    p_ref[0] = _dot(hb, w_ref[:, o:o + POOL_W]).astype(BF16)
    o += POOL_W
    a_low = _dot(hb, w_ref[:, o:o + LANES])
    z = _dot(a_low.astype(BF16), wdec_ref[...]) + bdec_ref[...]
    log_sig = jnp.minimum(z, 0.0) - jnp.log(1.0 + jnp.exp(-jnp.abs(z)))
    la_ref[0] = log_sig / GATE_NORMALIZER


def _inproj(x, shift, scale, g1, w_r, wdec, bdec, tm):
    b, l, d = x.shape
    wcols = w_r.shape[1]
    tok = lambda bi, i: (bi, i, 0)
    vec = lambda bi, i: (bi, 0, 0)
    fixed = lambda bi, i: (0, 0)
    outs = [(QK_W, BF16), (QK_W, BF16), (V_W, BF16), (V_W, BF16), (2 * QK_W, F32), (POOL_W, BF16)]
    return pl.pallas_call(
        _inproj_kernel,
        grid=(b, l // tm),
        in_specs=[pl.BlockSpec((1, tm, d), tok),
                  pl.BlockSpec((1, 1, d), vec),
                  pl.BlockSpec((1, 1, d), vec),
                  pl.BlockSpec((1, d), fixed),
                  pl.BlockSpec((d, wcols), fixed),
                  pl.BlockSpec(wdec.shape, fixed),
                  pl.BlockSpec(bdec.shape, fixed)],
        out_specs=[pl.BlockSpec((1, tm, w), tok) for w, _ in outs],
        out_shape=[jax.ShapeDtypeStruct((b, l, w), dt) for w, dt in outs],
        compiler_params=_params("arbitrary", "arbitrary"),
        name="inproj",
    )(x, shift, scale, g1, w_r, wdec, bdec)


GLA_GROUP = 256
CHUNKS_PER_GROUP = GLA_GROUP // CHUNK
CHUNK_LOG2 = 6
assert 1 << CHUNK_LOG2 == CHUNK


def _head_masks(rows):
    lane = lax.broadcasted_iota(jnp.int32, (rows, QK_W), 1)
    return [(lane >= h * GLA_DK) & (lane < (h + 1) * GLA_DK) for h in range(GLA_HEADS)]


def _gla_bulk(items, qd_scr, oi_scr, ds_scr, dc_scr):
    n = GLA_GROUP
    i0 = lax.broadcasted_iota(jnp.int32, (n, n), 0)
    i1 = lax.broadcasted_iota(jnp.int32, (n, n), 1)
    same_chunk = (i0 >> CHUNK_LOG2) == (i1 >> CHUNK_LOG2)
    causal = {False: same_chunk & (i1 <= i0), True: same_chunk & (i1 >= i0)}
    tri = {rev: jnp.where(m, 1.0, 0.0).astype(BF16) for rev, m in causal.items()}
    masks = _head_masks(n)
    rows = [pl.multiple_of(g * n, n) for _, _, g, *_ in items]

    cum = []
    for (d, rev, g, q_ref, k_ref, v_ref, la_ref), r in zip(items, rows):
        la_hi, la_lo = _split_bf16(la_ref[0, pl.ds(r, n), :])
        cum.append(_dot(tri[rev], la_hi) + _dot(tri[rev], la_lo))

    decayed = []
    for (d, rev, g, q_ref, k_ref, v_ref, la_ref), r, b in zip(items, rows, cum):
        last_row = 0 if rev else CHUNK - 1
        b_end = b.reshape(CHUNKS_PER_GROUP, CHUNK, QK_W)[:, last_row:last_row + 1, :]
        b_last = jnp.broadcast_to(b_end, (CHUNKS_PER_GROUP, CHUNK, QK_W)).reshape(n, QK_W)
        k = k_ref[0, pl.ds(r, n), :].astype(F32)
        qd = (q_ref[0, pl.ds(r, n), :].astype(F32) * jnp.exp(b)).astype(BF16)
        qd_scr[d, pl.ds(r, n), :] = qd
        decayed.append((qd, (k * jnp.exp(-b)).astype(BF16), k * jnp.exp(b_last - b), jnp.exp(b_end)))

    scores = [[lax.dot_general(jnp.where(m, qd, jnp.zeros_like(qd)), kd, _NT, preferred_element_type=F32)
               for m in masks] for qd, kd, _, _ in decayed]
    for (d, rev, g, q_ref, k_ref, v_ref, la_ref), r, sc in zip(items, rows, scores):
        for h in range(GLA_HEADS):
            cols = slice(h * GLA_DV, (h + 1) * GLA_DV)
            probs = jnp.where(causal[rev], sc[h], 0.0).astype(BF16)
            oi_scr[d, pl.ds(r, n), cols] = _dot(probs, v_ref[0, pl.ds(r, n), cols])

    for (d, rev, g, q_ref, k_ref, v_ref, la_ref), r, (_, _, k_st, decay) in zip(items, rows, decayed):
        for c in range(CHUNKS_PER_GROUP):
            ci = g * CHUNKS_PER_GROUP + c
            k_st_t = k_st[c * CHUNK:(c + 1) * CHUNK].T.astype(BF16)
            for h in range(GLA_HEADS):
                v_ch = v_ref[0, pl.ds(r + c * CHUNK, CHUNK), h * GLA_DV:(h + 1) * GLA_DV]
                ds_scr[d, ci, h * GLA_DK:(h + 1) * GLA_DK, :] = _dot(k_st_t[h * GLA_DK:(h + 1) * GLA_DK], v_ch)
            dc_scr[d, ci] = jnp.broadcast_to(decay[c], (LANES, QK_W)).T


def _gla_kernel(qf_ref, kf_ref, vf_ref, laf_ref, qb_ref, kb_ref, vb_ref, lab_ref, s0f_ref, s0b_ref,
                of_ref, ob_ref, sf_out_ref, sb_out_ref, s_ref, qd_scr, oi_scr, ds_scr, dc_scr, st_scr, *, n_chunks):
    j = pl.program_id(1)

    @pl.when(j == 0)
    def _():
        s_ref[0] = s0f_ref[0]
        s_ref[1] = s0b_ref[0]

    n_groups = n_chunks // CHUNKS_PER_GROUP
    unroll = 2 if n_groups % 2 == 0 else 1

    def bulk(it, carry):
        items = []
        for u in range(unroll):
            g = it * unroll + u
            items.append((0, False, g, qf_ref, kf_ref, vf_ref, laf_ref))
            items.append((1, True, g, qb_ref, kb_ref, vb_ref, lab_ref))
        _gla_bulk(items, qd_scr, oi_scr, ds_scr, dc_scr)
        return carry
    lax.fori_loop(0, n_groups // unroll, bulk, 0)

    def recur(i, carry):
        for d, ci in ((0, i), (1, n_chunks - 1 - i)):
            s = s_ref[d]
            st_scr[d, ci] = s.astype(BF16)
            s_ref[d] = dc_scr[d, ci] * s + ds_scr[d, ci]
        return carry
    lax.fori_loop(0, n_chunks, recur, 0)

    masks = _head_masks(CHUNK)

    def inter(it, carry):
        work = [(d, o_ref, (it * unroll + u) * CHUNKS_PER_GROUP + c)
                for u in range(unroll) for d, o_ref in ((0, of_ref), (1, ob_ref)) for c in range(CHUNKS_PER_GROUP)]
        from_state = []
        for d, o_ref, ci in work:
            qd = qd_scr[d, pl.ds(pl.multiple_of(ci * CHUNK, CHUNK), CHUNK), :]
            lhs = jnp.concatenate([jnp.where(m, qd, jnp.zeros_like(qd)) for m in masks], axis=0)
            from_state.append(_dot(lhs, st_scr[d, ci]))
        for (d, o_ref, ci), fs in zip(work, from_state):
            r = pl.multiple_of(ci * CHUNK, CHUNK)
            for h in range(GLA_HEADS):
                cols = slice(h * GLA_DV, (h + 1) * GLA_DV)
                o_ref[0, pl.ds(r, CHUNK), cols] = (oi_scr[d, pl.ds(r, CHUNK), cols]
                                                   + fs[h * CHUNK:(h + 1) * CHUNK]).astype(BF16)
        return carry
    lax.fori_loop(0, n_groups // unroll, inter, 0)

    @pl.when(j == pl.num_programs(1) - 1)
    def _():
        sf_out_ref[0] = s_ref[0]
        sb_out_ref[0] = s_ref[1]


def _gla(q, k, v, la, s0f, s0b, tb):
    b, l, _ = q.shape
    assert tb % GLA_GROUP == 0 and l % tb == 0
    nb = l // tb
    n_chunks = tb // CHUNK
    fwd = lambda bi, j: (bi, j, 0)
    bwd = lambda bi, j: (bi, nb - 1 - j, 0)
    fwd_la = lambda bi, j: (bi, j, 0)
    bwd_la = lambda bi, j: (bi, nb - 1 - j, 1)
    st = lambda bi, j: (bi, 0, 0)
    state_shape = (b, QK_W, GLA_DV)
    return pl.pallas_call(
        functools.partial(_gla_kernel, n_chunks=tb // CHUNK),
        grid=(b, nb),
        in_specs=[pl.BlockSpec((1, tb, QK_W), fwd), pl.BlockSpec((1, tb, QK_W), fwd),
                  pl.BlockSpec((1, tb, V_W), fwd), pl.BlockSpec((1, tb, QK_W), fwd_la),
                  pl.BlockSpec((1, tb, QK_W), bwd), pl.BlockSpec((1, tb, QK_W), bwd),
                  pl.BlockSpec((1, tb, V_W), bwd), pl.BlockSpec((1, tb, QK_W), bwd_la),
                  pl.BlockSpec((1, QK_W, GLA_DV), st), pl.BlockSpec((1, QK_W, GLA_DV), st)],
        out_specs=[pl.BlockSpec((1, tb, V_W), fwd), pl.BlockSpec((1, tb, V_W), bwd),
                   pl.BlockSpec((1, QK_W, GLA_DV), st), pl.BlockSpec((1, QK_W, GLA_DV), st)],
        out_shape=[jax.ShapeDtypeStruct((b, l, V_W), BF16), jax.ShapeDtypeStruct((b, l, V_W), BF16),
                   jax.ShapeDtypeStruct(state_shape, F32), jax.ShapeDtypeStruct(state_shape, F32)],
        scratch_shapes=[pltpu.VMEM((2, QK_W, GLA_DV), F32),
                        pltpu.VMEM((2, tb, QK_W), BF16),
                        pltpu.VMEM((2, tb, V_W), F32),
                        pltpu.VMEM((2, n_chunks, QK_W, GLA_DV), F32),
                        pltpu.VMEM((2, n_chunks, QK_W, GLA_DV), F32),
                        pltpu.VMEM((2, n_chunks, QK_W, GLA_DV), BF16)],
        compiler_params=_params("arbitrary", "arbitrary"),
        name="gla",
    )(q, k, v, la, q, k, v, la, s0f, s0b)


POOL_BLK = 256
POOL_COL_STEP = 1024
POOL_ROW_STEP = 256
POOL_OUT_STEP = 1024


def _pool_body(w, first_batch, p_ref, wp_ref, ps_ref, o_ref, buf_ref, inv_ref, y_ref, n_tok):
    lo = w // 2
    hi = w - 1 - lo
    pad = lo * GRID_W
    n_rows = n_tok // GRID_W
    blk = POOL_BLK

    @pl.when(first_batch)
    def _():
        zeros = jnp.zeros((GRID_W * 8, LANES), F32)
        buf_ref[0:pad, :] = zeros[0:pad]
        buf_ref[pad + n_tok:pad + n_tok + hi * GRID_W + GRID_W, :] = zeros[0:hi * GRID_W + GRID_W]

        def inv_step(t, carry):
            r = pl.multiple_of(t * POOL_OUT_STEP, POOL_OUT_STEP)
            tok = r + lax.broadcasted_iota(jnp.int32, (POOL_OUT_STEP, LANES), 0)
            g_row = tok >> GRID_W_LOG2
            g_col = tok & (GRID_W - 1)
            cnt_c = jnp.minimum(g_col + hi, GRID_W - 1) - jnp.maximum(g_col - lo, 0) + 1
            cnt_r = jnp.minimum(g_row + hi, n_rows - 1) - jnp.maximum(g_row - lo, 0) + 1
            inv_ref[pl.ds(r, POOL_OUT_STEP), :] = 1.0 / (cnt_c * cnt_r).astype(F32)
            return carry
        lax.fori_loop(0, n_tok // POOL_OUT_STEP, inv_step, 0)

    i0 = lax.broadcasted_iota(jnp.int32, (blk, blk), 0)
    i1 = lax.broadcasted_iota(jnp.int32, (blk, blk), 1)
    same_row = (i0 >> GRID_W_LOG2) == (i1 >> GRID_W_LOG2)
    band = jnp.where(same_row & (i1 - i0 >= -lo) & (i1 - i0 <= hi), 1.0, 0.0).astype(BF16)

    def col_step(t, carry):
        rows = [pl.multiple_of(t * POOL_COL_STEP + u * blk, blk) for u in range(POOL_COL_STEP // blk)]
        sums = [_dot(band, p_ref[0, pl.ds(r, blk), :]) for r in rows]
        for r, s in zip(rows, sums):
            buf_ref[pl.ds(pad + r, blk), :] = s
        return carry
    lax.fori_loop(0, n_tok // POOL_COL_STEP, col_step, 0)

    def row_step(t, carry):
        r = pl.multiple_of(t * POOL_ROW_STEP, POOL_ROW_STEP)
        acc = buf_ref[pl.ds(r, POOL_ROW_STEP), :]
        for d in range(1, w):
            acc = acc + buf_ref[pl.ds(r + d * GRID_W, POOL_ROW_STEP), :]
        m = acc * inv_ref[pl.ds(r, POOL_ROW_STEP), :]
        y_ref[pl.ds(r, POOL_ROW_STEP), :] = (m - p_ref[0, pl.ds(r, POOL_ROW_STEP), :].astype(F32)).astype(BF16)
        return carry
    lax.fori_loop(0, n_tok // POOL_ROW_STEP, row_step, 0)

    wp = wp_ref[0].astype(BF16)
    scale = ps_ref[0]

    def out_step(t, carry):
        r = pl.multiple_of(t * POOL_OUT_STEP, POOL_OUT_STEP)
        o_ref[0, pl.ds(r, POOL_OUT_STEP), :] = (_dot(y_ref[pl.ds(r, POOL_OUT_STEP), :], wp) * scale).astype(BF16)
        return carry
    lax.fori_loop(0, n_tok // POOL_OUT_STEP, out_step, 0)


def _pool_kernel(p_ref, wp_ref, ps_ref, o_ref, buf_ref, inv_ref, y_ref, *, n_tok):
    g = pl.program_id(0)
    first_batch = pl.program_id(1) == 0
    for gi, w in enumerate(POOL_WINDOWS):
        @pl.when(g == gi)
        def _(w=w):
            _pool_body(w, first_batch, p_ref, wp_ref, ps_ref, o_ref, buf_ref, inv_ref, y_ref, n_tok)


def _pool(p, w_pool, pool_scale):
    b, l, _ = p.shape
    n_g = len(POOL_WINDOWS)
    max_w = max(POOL_WINDOWS)
    buf_rows = l + (max_w + 1) * GRID_W
    return pl.pallas_call(
        functools.partial(_pool_kernel, n_tok=l),
        grid=(n_g, b),
        in_specs=[pl.BlockSpec((1, l, POOL_GROUP), lambda g, bi: (bi, 0, g)),
                  pl.BlockSpec((1, POOL_GROUP, POOL_GROUP), lambda g, bi: (g, 0, 0)),
                  pl.BlockSpec((1, 1, POOL_GROUP), lambda g, bi: (g, 0, 0))],
        out_specs=pl.BlockSpec((1, l, POOL_GROUP), lambda g, bi: (bi, 0, g)),
        out_shape=jax.ShapeDtypeStruct((b, l, POOL_W), BF16),
        scratch_shapes=[pltpu.VMEM((buf_rows, LANES), F32), pltpu.VMEM((l, LANES), F32),
                        pltpu.VMEM((l, LANES), BF16)],
        compiler_params=_params("arbitrary", "arbitrary"),
        name="pool",
    )(p, w_pool, pool_scale.reshape(n_g, 1, POOL_GROUP))


ROUTER_ROWS = 48
ROUTER_GROUP_ROW0 = N_EXPERTS
TOP_K = 2
SORT_TM = 512
PIECE = 16
SLOT_ROWS = TOP_K * SORT_TM + N_EXPERTS * PIECE
META_W1_HI, META_W1_LO, META_W2_HI, META_W2_LO, META_E1, META_E2 = range(6)
INFO_DEST1, INFO_DEST2 = 6, 7


def _outproj_kernel(of_ref, ob_ref, g_ref, pooled_ref, x_ref, gn_ref, wout_ref, gt_ref, sh_ref, sc_ref, g2_ref,
                    wr_hi_ref, wr_lo_ref, x1_ref, xs_ref, info_ref, cnt_ref):
    o = of_ref[0].astype(F32) + ob_ref[0].astype(F32)
    gate = _silu(g_ref[0].astype(F32))
    parts = []
    for h in range(GLA_HEADS):
        sl = slice(h * GLA_DV, (h + 1) * GLA_DV)
        parts.append((_rms(o[:, sl]) * gn_ref[...] * gate[:, sl]).astype(BF16))
    o_n = jnp.concatenate(parts, axis=-1)
    out = _dot(o_n, wout_ref[0:V_W, :]) + _dot(pooled_ref[0], wout_ref[V_W:V_W + POOL_W, :])
    x1 = x_ref[0] + gt_ref[0] * out
    x1_ref[0] = x1
    h2 = _rms(x1) * g2_ref[...]
    h2 = h2 * (1.0 + sc_ref[0]) + sh_ref[0]
    h2_bf = h2.astype(BF16)
    h2_lo = (h2 - h2_bf.astype(F32)).astype(BF16)
    logits = (lax.dot_general(wr_hi_ref[...], h2_bf, _NT, preferred_element_type=F32)
              + lax.dot_general(wr_hi_ref[...], h2_lo, _NT, preferred_element_type=F32)
              + lax.dot_general(wr_lo_ref[...], h2_bf, _NT, preferred_element_type=F32))
    tm = logits.shape[1]
    row = lax.broadcasted_iota(jnp.int32, logits.shape, 0)
    neg = -jnp.inf
    big = jnp.int32(1 << 20)
    is_group = (row >= ROUTER_GROUP_ROW0) & (row < ROUTER_GROUP_ROW0 + N_GROUPS)
    gl = jnp.where(is_group, logits, neg)
    g_max = jnp.max(gl, axis=0, keepdims=True)
    g_idx = jnp.min(jnp.where(gl == g_max, row - ROUTER_GROUP_ROW0, big), axis=0, keepdims=True)
    g_prob = 1.0 / jnp.sum(jnp.where(is_group, jnp.exp(gl - g_max), 0.0), axis=0, keepdims=True)
    n_e = EXPERTS_PER_GROUP
    el = logits[0:n_e]
    for g in range(1, N_GROUPS):
        el = jnp.where(g_idx == g, logits[g * n_e:(g + 1) * n_e], el)
    r8 = lax.broadcasted_iota(jnp.int32, el.shape, 0)
    m1 = jnp.max(el, axis=0, keepdims=True)
    i1 = jnp.min(jnp.where(el == m1, r8, big), axis=0, keepdims=True)
    el2 = jnp.where(r8 == i1, neg, el)
    m2 = jnp.max(el2, axis=0, keepdims=True)
    i2 = jnp.min(jnp.where(el2 == m2, r8, big), axis=0, keepdims=True)
    r = jnp.exp(m2 - m1)
    w1 = g_prob / (1.0 + r)
    w2 = g_prob * r / (1.0 + r)

    e1 = g_idx * n_e + i1
    e2 = g_idx * n_e + i2
    r_e = lax.broadcasted_iota(jnp.int32, (N_EXPERTS, tm), 0)
    pick1 = r_e == e1
    pick2 = r_e == e2
    onehot = jnp.where(pick1 | pick2, 1.0, 0.0)
    t_r = lax.broadcasted_iota(jnp.int32, (tm, tm), 0)
    t_c = lax.broadcasted_iota(jnp.int32, (tm, tm), 1)
    rank = _dot(onehot.astype(BF16), jnp.where(t_r < t_c, 1.0, 0.0).astype(BF16))
    count = jnp.sum(onehot, axis=1, keepdims=True)
    n_pieces = jnp.floor((count + (PIECE - 1)) * (1.0 / PIECE))
    e_r = lax.broadcasted_iota(jnp.int32, (N_EXPERTS, N_EXPERTS), 0)
    e_c = lax.broadcasted_iota(jnp.int32, (N_EXPERTS, N_EXPERTS), 1)
    run_start = _dot(jnp.where(e_c < e_r, 1.0, 0.0).astype(BF16),
                     jnp.broadcast_to(n_pieces, (N_EXPERTS, LANES)).astype(BF16))[:, 0:1] * PIECE
    slot = rank + run_start
    dest1 = jnp.sum(jnp.where(pick1, slot, 0.0), axis=0, keepdims=True)
    dest2 = jnp.sum(jnp.where(pick2, slot, 0.0), axis=0, keepdims=True)
    w1_hi = w1.astype(BF16).astype(F32)
    w2_hi = w2.astype(BF16).astype(F32)
    info_rows = [w1_hi, (w1 - w1_hi).astype(BF16).astype(F32), w2_hi, (w2 - w2_hi).astype(BF16).astype(F32),
                 e1.astype(F32), e2.astype(F32), dest1, dest2]
    info_t = jnp.concatenate(info_rows + [jnp.zeros((LANES - len(info_rows), tm), F32)], axis=0)
    info = info_t.T
    info_ref[0] = info
    cnt_ref[0] = jnp.broadcast_to(count, (N_EXPERTS, LANES)).astype(jnp.int32)
    s_id = lax.broadcasted_iota(jnp.int32, (SLOT_ROWS, tm), 0)
    perm_t = jnp.where((s_id == dest1.astype(jnp.int32)) | (s_id == dest2.astype(jnp.int32)),
                       1.0, 0.0).astype(BF16)
    lane = lax.broadcasted_iota(jnp.int32, info.shape, 1)
    meta = jnp.where(lane < INFO_DEST1, info, 0.0).astype(BF16)
    d = h2_bf.shape[1]
    xs_ref[0, :, 0:d] = _dot(perm_t, h2_bf).astype(BF16)
    xs_ref[0, :, d:d + LANES] = _dot(perm_t, meta).astype(BF16)


def _outproj(o_f, o_b, g, pooled, x, gn, w_out, gt_a, sh_m, sc_m, g2, wr_hi, wr_lo, tm):
    b, l, d = x.shape
    per_batch = l // tm
    n_tiles = b * per_batch
    tok = lambda s: (s // per_batch, s % per_batch, 0)
    vec = lambda s: (s // per_batch, 0, 0)
    fixed = lambda s: (0, 0)
    return pl.pallas_call(
        _outproj_kernel,
        grid=(n_tiles,),
        in_specs=[pl.BlockSpec((1, tm, V_W), tok), pl.BlockSpec((1, tm, V_W), tok),
                  pl.BlockSpec((1, tm, V_W), tok), pl.BlockSpec((1, tm, POOL_W), tok),
                  pl.BlockSpec((1, tm, d), tok),
                  pl.BlockSpec((1, GLA_DV), fixed),
                  pl.BlockSpec(w_out.shape, fixed),
                  pl.BlockSpec((1, 1, d), vec), pl.BlockSpec((1, 1, d), vec), pl.BlockSpec((1, 1, d), vec),
                  pl.BlockSpec((1, d), fixed),
                  pl.BlockSpec(wr_hi.shape, fixed), pl.BlockSpec(wr_lo.shape, fixed)],
        out_specs=[pl.BlockSpec((1, tm, d), tok),
                   pl.BlockSpec((1, SLOT_ROWS, d + LANES), lambda s: (s, 0, 0)),
                   pl.BlockSpec((1, tm, LANES), tok),
                   pl.BlockSpec((1, N_EXPERTS, LANES), lambda s: (s, 0, 0))],
        out_shape=[jax.ShapeDtypeStruct((b, l, d), F32),
                   jax.ShapeDtypeStruct((n_tiles, SLOT_ROWS, d + LANES), BF16),
                   jax.ShapeDtypeStruct((b, l, LANES), F32),
                   jax.ShapeDtypeStruct((n_tiles, N_EXPERTS, LANES), jnp.int32)],
        compiler_params=_params("arbitrary"),
        name="outproj_router",
    )(o_f, o_b, g, pooled, x, gn, w_out, gt_a, sh_m, sc_m, g2, wr_hi, wr_lo)


EXPERT_TM = 512
PIECES_PER_STEP = EXPERT_TM // PIECE
GATHER_SLOTS = 6
SLOT_PIECES = SLOT_ROWS // PIECE


def _expert_steps(n_tok):
    pieces = TOP_K * n_tok // PIECE + (n_tok // SORT_TM) * N_EXPERTS
    return -(-pieces // PIECES_PER_STEP) + N_EXPERTS + 1


def _experts_kernel(exp_ref, valid_ref, src_ref, xs_hbm, wi_ref, wo_ref, ys_ref, xbuf, wi_bf, wo_bf, sem):
    i = pl.program_id(0)
    n_steps = pl.num_programs(0)
    slot = i % GATHER_SLOTS
    d = ys_ref.shape[1]

    def piece_copy(s, j, src_row):
        return pltpu.make_async_copy(xs_hbm.at[pl.ds(src_row, PIECE), :], xbuf.at[s, pl.ds(j * PIECE, PIECE), :],
                                     sem.at[s])

    def fetch(step):
        @pl.when((step < n_steps) & (valid_ref[jnp.minimum(step, n_steps - 1)] == 1))
        def _():
            for j in range(PIECES_PER_STEP):
                piece_copy(step % GATHER_SLOTS, j,
                           pl.multiple_of(src_ref[step * PIECES_PER_STEP + j], PIECE)).start(priority=j % 2)

    @pl.when(i == 0)
    def _():
        for ahead in range(GATHER_SLOTS - 1):
            fetch(jnp.int32(ahead))

    fetch(i + GATHER_SLOTS - 1)
    expert = exp_ref[i]

    @pl.when((i == 0) | (exp_ref[jnp.maximum(i - 1, 0)] != expert))
    def _():
        wi_bf[...] = wi_ref[0].astype(BF16)
        wo_bf[...] = wo_ref[0].astype(BF16)

    @pl.when(valid_ref[i] == 1)
    def _():
        for j in range(PIECES_PER_STEP):
            piece_copy(slot, j, 0).wait()
        x = xbuf[slot, :, 0:d]
        meta = xbuf[slot, :, d:d + LANES].astype(F32)
        lane = lax.broadcasted_iota(jnp.int32, meta.shape, 1)

        def pick(*lanes):
            sel = lane == lanes[0]
            for ln in lanes[1:]:
                sel = sel | (lane == ln)
            return jnp.sum(jnp.where(sel, meta, 0.0), axis=-1, keepdims=True)

        is_first = pick(META_E1) == expert.astype(F32)
        gate = jnp.where(is_first, pick(META_W1_HI, META_W1_LO), pick(META_W2_HI, META_W2_LO))
        au = _dot(x, wi_bf[...])
        hidden = (_silu(au[:, :D_EXPERT]) * au[:, D_EXPERT:] * gate).astype(BF16)
        ys_ref[...] = _dot(hidden, wo_bf[...]).astype(BF16)

    @pl.when(valid_ref[i] == 0)
    def _():
        ys_ref[...] = jnp.zeros_like(ys_ref)


def _experts(xs, step_expert, step_valid, piece_src, wi, wo):
    n_steps = step_expert.shape[0]
    width = xs.shape[1]
    d = wo.shape[2]
    return pl.pallas_call(
        _experts_kernel,
        grid_spec=pltpu.PrefetchScalarGridSpec(
            num_scalar_prefetch=3,
            grid=(n_steps,),
            in_specs=[pl.BlockSpec(memory_space=pl.ANY),
                      pl.BlockSpec((1,) + wi.shape[1:], lambda i, e, v, s: (e[i], 0, 0)),
                      pl.BlockSpec((1,) + wo.shape[1:], lambda i, e, v, s: (e[i], 0, 0))],
            out_specs=pl.BlockSpec((EXPERT_TM, d), lambda i, e, v, s: (i, 0)),
            scratch_shapes=[pltpu.VMEM((GATHER_SLOTS, EXPERT_TM, width), BF16), pltpu.VMEM(wi.shape[1:], BF16),
                            pltpu.VMEM(wo.shape[1:], BF16), pltpu.SemaphoreType.DMA((GATHER_SLOTS,))]),
        out_shape=jax.ShapeDtypeStruct((n_steps * EXPERT_TM, d), BF16),
        compiler_params=_params("arbitrary"),
        name="moe_experts",
    )(step_expert, step_valid, piece_src, xs, wi, wo)


def _combine_kernel(src_ref, ys_hbm, info_ref, x1_ref, gt_ref, gf_ref, o_ref, stage, sem):
    i = pl.program_id(0)
    n_steps = pl.num_programs(0)
    tm = SORT_TM

    def piece_copy(s, src_row, dst_row):
        return pltpu.make_async_copy(ys_hbm.at[pl.ds(src_row, PIECE), :], stage.at[s, pl.ds(dst_row, PIECE), :],
                                     sem.at[s])

    def fetch(tile, s):
        for m in range(SLOT_PIECES):
            piece_copy(s, pl.multiple_of(src_ref[tile * SLOT_PIECES + m], PIECE), m * PIECE).start(priority=m % 2)

    def wait(s):
        for m in range(SLOT_PIECES):
            piece_copy(s, 0, m * PIECE).wait()

    @pl.when(i == 0)
    def _():
        fetch(0, 0)

    slot = i % 2
    fetch(jnp.minimum(i + 1, n_steps - 1), 1 - slot)
    wait(slot)
    ys = stage[slot]
    info = info_ref[...]
    lane = lax.broadcasted_iota(jnp.int32, info.shape, 1)
    dest1 = jnp.sum(jnp.where(lane == INFO_DEST1, info, 0.0), axis=-1, keepdims=True).astype(jnp.int32)
    dest2 = jnp.sum(jnp.where(lane == INFO_DEST2, info, 0.0), axis=-1, keepdims=True).astype(jnp.int32)
    slot_id = lax.broadcasted_iota(jnp.int32, (tm, SLOT_ROWS), 1)
    perm = jnp.where((slot_id == dest1) | (slot_id == dest2), 1.0, 0.0).astype(BF16)
    y = _dot(perm, ys)
    x2 = x1_ref[...] + gt_ref[0] * y
    o_ref[...] = _rms(x2) * gf_ref[...]

    @pl.when(i == n_steps - 1)
    def _():
        wait(1 - slot)


def _combine(ys, piece_src, info, x1, gt_m, gf, tiles_per_batch):
    t, d = x1.shape
    n_tiles = t // SORT_TM
    tok = lambda i, *_: (i, 0)
    return pl.pallas_call(
        _combine_kernel,
        grid_spec=pltpu.PrefetchScalarGridSpec(
            num_scalar_prefetch=1,
            grid=(n_tiles,),
            in_specs=[pl.BlockSpec(memory_space=pl.ANY),
                      pl.BlockSpec((SORT_TM, LANES), tok),
                      pl.BlockSpec((SORT_TM, d), tok),
                      pl.BlockSpec((1, 1, d), lambda i, *_: (i // tiles_per_batch, 0, 0)),
                      pl.BlockSpec((1, d), lambda i, *_: (0, 0))],
            out_specs=pl.BlockSpec((SORT_TM, d), tok),
            scratch_shapes=[pltpu.VMEM((2, SLOT_ROWS, d), BF16), pltpu.SemaphoreType.DMA((2,))]),
        out_shape=jax.ShapeDtypeStruct((t, d), F32),
        compiler_params=_params("arbitrary"),
        name="moe_combine",
    )(piece_src, ys, info, x1, gt_m, gf)


def _sparse_moe(xs, info, counts, x1, wi, wo, gt_m, gf):
    b, l, d = x1.shape
    t = b * l
    n_tiles = t // SORT_TM
    i32 = jnp.int32

    cnt = counts[:, :, 0]
    run_pieces = (cnt + (PIECE - 1)) // PIECE
    run_slot = (jnp.cumsum(run_pieces, axis=1) - run_pieces) * PIECE
    before = jnp.cumsum(run_pieces, axis=0) - run_pieces
    expert_pieces = jnp.sum(run_pieces, axis=0)
    steps_e = (expert_pieces + PIECES_PER_STEP - 1) // PIECES_PER_STEP
    step_end = jnp.cumsum(steps_e)
    region = (step_end - steps_e) * PIECES_PER_STEP
    run_pos = (region[None, :] + before) * PIECE

    n_steps = _expert_steps(t)
    step = jnp.arange(n_steps, dtype=i32)
    step_expert = jnp.minimum(jnp.sum((step[:, None] >= step_end[None, :]).astype(i32), axis=1), N_EXPERTS - 1)
    step_valid = (step < step_end[-1]).astype(i32)
    lookup = lambda onehot, table: jnp.dot(onehot, table.astype(F32), precision=lax.Precision.HIGHEST)
    piece = jnp.arange(n_steps * PIECES_PER_STEP, dtype=i32)
    is_expert = (jnp.repeat(step_expert, PIECES_PER_STEP)[:, None] == jnp.arange(N_EXPERTS, dtype=i32)).astype(F32)
    k_local = piece.astype(F32) - lookup(is_expert, region)
    ends = lookup(is_expert, jnp.cumsum(run_pieces, axis=0).T)
    tile_of = jnp.minimum(jnp.sum((ends <= k_local[:, None]).astype(i32), axis=1), n_tiles - 1)
    is_tile = (tile_of[:, None] == jnp.arange(n_tiles, dtype=i32)).astype(F32)
    j_in_run = k_local - jnp.sum(is_tile * lookup(is_expert, before.T), axis=1)
    slot_start = jnp.sum(is_tile * lookup(is_expert, run_slot.T), axis=1)
    real = (k_local < lookup(is_expert, expert_pieces)) & (jnp.repeat(step_valid, PIECES_PER_STEP) == 1)
    zero_piece = SLOT_ROWS - PIECE
    piece_src = jnp.where(real, tile_of * SLOT_ROWS + (slot_start + j_in_run * PIECE).astype(i32), zero_piece)


    m = jnp.arange(SLOT_PIECES, dtype=i32)
    run_end = jnp.cumsum(run_pieces, axis=1)
    run_of = jnp.minimum(jnp.sum((run_end[:, None, :] <= m[None, :, None]).astype(i32), axis=-1), N_EXPERTS - 1)
    is_run = run_of[:, :, None] == jnp.arange(N_EXPERTS, dtype=i32)
    pos_m = jnp.sum(jnp.where(is_run, run_pos[:, None, :], 0), axis=-1)
    first_m = jnp.sum(jnp.where(is_run, (run_end - run_pieces)[:, None, :], 0), axis=-1)
    zero_row = (n_steps - 1) * EXPERT_TM
    back_src = jnp.where(m[None, :] < run_end[:, -1:], pos_m + (m[None, :] - first_m) * PIECE, zero_row)

    flat = lambda a: a.reshape(-1).astype(i32)
    ys = _experts(xs.reshape(n_tiles * SLOT_ROWS, d + LANES), step_expert, step_valid, flat(piece_src), wi, wo)
    out = _combine(ys, flat(back_src), info.reshape(t, LANES), x1.reshape(t, d), gt_m, gf, l // SORT_TM)
    return out.reshape(b, l, d)


def kernel(x, c, ctx, c_ctx, w_ada, b_ada, norm1_g, w_in, w_decay, b_decay, gla_norm_g, w_pool, pool_scale, w_out,
           norm2_g, w_router_group, w_router_expert, w_expert_in, w_expert_out, final_norm_g):
    assert w_ada.shape[0] == 1, "single-layer trunk"
    b, l, d = x.shape
    off_a = 2 * QK_W + 2 * V_W
    a_cols = 2 * GATE_RANK
    off_p = off_a + a_cols

    c8 = jnp.zeros((8, d), F32).at[:b].set(c).at[b].set(c_ctx)
    mod = _ada_mod(c8, w_ada[0], b_ada[0])
    sh_a, sc_a, gt_a, sh_m, sc_m, gt_m = [m.reshape(8, 1, d) for m in jnp.split(mod, 6, axis=-1)]
    ctx_rows = lambda m: jnp.broadcast_to(m[b:b + 1], (b, 1, d))

    w = w_in[0]
    w_r = jnp.concatenate([w[:, :off_a], w[:, off_p:off_p + POOL_W], w[:, off_a:off_p],
                           jnp.zeros((d, LANES - a_cols), F32)], axis=1).astype(BF16)
    wdec = jnp.zeros((LANES, 2 * QK_W), F32)
    wdec = wdec.at[:GATE_RANK, :QK_W].set(w_decay[0, 0]).at[GATE_RANK:a_cols, QK_W:].set(w_decay[0, 1]).astype(BF16)
    bdec = b_decay[0].reshape(1, 2 * QK_W)
    g1 = norm1_g[0].reshape(1, d)

    q_c, k_c, v_c, _, la_c, _ = _inproj(ctx, ctx_rows(sh_a), ctx_rows(sc_a), g1, w_r, wdec, bdec, tm=ctx.shape[1])
    zero_s = jnp.zeros((b, QK_W, GLA_DV), F32)
    _, _, s_f, s_b = _gla(q_c, k_c, v_c, la_c, zero_s, zero_s, tb=ctx.shape[1])

    q, k, v, g, la, p = _inproj(x, sh_a[:b], sc_a[:b], g1, w_r, wdec, bdec, tm=1024)
    o_f, o_b, _, _ = _gla(q, k, v, la, s_f, s_b, tb=1024)
    pooled = _pool(p, w_pool[0], pool_scale[0])

    w_router = jnp.zeros((ROUTER_ROWS, d), F32)
    w_router = w_router.at[:N_EXPERTS].set(w_router_expert[0].T)
    w_router = w_router.at[ROUTER_GROUP_ROW0:ROUTER_GROUP_ROW0 + N_GROUPS].set(w_router_group[0].T)
    wr_hi = w_router.astype(BF16)
    wr_lo = (w_router - wr_hi.astype(F32)).astype(BF16)
    x1, xs, info, counts = _outproj(o_f, o_b, g, pooled, x, gla_norm_g[0].reshape(1, GLA_DV), w_out[0].astype(BF16),
                                    gt_a[:b], sh_m[:b], sc_m[:b], norm2_g[0].reshape(1, d), wr_hi, wr_lo,
                                    tm=SORT_TM)

    return _sparse_moe(xs, info, counts, x1, w_expert_in[0], w_expert_out[0], gt_m[:b], final_norm_g.reshape(1, d))
```

```python
import functools

import jax
import jax.numpy as jnp
from jax import lax
from jax.experimental import pallas as pl
from jax.experimental.pallas import tpu as pltpu

F32 = jnp.float32
BF16 = jnp.bfloat16

GRID_W = 64
GRID_W_LOG2 = 6
assert 1 << GRID_W_LOG2 == GRID_W
GLA_HEADS = 4
GLA_DK = 64
GLA_DV = 128
GATE_RANK = 16
GATE_NORMALIZER = 16.0
CHUNK = 64
POOL_WINDOWS = (2, 4, 8, 16)
POOL_GROUP = 128
N_GROUPS = 4
EXPERTS_PER_GROUP = 8
N_EXPERTS = N_GROUPS * EXPERTS_PER_GROUP
D_EXPERT = 256
EPS = 1e-6

QK_W = GLA_HEADS * GLA_DK
V_W = GLA_HEADS * GLA_DV
POOL_W = POOL_GROUP * len(POOL_WINDOWS)
LANES = 128
VMEM_LIMIT = 48 * 1024 * 1024

_NT = (((1,), (1,)), ((), ()))


def _dot(a, b):
    return jnp.dot(a, b, preferred_element_type=F32)


def _split_bf16(x):
    hi = x.astype(BF16)
    lo = (x - hi.astype(F32)).astype(BF16)
    return hi, lo


def _rms(x):
    return x * lax.rsqrt(jnp.mean(x * x, axis=-1, keepdims=True) + EPS)


def _silu(x):
    return x / (1.0 + jnp.exp(-x))


def _params(*sem):
    return pltpu.CompilerParams(dimension_semantics=sem, vmem_limit_bytes=VMEM_LIMIT)


def _ada_kernel(c_ref, w_ref, b_ref, o_ref):
    s = _silu(c_ref[...]).astype(BF16)
    o_ref[...] = _dot(s, w_ref[...].astype(BF16)) + b_ref[...]


def _ada_mod(c8, w_ada, b_ada):
    rows, d = c8.shape
    n = w_ada.shape[1]
    tn = 1024
    return pl.pallas_call(
        _ada_kernel,
        grid=(n // tn,),
        in_specs=[pl.BlockSpec((rows, d), lambda j: (0, 0)),
                  pl.BlockSpec((d, tn), lambda j: (0, j)),
                  pl.BlockSpec((1, tn), lambda j: (0, j))],
        out_specs=pl.BlockSpec((rows, tn), lambda j: (0, j)),
        out_shape=jax.ShapeDtypeStruct((rows, n), F32),
        compiler_params=_params("arbitrary"),
        name="ada_mod",
    )(c8, w_ada, b_ada.reshape(1, n))


def _inproj_kernel(x_ref, sh_ref, sc_ref, g1_ref, w_ref, wdec_ref, bdec_ref,
                   q_ref, k_ref, v_ref, g_ref, la_ref, p_ref):
    x = x_ref[0]
    h = _rms(x) * g1_ref[...]
    h = h * (1.0 + sc_ref[0]) + sh_ref[0]
    hb = h.astype(BF16)
    o = 0
    q_ref[0] = (_dot(hb, w_ref[:, o:o + QK_W]) * (GLA_DK ** -0.5)).astype(BF16)
    o += QK_W
    k_ref[0] = _dot(hb, w_ref[:, o:o + QK_W]).astype(BF16)
    o += QK_W
    v_ref[0] = _dot(hb, w_ref[:, o:o + V_W]).astype(BF16)
    o += V_W
    g_ref[0] = _dot(hb, w_ref[:, o:o + V_W]).astype(BF16)
    o += V_W
    p_ref[0] = _dot(hb, w_ref[:, o:o + POOL_W])
    o += POOL_W
    a_low = _dot(hb, w_ref[:, o:o + LANES])
    z = _dot(a_low.astype(BF16), wdec_ref[...]) + bdec_ref[...]
    log_sig = jnp.minimum(z, 0.0) - jnp.log(1.0 + jnp.exp(-jnp.abs(z)))
    la_ref[0] = log_sig / GATE_NORMALIZER


def _inproj(x, shift, scale, g1, w_r, wdec, bdec, tm):
    b, l, d = x.shape
    wcols = w_r.shape[1]
    tok = lambda bi, i: (bi, i, 0)
    vec = lambda bi, i: (bi, 0, 0)
    fixed = lambda bi, i: (0, 0)
    outs = [(QK_W, BF16), (QK_W, BF16), (V_W, BF16), (V_W, BF16), (2 * QK_W, F32), (POOL_W, F32)]
    return pl.pallas_call(
        _inproj_kernel,
        grid=(b, l // tm),
        in_specs=[pl.BlockSpec((1, tm, d), tok),
                  pl.BlockSpec((1, 1, d), vec),
                  pl.BlockSpec((1, 1, d), vec),
                  pl.BlockSpec((1, d), fixed),
                  pl.BlockSpec((d, wcols), fixed),
                  pl.BlockSpec(wdec.shape, fixed),
                  pl.BlockSpec(bdec.shape, fixed)],
        out_specs=[pl.BlockSpec((1, tm, w), tok) for w, _ in outs],
        out_shape=[jax.ShapeDtypeStruct((b, l, w), dt) for w, dt in outs],
        compiler_params=_params("arbitrary", "arbitrary"),
        name="inproj",
    )(x, shift, scale, g1, w_r, wdec, bdec)


GLA_GROUP = 256
CHUNKS_PER_GROUP = GLA_GROUP // CHUNK
CHUNK_LOG2 = 6
assert 1 << CHUNK_LOG2 == CHUNK


def _head_masks(rows):
    lane = lax.broadcasted_iota(jnp.int32, (rows, QK_W), 1)
    return [(lane >= h * GLA_DK) & (lane < (h + 1) * GLA_DK) for h in range(GLA_HEADS)]


def _gla_bulk(items, qd_scr, oi_scr, ds_scr, dc_scr):
    n = GLA_GROUP
    i0 = lax.broadcasted_iota(jnp.int32, (n, n), 0)
    i1 = lax.broadcasted_iota(jnp.int32, (n, n), 1)
    same_chunk = (i0 >> CHUNK_LOG2) == (i1 >> CHUNK_LOG2)
    causal = {False: same_chunk & (i1 <= i0), True: same_chunk & (i1 >= i0)}
    tri = {rev: jnp.where(m, 1.0, 0.0).astype(BF16) for rev, m in causal.items()}
    masks = _head_masks(n)
    rows = [pl.multiple_of(g * n, n) for _, _, g, *_ in items]

    cum = []
    for (d, rev, g, q_ref, k_ref, v_ref, la_ref), r in zip(items, rows):
        la_hi, la_lo = _split_bf16(la_ref[0, pl.ds(r, n), :])
        cum.append(_dot(tri[rev], la_hi) + _dot(tri[rev], la_lo))

    decayed = []
    for (d, rev, g, q_ref, k_ref, v_ref, la_ref), r, b in zip(items, rows, cum):
        last_row = 0 if rev else CHUNK - 1
        b_end = b.reshape(CHUNKS_PER_GROUP, CHUNK, QK_W)[:, last_row:last_row + 1, :]
        b_last = jnp.broadcast_to(b_end, (CHUNKS_PER_GROUP, CHUNK, QK_W)).reshape(n, QK_W)
        k = k_ref[0, pl.ds(r, n), :].astype(F32)
        qd = (q_ref[0, pl.ds(r, n), :].astype(F32) * jnp.exp(b)).astype(BF16)
        qd_scr[d, pl.ds(r, n), :] = qd
        decayed.append((qd, (k * jnp.exp(-b)).astype(BF16), k * jnp.exp(b_last - b), jnp.exp(b_end)))

    scores = [[lax.dot_general(jnp.where(m, qd, jnp.zeros_like(qd)), kd, _NT, preferred_element_type=F32)
               for m in masks] for qd, kd, _, _ in decayed]
    for (d, rev, g, q_ref, k_ref, v_ref, la_ref), r, sc in zip(items, rows, scores):
        for h in range(GLA_HEADS):
            cols = slice(h * GLA_DV, (h + 1) * GLA_DV)
            probs = jnp.where(causal[rev], sc[h], 0.0).astype(BF16)
            oi_scr[d, pl.ds(r, n), cols] = _dot(probs, v_ref[0, pl.ds(r, n), cols])

    for (d, rev, g, q_ref, k_ref, v_ref, la_ref), r, (_, _, k_st, decay) in zip(items, rows, decayed):
        for c in range(CHUNKS_PER_GROUP):
            ci = g * CHUNKS_PER_GROUP + c
            k_st_t = k_st[c * CHUNK:(c + 1) * CHUNK].T.astype(BF16)
            for h in range(GLA_HEADS):
                v_ch = v_ref[0, pl.ds(r + c * CHUNK, CHUNK), h * GLA_DV:(h + 1) * GLA_DV]
                ds_scr[d, ci, h * GLA_DK:(h + 1) * GLA_DK, :] = _dot(k_st_t[h * GLA_DK:(h + 1) * GLA_DK], v_ch)
            dc_scr[d, ci] = jnp.broadcast_to(decay[c], (LANES, QK_W)).T


def _gla_kernel(qf_ref, kf_ref, vf_ref, laf_ref, qb_ref, kb_ref, vb_ref, lab_ref, s0f_ref, s0b_ref,
                of_ref, ob_ref, sf_out_ref, sb_out_ref, s_ref, qd_scr, oi_scr, ds_scr, dc_scr, st_scr, *, n_chunks):
    j = pl.program_id(1)

    @pl.when(j == 0)
    def _():
        s_ref[0] = s0f_ref[0]
        s_ref[1] = s0b_ref[0]

    n_groups = n_chunks // CHUNKS_PER_GROUP
    unroll = 2 if n_groups % 2 == 0 else 1

    def bulk(it, carry):
        items = []
        for u in range(unroll):
            g = it * unroll + u
            items.append((0, False, g, qf_ref, kf_ref, vf_ref, laf_ref))
            items.append((1, True, g, qb_ref, kb_ref, vb_ref, lab_ref))
        _gla_bulk(items, qd_scr, oi_scr, ds_scr, dc_scr)
        return carry
    lax.fori_loop(0, n_groups // unroll, bulk, 0)

    def recur(i, carry):
        for d, ci in ((0, i), (1, n_chunks - 1 - i)):
            s = s_ref[d]
            st_scr[d, ci] = s.astype(BF16)
            s_ref[d] = dc_scr[d, ci] * s + ds_scr[d, ci]
        return carry
    lax.fori_loop(0, n_chunks, recur, 0)

    masks = _head_masks(CHUNK)

    def inter(it, carry):
        work = [(d, o_ref, (it * unroll + u) * CHUNKS_PER_GROUP + c)
                for u in range(unroll) for d, o_ref in ((0, of_ref), (1, ob_ref)) for c in range(CHUNKS_PER_GROUP)]
        from_state = []
        for d, o_ref, ci in work:
            qd = qd_scr[d, pl.ds(pl.multiple_of(ci * CHUNK, CHUNK), CHUNK), :]
            lhs = jnp.concatenate([jnp.where(m, qd, jnp.zeros_like(qd)) for m in masks], axis=0)
            from_state.append(_dot(lhs, st_scr[d, ci]))
        for (d, o_ref, ci), fs in zip(work, from_state):
            r = pl.multiple_of(ci * CHUNK, CHUNK)
            for h in range(GLA_HEADS):
                cols = slice(h * GLA_DV, (h + 1) * GLA_DV)
                o_ref[0, pl.ds(r, CHUNK), cols] = (oi_scr[d, pl.ds(r, CHUNK), cols]
                                                   + fs[h * CHUNK:(h + 1) * CHUNK]).astype(BF16)
        return carry
    lax.fori_loop(0, n_groups // unroll, inter, 0)

    @pl.when(j == pl.num_programs(1) - 1)
    def _():
        sf_out_ref[0] = s_ref[0]
        sb_out_ref[0] = s_ref[1]


def _gla(q, k, v, la, s0f, s0b, tb):
    b, l, _ = q.shape
    assert tb % GLA_GROUP == 0 and l % tb == 0
    nb = l // tb
    n_chunks = tb // CHUNK
    fwd = lambda bi, j: (bi, j, 0)
    bwd = lambda bi, j: (bi, nb - 1 - j, 0)
    fwd_la = lambda bi, j: (bi, j, 0)
    bwd_la = lambda bi, j: (bi, nb - 1 - j, 1)
    st = lambda bi, j: (bi, 0, 0)
    state_shape = (b, QK_W, GLA_DV)
    return pl.pallas_call(
        functools.partial(_gla_kernel, n_chunks=tb // CHUNK),
        grid=(b, nb),
        in_specs=[pl.BlockSpec((1, tb, QK_W), fwd), pl.BlockSpec((1, tb, QK_W), fwd),
                  pl.BlockSpec((1, tb, V_W), fwd), pl.BlockSpec((1, tb, QK_W), fwd_la),
                  pl.BlockSpec((1, tb, QK_W), bwd), pl.BlockSpec((1, tb, QK_W), bwd),
                  pl.BlockSpec((1, tb, V_W), bwd), pl.BlockSpec((1, tb, QK_W), bwd_la),
                  pl.BlockSpec((1, QK_W, GLA_DV), st), pl.BlockSpec((1, QK_W, GLA_DV), st)],
        out_specs=[pl.BlockSpec((1, tb, V_W), fwd), pl.BlockSpec((1, tb, V_W), bwd),
                   pl.BlockSpec((1, QK_W, GLA_DV), st), pl.BlockSpec((1, QK_W, GLA_DV), st)],
        out_shape=[jax.ShapeDtypeStruct((b, l, V_W), BF16), jax.ShapeDtypeStruct((b, l, V_W), BF16),
                   jax.ShapeDtypeStruct(state_shape, F32), jax.ShapeDtypeStruct(state_shape, F32)],
        scratch_shapes=[pltpu.VMEM((2, QK_W, GLA_DV), F32),
                        pltpu.VMEM((2, tb, QK_W), BF16),
                        pltpu.VMEM((2, tb, V_W), F32),
                        pltpu.VMEM((2, n_chunks, QK_W, GLA_DV), F32),
                        pltpu.VMEM((2, n_chunks, QK_W, GLA_DV), F32),
                        pltpu.VMEM((2, n_chunks, QK_W, GLA_DV), BF16)],
        compiler_params=_params("arbitrary", "arbitrary"),
        name="gla",
    )(q, k, v, la, q, k, v, la, s0f, s0b)


POOL_BLK = 256
POOL_COL_STEP = 1024
POOL_ROW_STEP = 256
POOL_OUT_STEP = 1024


def _pool_body(w, first_batch, p_ref, wp_ref, ps_ref, o_ref, buf_ref, inv_ref, y_ref, n_tok):
    lo = w // 2
    hi = w - 1 - lo
    pad = lo * GRID_W
    n_rows = n_tok // GRID_W
    blk = POOL_BLK

    @pl.when(first_batch)
    def _():
        zeros = jnp.zeros((GRID_W * 8, LANES), F32)
        buf_ref[0:pad, :] = zeros[0:pad]
        buf_ref[pad + n_tok:pad + n_tok + hi * GRID_W + GRID_W, :] = zeros[0:hi * GRID_W + GRID_W]

        def inv_step(t, carry):
            r = pl.multiple_of(t * POOL_OUT_STEP, POOL_OUT_STEP)
            tok = r + lax.broadcasted_iota(jnp.int32, (POOL_OUT_STEP, LANES), 0)
            g_row = tok >> GRID_W_LOG2
            g_col = tok & (GRID_W - 1)
            cnt_c = jnp.minimum(g_col + hi, GRID_W - 1) - jnp.maximum(g_col - lo, 0) + 1
            cnt_r = jnp.minimum(g_row + hi, n_rows - 1) - jnp.maximum(g_row - lo, 0) + 1
            inv_ref[pl.ds(r, POOL_OUT_STEP), :] = 1.0 / (cnt_c * cnt_r).astype(F32)
            return carry
        lax.fori_loop(0, n_tok // POOL_OUT_STEP, inv_step, 0)

    i0 = lax.broadcasted_iota(jnp.int32, (blk, blk), 0)
    i1 = lax.broadcasted_iota(jnp.int32, (blk, blk), 1)
    same_row = (i0 >> GRID_W_LOG2) == (i1 >> GRID_W_LOG2)
    band = jnp.where(same_row & (i1 - i0 >= -lo) & (i1 - i0 <= hi), 1.0, 0.0).astype(BF16)

    def col_step(t, carry):
        rows = [pl.multiple_of(t * POOL_COL_STEP + u * blk, blk) for u in range(POOL_COL_STEP // blk)]
        sums = []
        for r in rows:
            sums.append(_dot(band, jnp.concatenate(_split_bf16(p_ref[0, pl.ds(r, blk), :]), axis=-1)))
        for r, s in zip(rows, sums):
            buf_ref[pl.ds(pad + r, blk), :] = s[:, :LANES] + s[:, LANES:]
        return carry
    lax.fori_loop(0, n_tok // POOL_COL_STEP, col_step, 0)

    def row_step(t, carry):
        r = pl.multiple_of(t * POOL_ROW_STEP, POOL_ROW_STEP)
        acc = buf_ref[pl.ds(r, POOL_ROW_STEP), :]
        for d in range(1, w):
            acc = acc + buf_ref[pl.ds(r + d * GRID_W, POOL_ROW_STEP), :]
        m = acc * inv_ref[pl.ds(r, POOL_ROW_STEP), :]
        y_ref[pl.ds(r, POOL_ROW_STEP), :] = (m - p_ref[0, pl.ds(r, POOL_ROW_STEP), :]).astype(BF16)
        return carry
    lax.fori_loop(0, n_tok // POOL_ROW_STEP, row_step, 0)

    wp = wp_ref[0].astype(BF16)
    scale = ps_ref[0]

    def out_step(t, carry):
        r = pl.multiple_of(t * POOL_OUT_STEP, POOL_OUT_STEP)
        o_ref[0, pl.ds(r, POOL_OUT_STEP), :] = (_dot(y_ref[pl.ds(r, POOL_OUT_STEP), :], wp) * scale).astype(BF16)
        return carry
    lax.fori_loop(0, n_tok // POOL_OUT_STEP, out_step, 0)


def _pool_kernel(p_ref, wp_ref, ps_ref, o_ref, buf_ref, inv_ref, y_ref, *, n_tok):
    g = pl.program_id(0)
    first_batch = pl.program_id(1) == 0
    for gi, w in enumerate(POOL_WINDOWS):
        @pl.when(g == gi)
        def _(w=w):
            _pool_body(w, first_batch, p_ref, wp_ref, ps_ref, o_ref, buf_ref, inv_ref, y_ref, n_tok)


def _pool(p, w_pool, pool_scale):
    b, l, _ = p.shape
    n_g = len(POOL_WINDOWS)
    max_w = max(POOL_WINDOWS)
    buf_rows = l + (max_w + 1) * GRID_W
    return pl.pallas_call(
        functools.partial(_pool_kernel, n_tok=l),
        grid=(n_g, b),
        in_specs=[pl.BlockSpec((1, l, POOL_GROUP), lambda g, bi: (bi, 0, g)),
                  pl.BlockSpec((1, POOL_GROUP, POOL_GROUP), lambda g, bi: (g, 0, 0)),
                  pl.BlockSpec((1, 1, POOL_GROUP), lambda g, bi: (g, 0, 0))],
        out_specs=pl.BlockSpec((1, l, POOL_GROUP), lambda g, bi: (bi, 0, g)),
        out_shape=jax.ShapeDtypeStruct((b, l, POOL_W), BF16),
        scratch_shapes=[pltpu.VMEM((buf_rows, LANES), F32), pltpu.VMEM((l, LANES), F32),
                        pltpu.VMEM((l, LANES), BF16)],
        compiler_params=_params("arbitrary", "arbitrary"),
        name="pool",
    )(p, w_pool, pool_scale.reshape(n_g, 1, POOL_GROUP))


ROUTER_ROWS = 48
ROUTER_GROUP_ROW0 = N_EXPERTS
TOP_K = 2
SORT_TM = 512
PIECE = 16
SLOT_ROWS = TOP_K * SORT_TM + N_EXPERTS * PIECE
META_W1_HI, META_W1_LO, META_W2_HI, META_W2_LO, META_E1, META_E2 = range(6)
INFO_DEST1, INFO_DEST2 = 6, 7


def _outproj_kernel(of_ref, ob_ref, g_ref, pooled_ref, x_ref, gn_ref, wout_ref, gt_ref, sh_ref, sc_ref, g2_ref,
                    wr_hi_ref, wr_lo_ref, x1_ref, xs_ref, info_ref, cnt_ref):
    o = of_ref[0].astype(F32) + ob_ref[0].astype(F32)
    gate = _silu(g_ref[0].astype(F32))
    parts = []
    for h in range(GLA_HEADS):
        sl = slice(h * GLA_DV, (h + 1) * GLA_DV)
        parts.append((_rms(o[:, sl]) * gn_ref[...] * gate[:, sl]).astype(BF16))
    o_n = jnp.concatenate(parts, axis=-1)
    out = _dot(o_n, wout_ref[0:V_W, :]) + _dot(pooled_ref[0], wout_ref[V_W:V_W + POOL_W, :])
    x1 = x_ref[0] + gt_ref[0] * out
    x1_ref[0] = x1.astype(BF16)
    h2 = _rms(x1) * g2_ref[...]
    h2 = h2 * (1.0 + sc_ref[0]) + sh_ref[0]
    h2_bf = h2.astype(BF16)
    h2_lo = (h2 - h2_bf.astype(F32)).astype(BF16)
    logits = (lax.dot_general(wr_hi_ref[...], h2_bf, _NT, preferred_element_type=F32)
              + lax.dot_general(wr_hi_ref[...], h2_lo, _NT, preferred_element_type=F32)
              + lax.dot_general(wr_lo_ref[...], h2_bf, _NT, preferred_element_type=F32))
    tm = logits.shape[1]
    row = lax.broadcasted_iota(jnp.int32, logits.shape, 0)
    neg = -jnp.inf
    big = jnp.int32(1 << 20)
    is_group = (row >= ROUTER_GROUP_ROW0) & (row < ROUTER_GROUP_ROW0 + N_GROUPS)
    gl = jnp.where(is_group, logits, neg)
    g_max = jnp.max(gl, axis=0, keepdims=True)
    g_idx = jnp.min(jnp.where(gl == g_max, row - ROUTER_GROUP_ROW0, big), axis=0, keepdims=True)
    g_prob = 1.0 / jnp.sum(jnp.where(is_group, jnp.exp(gl - g_max), 0.0), axis=0, keepdims=True)
    n_e = EXPERTS_PER_GROUP
    el = logits[0:n_e]
    for g in range(1, N_GROUPS):
        el = jnp.where(g_idx == g, logits[g * n_e:(g + 1) * n_e], el)
    r8 = lax.broadcasted_iota(jnp.int32, el.shape, 0)
    m1 = jnp.max(el, axis=0, keepdims=True)
    i1 = jnp.min(jnp.where(el == m1, r8, big), axis=0, keepdims=True)
    el2 = jnp.where(r8 == i1, neg, el)
    m2 = jnp.max(el2, axis=0, keepdims=True)
    i2 = jnp.min(jnp.where(el2 == m2, r8, big), axis=0, keepdims=True)
    r = jnp.exp(m2 - m1)
    w1 = g_prob / (1.0 + r)
    w2 = g_prob * r / (1.0 + r)

    e1 = g_idx * n_e + i1
    e2 = g_idx * n_e + i2
    r_e = lax.broadcasted_iota(jnp.int32, (N_EXPERTS, tm), 0)
    pick1 = r_e == e1
    pick2 = r_e == e2
    onehot = jnp.where(pick1 | pick2, 1.0, 0.0)
    t_r = lax.broadcasted_iota(jnp.int32, (tm, tm), 0)
    t_c = lax.broadcasted_iota(jnp.int32, (tm, tm), 1)
    rank = _dot(onehot.astype(BF16), jnp.where(t_r < t_c, 1.0, 0.0).astype(BF16))
    count = jnp.sum(onehot, axis=1, keepdims=True)
    n_pieces = jnp.floor((count + (PIECE - 1)) * (1.0 / PIECE))
    e_r = lax.broadcasted_iota(jnp.int32, (N_EXPERTS, N_EXPERTS), 0)
    e_c = lax.broadcasted_iota(jnp.int32, (N_EXPERTS, N_EXPERTS), 1)
    run_start = _dot(jnp.where(e_c < e_r, 1.0, 0.0).astype(BF16),
                     jnp.broadcast_to(n_pieces, (N_EXPERTS, LANES)).astype(BF16))[:, 0:1] * PIECE
    slot = rank + run_start
    dest1 = jnp.sum(jnp.where(pick1, slot, 0.0), axis=0, keepdims=True)
    dest2 = jnp.sum(jnp.where(pick2, slot, 0.0), axis=0, keepdims=True)
    w1_hi = w1.astype(BF16).astype(F32)
    w2_hi = w2.astype(BF16).astype(F32)
    info_rows = [w1_hi, (w1 - w1_hi).astype(BF16).astype(F32), w2_hi, (w2 - w2_hi).astype(BF16).astype(F32),
                 e1.astype(F32), e2.astype(F32), dest1, dest2]
    info_t = jnp.concatenate(info_rows + [jnp.zeros((LANES - len(info_rows), tm), F32)], axis=0)
    info = info_t.T
    info_ref[0] = info
    cnt_ref[0] = jnp.broadcast_to(count, (N_EXPERTS, LANES)).astype(jnp.int32)
    s_id = lax.broadcasted_iota(jnp.int32, (SLOT_ROWS, tm), 0)
    perm_t = jnp.where((s_id == dest1.astype(jnp.int32)) | (s_id == dest2.astype(jnp.int32)),
                       1.0, 0.0).astype(BF16)
    lane = lax.broadcasted_iota(jnp.int32, info.shape, 1)
    meta = jnp.where(lane < INFO_DEST1, info, 0.0).astype(BF16)
    d = h2_bf.shape[1]
    xs_ref[0, :, 0:d] = _dot(perm_t, h2_bf).astype(BF16)
    xs_ref[0, :, d:d + LANES] = _dot(perm_t, meta).astype(BF16)


def _outproj(o_f, o_b, g, pooled, x, gn, w_out, gt_a, sh_m, sc_m, g2, wr_hi, wr_lo, tm):
    b, l, d = x.shape
    per_batch = l // tm
    n_tiles = b * per_batch
    tok = lambda s: (s // per_batch, s % per_batch, 0)
    vec = lambda s: (s // per_batch, 0, 0)
    fixed = lambda s: (0, 0)
    return pl.pallas_call(
        _outproj_kernel,
        grid=(n_tiles,),
        in_specs=[pl.BlockSpec((1, tm, V_W), tok), pl.BlockSpec((1, tm, V_W), tok),
                  pl.BlockSpec((1, tm, V_W), tok), pl.BlockSpec((1, tm, POOL_W), tok),
                  pl.BlockSpec((1, tm, d), tok),
                  pl.BlockSpec((1, GLA_DV), fixed),
                  pl.BlockSpec(w_out.shape, fixed),
                  pl.BlockSpec((1, 1, d), vec), pl.BlockSpec((1, 1, d), vec), pl.BlockSpec((1, 1, d), vec),
                  pl.BlockSpec((1, d), fixed),
                  pl.BlockSpec(wr_hi.shape, fixed), pl.BlockSpec(wr_lo.shape, fixed)],
        out_specs=[pl.BlockSpec((1, tm, d), tok),
                   pl.BlockSpec((1, SLOT_ROWS, d + LANES), lambda s: (s, 0, 0)),
                   pl.BlockSpec((1, tm, LANES), tok),
                   pl.BlockSpec((1, N_EXPERTS, LANES), lambda s: (s, 0, 0))],
        out_shape=[jax.ShapeDtypeStruct((b, l, d), BF16),
                   jax.ShapeDtypeStruct((n_tiles, SLOT_ROWS, d + LANES), BF16),
                   jax.ShapeDtypeStruct((b, l, LANES), F32),
                   jax.ShapeDtypeStruct((n_tiles, N_EXPERTS, LANES), jnp.int32)],
        compiler_params=_params("arbitrary"),
        name="outproj_router",
    )(o_f, o_b, g, pooled, x, gn, w_out, gt_a, sh_m, sc_m, g2, wr_hi, wr_lo)


EXPERT_TM = 512
PIECES_PER_STEP = EXPERT_TM // PIECE
GATHER_SLOTS = 6
SLOT_PIECES = SLOT_ROWS // PIECE


def _expert_steps(n_tok):
    pieces = TOP_K * n_tok // PIECE + (n_tok // SORT_TM) * N_EXPERTS
    return -(-pieces // PIECES_PER_STEP) + N_EXPERTS


def _experts_kernel(exp_ref, valid_ref, src_ref, xs_hbm, wi_ref, wo_ref, ys_ref, xbuf, wi_bf, wo_bf, sem):
    i = pl.program_id(0)
    n_steps = pl.num_programs(0)
    slot = i % GATHER_SLOTS
    d = ys_ref.shape[1]

    def piece_copy(s, j, src_row):
        return pltpu.make_async_copy(xs_hbm.at[pl.ds(src_row, PIECE), :], xbuf.at[s, pl.ds(j * PIECE, PIECE), :],
                                     sem.at[s])

    def fetch(step):
        @pl.when((step < n_steps) & (valid_ref[jnp.minimum(step, n_steps - 1)] == 1))
        def _():
            for j in range(PIECES_PER_STEP):
                piece_copy(step % GATHER_SLOTS, j,
                           pl.multiple_of(src_ref[step * PIECES_PER_STEP + j], PIECE)).start(priority=j % 2)

    @pl.when(i == 0)
    def _():
        for ahead in range(GATHER_SLOTS - 1):
            fetch(jnp.int32(ahead))

    fetch(i + GATHER_SLOTS - 1)
    expert = exp_ref[i]

    @pl.when((i == 0) | (exp_ref[jnp.maximum(i - 1, 0)] != expert))
    def _():
        wi_bf[...] = wi_ref[0].astype(BF16)
        wo_bf[...] = wo_ref[0].astype(BF16)

    @pl.when(valid_ref[i] == 1)
    def _():
        for j in range(PIECES_PER_STEP):
            piece_copy(slot, j, 0).wait()
        x = xbuf[slot, :, 0:d]
        meta = xbuf[slot, :, d:d + LANES].astype(F32)
        lane = lax.broadcasted_iota(jnp.int32, meta.shape, 1)

        def pick(*lanes):
            sel = lane == lanes[0]
            for ln in lanes[1:]:
                sel = sel | (lane == ln)
            return jnp.sum(jnp.where(sel, meta, 0.0), axis=-1, keepdims=True)

        is_first = pick(META_E1) == expert.astype(F32)
        gate = jnp.where(is_first, pick(META_W1_HI, META_W1_LO), pick(META_W2_HI, META_W2_LO))
        au = _dot(x, wi_bf[...])
        hidden = (_silu(au[:, :D_EXPERT]) * au[:, D_EXPERT:] * gate).astype(BF16)
        ys_ref[...] = _dot(hidden, wo_bf[...]).astype(BF16)

    @pl.when(valid_ref[i] == 0)
    def _():
        ys_ref[...] = jnp.zeros_like(ys_ref)


def _experts(xs, step_expert, step_valid, piece_src, wi, wo):
    n_steps = step_expert.shape[0]
    width = xs.shape[1]
    d = wo.shape[2]
    return pl.pallas_call(
        _experts_kernel,
        grid_spec=pltpu.PrefetchScalarGridSpec(
            num_scalar_prefetch=3,
            grid=(n_steps,),
            in_specs=[pl.BlockSpec(memory_space=pl.ANY),
                      pl.BlockSpec((1,) + wi.shape[1:], lambda i, e, v, s: (e[i], 0, 0)),
                      pl.BlockSpec((1,) + wo.shape[1:], lambda i, e, v, s: (e[i], 0, 0))],
            out_specs=pl.BlockSpec((EXPERT_TM, d), lambda i, e, v, s: (i, 0)),
            scratch_shapes=[pltpu.VMEM((GATHER_SLOTS, EXPERT_TM, width), BF16), pltpu.VMEM(wi.shape[1:], BF16),
                            pltpu.VMEM(wo.shape[1:], BF16), pltpu.SemaphoreType.DMA((GATHER_SLOTS,))]),
        out_shape=jax.ShapeDtypeStruct((n_steps * EXPERT_TM, d), BF16),
        compiler_params=_params("arbitrary"),
        name="moe_experts",
    )(step_expert, step_valid, piece_src, xs, wi, wo)


def _combine_kernel(src_ref, n_ref, ys_hbm, info_ref, x1_ref, gt_ref, gf_ref, o_ref, stage, sem):
    i = pl.program_id(0)
    n_steps = pl.num_programs(0)
    tm = SORT_TM

    def piece_copy(s, src_row, m):
        return pltpu.make_async_copy(ys_hbm.at[pl.ds(src_row, PIECE), :],
                                     stage.at[s, pl.ds(pl.multiple_of(m * PIECE, PIECE), PIECE), :], sem.at[s])

    def fetch(tile, s):
        def body(m, carry):
            piece_copy(s, pl.multiple_of(src_ref[tile * SLOT_PIECES + m], PIECE), m).start()
            return carry
        lax.fori_loop(0, n_ref[tile], body, 0)

    @pl.when(i == 0)
    def _():
        fetch(0, 0)

    @pl.when(i + 1 < n_steps)
    def _():
        fetch(i + 1, (i + 1) % 2)

    slot = i % 2
    n_tile = n_ref[i]

    def wait_body(m, carry):
        piece_copy(slot, 0, m).wait()
        return carry
    lax.fori_loop(0, n_tile, wait_body, 0)

    def zero_body(m, carry):
        stage[slot, pl.ds(pl.multiple_of(m * PIECE, PIECE), PIECE), :] = jnp.zeros((PIECE, stage.shape[2]), BF16)
        return carry
    lax.fori_loop(n_tile, SLOT_PIECES, zero_body, 0)

    ys = stage[slot]
    info = info_ref[...]
    lane = lax.broadcasted_iota(jnp.int32, info.shape, 1)
    dest1 = jnp.sum(jnp.where(lane == INFO_DEST1, info, 0.0), axis=-1, keepdims=True).astype(jnp.int32)
    dest2 = jnp.sum(jnp.where(lane == INFO_DEST2, info, 0.0), axis=-1, keepdims=True).astype(jnp.int32)
    slot_id = lax.broadcasted_iota(jnp.int32, (tm, SLOT_ROWS), 1)
    perm = jnp.where((slot_id == dest1) | (slot_id == dest2), 1.0, 0.0).astype(BF16)
    y = _dot(perm, ys)
    x2 = x1_ref[...].astype(F32) + gt_ref[0] * y
    o_ref[...] = _rms(x2) * gf_ref[...]


def _combine(ys, piece_src, tile_pieces, info, x1, gt_m, gf, tiles_per_batch):
    t, d = x1.shape
    n_tiles = t // SORT_TM
    tok = lambda i, *_: (i, 0)
    return pl.pallas_call(
        _combine_kernel,
        grid_spec=pltpu.PrefetchScalarGridSpec(
            num_scalar_prefetch=2,
            grid=(n_tiles,),
            in_specs=[pl.BlockSpec(memory_space=pl.ANY),
                      pl.BlockSpec((SORT_TM, LANES), tok),
                      pl.BlockSpec((SORT_TM, d), tok),
                      pl.BlockSpec((1, 1, d), lambda i, *_: (i // tiles_per_batch, 0, 0)),
                      pl.BlockSpec((1, d), lambda i, *_: (0, 0))],
            out_specs=pl.BlockSpec((SORT_TM, d), tok),
            scratch_shapes=[pltpu.VMEM((2, SLOT_ROWS, d), BF16), pltpu.SemaphoreType.DMA((2,))]),
        out_shape=jax.ShapeDtypeStruct((t, d), F32),
        compiler_params=_params("arbitrary"),
        name="moe_combine",
    )(piece_src, tile_pieces, ys, info, x1, gt_m, gf)


def _sparse_moe(xs, info, counts, x1, wi, wo, gt_m, gf):
    b, l, d = x1.shape
    t = b * l
    n_tiles = t // SORT_TM
    i32 = jnp.int32

    cnt = counts[:, :, 0]
    run_pieces = (cnt + (PIECE - 1)) // PIECE
    run_slot = (jnp.cumsum(run_pieces, axis=1) - run_pieces) * PIECE
    before = jnp.cumsum(run_pieces, axis=0) - run_pieces
    expert_pieces = jnp.sum(run_pieces, axis=0)
    steps_e = (expert_pieces + PIECES_PER_STEP - 1) // PIECES_PER_STEP
    step_end = jnp.cumsum(steps_e)
    region = (step_end - steps_e) * PIECES_PER_STEP
    run_pos = (region[None, :] + before) * PIECE

    n_steps = _expert_steps(t)
    step = jnp.arange(n_steps, dtype=i32)
    step_expert = jnp.minimum(jnp.sum((step[:, None] >= step_end[None, :]).astype(i32), axis=1), N_EXPERTS - 1)
    step_valid = (step < step_end[-1]).astype(i32)
    lookup = lambda onehot, table: jnp.dot(onehot, table.astype(F32), precision=lax.Precision.HIGHEST)
    piece = jnp.arange(n_steps * PIECES_PER_STEP, dtype=i32)
    is_expert = (jnp.repeat(step_expert, PIECES_PER_STEP)[:, None] == jnp.arange(N_EXPERTS, dtype=i32)).astype(F32)
    k_local = piece.astype(F32) - lookup(is_expert, region)
    ends = lookup(is_expert, jnp.cumsum(run_pieces, axis=0).T)
    tile_of = jnp.minimum(jnp.sum((ends <= k_local[:, None]).astype(i32), axis=1), n_tiles - 1)
    is_tile = (tile_of[:, None] == jnp.arange(n_tiles, dtype=i32)).astype(F32)
    j_in_run = k_local - jnp.sum(is_tile * lookup(is_expert, before.T), axis=1)
    slot_start = jnp.sum(is_tile * lookup(is_expert, run_slot.T), axis=1)
    real = (k_local < lookup(is_expert, expert_pieces)) & (jnp.repeat(step_valid, PIECES_PER_STEP) == 1)
    zero_piece = SLOT_ROWS - PIECE
    piece_src = jnp.where(real, tile_of * SLOT_ROWS + (slot_start + j_in_run * PIECE).astype(i32), zero_piece)


    m = jnp.arange(SLOT_PIECES, dtype=i32)
    run_end = jnp.cumsum(run_pieces, axis=1)
    run_of = jnp.minimum(jnp.sum((run_end[:, None, :] <= m[None, :, None]).astype(i32), axis=-1), N_EXPERTS - 1)
    is_run = run_of[:, :, None] == jnp.arange(N_EXPERTS, dtype=i32)
    pos_m = jnp.sum(jnp.where(is_run, run_pos[:, None, :], 0), axis=-1)
    first_m = jnp.sum(jnp.where(is_run, (run_end - run_pieces)[:, None, :], 0), axis=-1)
    back_src = jnp.where(m[None, :] < run_end[:, -1:], pos_m + (m[None, :] - first_m) * PIECE, 0)

    flat = lambda a: a.reshape(-1).astype(i32)
    ys = _experts(xs.reshape(n_tiles * SLOT_ROWS, d + LANES), step_expert, step_valid, flat(piece_src), wi, wo)
    out = _combine(ys, flat(back_src), flat(run_end[:, -1]), info.reshape(t, LANES), x1.reshape(t, d), gt_m, gf,
                   l // SORT_TM)
    return out.reshape(b, l, d)


def kernel(x, c, ctx, c_ctx, w_ada, b_ada, norm1_g, w_in, w_decay, b_decay, gla_norm_g, w_pool, pool_scale, w_out,
           norm2_g, w_router_group, w_router_expert, w_expert_in, w_expert_out, final_norm_g):
    assert w_ada.shape[0] == 1, "single-layer trunk"
    b, l, d = x.shape
    off_a = 2 * QK_W + 2 * V_W
    a_cols = 2 * GATE_RANK
    off_p = off_a + a_cols

    c8 = jnp.zeros((8, d), F32).at[:b].set(c).at[b].set(c_ctx)
    mod = _ada_mod(c8, w_ada[0], b_ada[0])
    sh_a, sc_a, gt_a, sh_m, sc_m, gt_m = [m.reshape(8, 1, d) for m in jnp.split(mod, 6, axis=-1)]
    ctx_rows = lambda m: jnp.broadcast_to(m[b:b + 1], (b, 1, d))

    w = w_in[0]
    w_r = jnp.concatenate([w[:, :off_a], w[:, off_p:off_p + POOL_W], w[:, off_a:off_p],
                           jnp.zeros((d, LANES - a_cols), F32)], axis=1).astype(BF16)
    wdec = jnp.zeros((LANES, 2 * QK_W), F32)
    wdec = wdec.at[:GATE_RANK, :QK_W].set(w_decay[0, 0]).at[GATE_RANK:a_cols, QK_W:].set(w_decay[0, 1]).astype(BF16)
    bdec = b_decay[0].reshape(1, 2 * QK_W)
    g1 = norm1_g[0].reshape(1, d)

    q_c, k_c, v_c, _, la_c, _ = _inproj(ctx, ctx_rows(sh_a), ctx_rows(sc_a), g1, w_r, wdec, bdec, tm=ctx.shape[1])
    zero_s = jnp.zeros((b, QK_W, GLA_DV), F32)
    _, _, s_f, s_b = _gla(q_c, k_c, v_c, la_c, zero_s, zero_s, tb=ctx.shape[1])

    q, k, v, g, la, p = _inproj(x, sh_a[:b], sc_a[:b], g1, w_r, wdec, bdec, tm=1024)
    o_f, o_b, _, _ = _gla(q, k, v, la, s_f, s_b, tb=1024)
    pooled = _pool(p, w_pool[0], pool_scale[0])

    w_router = jnp.zeros((ROUTER_ROWS, d), F32)
    w_router = w_router.at[:N_EXPERTS].set(w_router_expert[0].T)
    w_router = w_router.at[ROUTER_GROUP_ROW0:ROUTER_GROUP_ROW0 + N_GROUPS].set(w_router_group[0].T)
    wr_hi = w_router.astype(BF16)
    wr_lo = (w_router - wr_hi.astype(F32)).astype(BF16)
    x1, xs, info, counts = _outproj(o_f, o_b, g, pooled, x, gla_norm_g[0].reshape(1, GLA_DV), w_out[0].astype(BF16),
                                    gt_a[:b], sh_m[:b], sc_m[:b], norm2_g[0].reshape(1, d), wr_hi, wr_lo,
                                    tm=SORT_TM)

    return _sparse_moe(xs, info, counts, x1, w_expert_in[0], w_expert_out[0], gt_m[:b], final_norm_g.reshape(1, d))
```

```python
import functools

import jax
import jax.numpy as jnp
from jax import lax
from jax.experimental import pallas as pl
from jax.experimental.pallas import tpu as pltpu

F32 = jnp.float32
BF16 = jnp.bfloat16

GRID_W = 64
GRID_W_LOG2 = 6
assert 1 << GRID_W_LOG2 == GRID_W
GLA_HEADS = 4
GLA_DK = 64
GLA_DV = 128
GATE_RANK = 16
GATE_NORMALIZER = 16.0
CHUNK = 64
POOL_WINDOWS = (2, 4, 8, 16)
POOL_GROUP = 128
N_GROUPS = 4
EXPERTS_PER_GROUP = 8
N_EXPERTS = N_GROUPS * EXPERTS_PER_GROUP
D_EXPERT = 256
EPS = 1e-6

QK_W = GLA_HEADS * GLA_DK
V_W = GLA_HEADS * GLA_DV
POOL_W = POOL_GROUP * len(POOL_WINDOWS)
LANES = 128
VMEM_LIMIT = 48 * 1024 * 1024

_NT = (((1,), (1,)), ((), ()))


def _dot(a, b):
    return jnp.dot(a, b, preferred_element_type=F32)


def _split_bf16(x):
    hi = x.astype(BF16)
    lo = (x - hi.astype(F32)).astype(BF16)
    return hi, lo


def _rms(x):
    return x * lax.rsqrt(jnp.mean(x * x, axis=-1, keepdims=True) + EPS)


def _silu(x):
    return x / (1.0 + jnp.exp(-x))


def _params(*sem):
    return pltpu.CompilerParams(dimension_semantics=sem, vmem_limit_bytes=VMEM_LIMIT)


def _ada_kernel(c_ref, w_ref, b_ref, o_ref):
    s = _silu(c_ref[...]).astype(BF16)
    o_ref[...] = _dot(s, w_ref[...].astype(BF16)) + b_ref[...]


def _ada_mod(c8, w_ada, b_ada):
    rows, d = c8.shape
    n = w_ada.shape[1]
    tn = 1024
    return pl.pallas_call(
        _ada_kernel,
        grid=(n // tn,),
        in_specs=[pl.BlockSpec((rows, d), lambda j: (0, 0)),
                  pl.BlockSpec((d, tn), lambda j: (0, j)),
                  pl.BlockSpec((1, tn), lambda j: (0, j))],
        out_specs=pl.BlockSpec((rows, tn), lambda j: (0, j)),
        out_shape=jax.ShapeDtypeStruct((rows, n), F32),
        compiler_params=_params("arbitrary"),
        name="ada_mod",
    )(c8, w_ada, b_ada.reshape(1, n))


def _inproj_kernel(x_ref, sh_ref, sc_ref, g1_ref, w_ref, wdec_ref, bdec_ref,
                   q_ref, k_ref, v_ref, g_ref, la_ref, p_ref):
    x = x_ref[0]
    h = _rms(x) * g1_ref[...]
    h = h * (1.0 + sc_ref[0]) + sh_ref[0]
    hb = h.astype(BF16)
    o = 0
    q_ref[0] = (_dot(hb, w_ref[:, o:o + QK_W]) * (GLA_DK ** -0.5)).astype(BF16)
    o += QK_W
    k_ref[0] = _dot(hb, w_ref[:, o:o + QK_W]).astype(BF16)
    o += QK_W
    v_ref[0] = _dot(hb, w_ref[:, o:o + V_W]).astype(BF16)
    o += V_W
    g_ref[0] = _dot(hb, w_ref[:, o:o + V_W]).astype(BF16)
    o += V_W
    p_ref[0] = _dot(hb, w_ref[:, o:o + POOL_W])
    o += POOL_W
    a_low = _dot(hb, w_ref[:, o:o + LANES])
    z = _dot(a_low.astype(BF16), wdec_ref[...]) + bdec_ref[...]
    log_sig = jnp.minimum(z, 0.0) - jnp.log(1.0 + jnp.exp(-jnp.abs(z)))
    la_ref[0] = log_sig / GATE_NORMALIZER


def _inproj(x, shift, scale, g1, w_r, wdec, bdec, tm):
    b, l, d = x.shape
    wcols = w_r.shape[1]
    tok = lambda bi, i: (bi, i, 0)
    vec = lambda bi, i: (bi, 0, 0)
    fixed = lambda bi, i: (0, 0)
    outs = [(QK_W, BF16), (QK_W, BF16), (V_W, BF16), (V_W, BF16), (2 * QK_W, F32), (POOL_W, F32)]
    return pl.pallas_call(
        _inproj_kernel,
        grid=(b, l // tm),
        in_specs=[pl.BlockSpec((1, tm, d), tok),
                  pl.BlockSpec((1, 1, d), vec),
                  pl.BlockSpec((1, 1, d), vec),
                  pl.BlockSpec((1, d), fixed),
                  pl.BlockSpec((d, wcols), fixed),
                  pl.BlockSpec(wdec.shape, fixed),
                  pl.BlockSpec(bdec.shape, fixed)],
        out_specs=[pl.BlockSpec((1, tm, w), tok) for w, _ in outs],
        out_shape=[jax.ShapeDtypeStruct((b, l, w), dt) for w, dt in outs],
        compiler_params=_params("arbitrary", "arbitrary"),
        name="inproj",
    )(x, shift, scale, g1, w_r, wdec, bdec)


GLA_GROUP = 256
CHUNKS_PER_GROUP = GLA_GROUP // CHUNK
CHUNK_LOG2 = 6
assert 1 << CHUNK_LOG2 == CHUNK


def _head_masks(rows):
    lane = lax.broadcasted_iota(jnp.int32, (rows, QK_W), 1)
    return [(lane >= h * GLA_DK) & (lane < (h + 1) * GLA_DK) for h in range(GLA_HEADS)]


def _gla_bulk(items, qd_scr, oi_scr, ds_scr, dc_scr):
    n = GLA_GROUP
    i0 = lax.broadcasted_iota(jnp.int32, (n, n), 0)
    i1 = lax.broadcasted_iota(jnp.int32, (n, n), 1)
    same_chunk = (i0 >> CHUNK_LOG2) == (i1 >> CHUNK_LOG2)
    causal = {False: same_chunk & (i1 <= i0), True: same_chunk & (i1 >= i0)}
    tri = {rev: jnp.where(m, 1.0, 0.0).astype(BF16) for rev, m in causal.items()}
    masks = _head_masks(n)
    rows = [pl.multiple_of(g * n, n) for _, _, g, *_ in items]

    cum = []
    for (d, rev, g, q_ref, k_ref, v_ref, la_ref), r in zip(items, rows):
        la_hi, la_lo = _split_bf16(la_ref[0, pl.ds(r, n), :])
        cum.append(_dot(tri[rev], la_hi) + _dot(tri[rev], la_lo))

    decayed = []
    for (d, rev, g, q_ref, k_ref, v_ref, la_ref), r, b in zip(items, rows, cum):
        last_row = 0 if rev else CHUNK - 1
        b_end = b.reshape(CHUNKS_PER_GROUP, CHUNK, QK_W)[:, last_row:last_row + 1, :]
        b_last = jnp.broadcast_to(b_end, (CHUNKS_PER_GROUP, CHUNK, QK_W)).reshape(n, QK_W)
        k = k_ref[0, pl.ds(r, n), :].astype(F32)
        qd = (q_ref[0, pl.ds(r, n), :].astype(F32) * jnp.exp(b)).astype(BF16)
        qd_scr[d, pl.ds(r, n), :] = qd
        decayed.append((qd, (k * jnp.exp(-b)).astype(BF16), k * jnp.exp(b_last - b), jnp.exp(b_end)))

    scores = [[lax.dot_general(jnp.where(m, qd, jnp.zeros_like(qd)), kd, _NT, preferred_element_type=F32)
               for m in masks] for qd, kd, _, _ in decayed]
    for (d, rev, g, q_ref, k_ref, v_ref, la_ref), r, sc in zip(items, rows, scores):
        for h in range(GLA_HEADS):
            cols = slice(h * GLA_DV, (h + 1) * GLA_DV)
            probs = jnp.where(causal[rev], sc[h], 0.0).astype(BF16)
            oi_scr[d, pl.ds(r, n), cols] = _dot(probs, v_ref[0, pl.ds(r, n), cols])

    for (d, rev, g, q_ref, k_ref, v_ref, la_ref), r, (_, _, k_st, decay) in zip(items, rows, decayed):
        for c in range(CHUNKS_PER_GROUP):
            ci = g * CHUNKS_PER_GROUP + c
            k_st_t = k_st[c * CHUNK:(c + 1) * CHUNK].T.astype(BF16)
            for h in range(GLA_HEADS):
                v_ch = v_ref[0, pl.ds(r + c * CHUNK, CHUNK), h * GLA_DV:(h + 1) * GLA_DV]
                ds_scr[d, ci, h * GLA_DK:(h + 1) * GLA_DK, :] = _dot(k_st_t[h * GLA_DK:(h + 1) * GLA_DK], v_ch)
            dc_scr[d, ci] = jnp.broadcast_to(decay[c], (LANES, QK_W)).T


def _gla_kernel(qf_ref, kf_ref, vf_ref, laf_ref, qb_ref, kb_ref, vb_ref, lab_ref, s0f_ref, s0b_ref,
                of_ref, ob_ref, sf_out_ref, sb_out_ref, s_ref, qd_scr, oi_scr, ds_scr, dc_scr, st_scr, *, n_chunks):
    j = pl.program_id(1)

    @pl.when(j == 0)
    def _():
        s_ref[0] = s0f_ref[0]
        s_ref[1] = s0b_ref[0]

    n_groups = n_chunks // CHUNKS_PER_GROUP
    unroll = 2 if n_groups % 2 == 0 else 1

    def bulk(it, carry):
        items = []
        for u in range(unroll):
            g = it * unroll + u
            items.append((0, False, g, qf_ref, kf_ref, vf_ref, laf_ref))
            items.append((1, True, g, qb_ref, kb_ref, vb_ref, lab_ref))
        _gla_bulk(items, qd_scr, oi_scr, ds_scr, dc_scr)
        return carry
    lax.fori_loop(0, n_groups // unroll, bulk, 0)

    def recur(i, carry):
        for d, ci in ((0, i), (1, n_chunks - 1 - i)):
            s = s_ref[d]
            st_scr[d, ci] = s.astype(BF16)
            s_ref[d] = dc_scr[d, ci] * s + ds_scr[d, ci]
        return carry
    lax.fori_loop(0, n_chunks, recur, 0)

    masks = _head_masks(CHUNK)

    def inter(it, carry):
        work = [(d, o_ref, (it * unroll + u) * CHUNKS_PER_GROUP + c)
                for u in range(unroll) for d, o_ref in ((0, of_ref), (1, ob_ref)) for c in range(CHUNKS_PER_GROUP)]
        from_state = []
        for d, o_ref, ci in work:
            qd = qd_scr[d, pl.ds(pl.multiple_of(ci * CHUNK, CHUNK), CHUNK), :]
            lhs = jnp.concatenate([jnp.where(m, qd, jnp.zeros_like(qd)) for m in masks], axis=0)
            from_state.append(_dot(lhs, st_scr[d, ci]))
        for (d, o_ref, ci), fs in zip(work, from_state):
            r = pl.multiple_of(ci * CHUNK, CHUNK)
            for h in range(GLA_HEADS):
                cols = slice(h * GLA_DV, (h + 1) * GLA_DV)
                o_ref[0, pl.ds(r, CHUNK), cols] = (oi_scr[d, pl.ds(r, CHUNK), cols]
                                                   + fs[h * CHUNK:(h + 1) * CHUNK]).astype(BF16)
        return carry
    lax.fori_loop(0, n_groups // unroll, inter, 0)

    @pl.when(j == pl.num_programs(1) - 1)
    def _():
        sf_out_ref[0] = s_ref[0]
        sb_out_ref[0] = s_ref[1]


def _gla(q, k, v, la, s0f, s0b, tb):
    b, l, _ = q.shape
    assert tb % GLA_GROUP == 0 and l % tb == 0
    nb = l // tb
    n_chunks = tb // CHUNK
    fwd = lambda bi, j: (bi, j, 0)
    bwd = lambda bi, j: (bi, nb - 1 - j, 0)
    fwd_la = lambda bi, j: (bi, j, 0)
    bwd_la = lambda bi, j: (bi, nb - 1 - j, 1)
    st = lambda bi, j: (bi, 0, 0)
    state_shape = (b, QK_W, GLA_DV)
    return pl.pallas_call(
        functools.partial(_gla_kernel, n_chunks=tb // CHUNK),
        grid=(b, nb),
        in_specs=[pl.BlockSpec((1, tb, QK_W), fwd), pl.BlockSpec((1, tb, QK_W), fwd),
                  pl.BlockSpec((1, tb, V_W), fwd), pl.BlockSpec((1, tb, QK_W), fwd_la),
                  pl.BlockSpec((1, tb, QK_W), bwd), pl.BlockSpec((1, tb, QK_W), bwd),
                  pl.BlockSpec((1, tb, V_W), bwd), pl.BlockSpec((1, tb, QK_W), bwd_la),
                  pl.BlockSpec((1, QK_W, GLA_DV), st), pl.BlockSpec((1, QK_W, GLA_DV), st)],
        out_specs=[pl.BlockSpec((1, tb, V_W), fwd), pl.BlockSpec((1, tb, V_W), bwd),
                   pl.BlockSpec((1, QK_W, GLA_DV), st), pl.BlockSpec((1, QK_W, GLA_DV), st)],
        out_shape=[jax.ShapeDtypeStruct((b, l, V_W), BF16), jax.ShapeDtypeStruct((b, l, V_W), BF16),
                   jax.ShapeDtypeStruct(state_shape, F32), jax.ShapeDtypeStruct(state_shape, F32)],
        scratch_shapes=[pltpu.VMEM((2, QK_W, GLA_DV), F32),
                        pltpu.VMEM((2, tb, QK_W), BF16),
                        pltpu.VMEM((2, tb, V_W), F32),
                        pltpu.VMEM((2, n_chunks, QK_W, GLA_DV), F32),
                        pltpu.VMEM((2, n_chunks, QK_W, GLA_DV), F32),
                        pltpu.VMEM((2, n_chunks, QK_W, GLA_DV), BF16)],
        compiler_params=_params("arbitrary", "arbitrary"),
        name="gla",
    )(q, k, v, la, q, k, v, la, s0f, s0b)


POOL_BLK = 256
POOL_COL_STEP = 1024
POOL_ROW_STEP = 256
POOL_OUT_STEP = 1024


def _pool_body(w, first_batch, p_ref, wp_ref, ps_ref, o_ref, buf_ref, inv_ref, y_ref, n_tok):
    lo = w // 2
    hi = w - 1 - lo
    pad = lo * GRID_W
    n_rows = n_tok // GRID_W
    blk = POOL_BLK

    @pl.when(first_batch)
    def _():
        zeros = jnp.zeros((GRID_W * 8, LANES), F32)
        buf_ref[0:pad, :] = zeros[0:pad]
        buf_ref[pad + n_tok:pad + n_tok + hi * GRID_W + GRID_W, :] = zeros[0:hi * GRID_W + GRID_W]

        def inv_step(t, carry):
            r = pl.multiple_of(t * POOL_OUT_STEP, POOL_OUT_STEP)
            tok = r + lax.broadcasted_iota(jnp.int32, (POOL_OUT_STEP, LANES), 0)
            g_row = tok >> GRID_W_LOG2
            g_col = tok & (GRID_W - 1)
            cnt_c = jnp.minimum(g_col + hi, GRID_W - 1) - jnp.maximum(g_col - lo, 0) + 1
            cnt_r = jnp.minimum(g_row + hi, n_rows - 1) - jnp.maximum(g_row - lo, 0) + 1
            inv_ref[pl.ds(r, POOL_OUT_STEP), :] = 1.0 / (cnt_c * cnt_r).astype(F32)
            return carry
        lax.fori_loop(0, n_tok // POOL_OUT_STEP, inv_step, 0)

    i0 = lax.broadcasted_iota(jnp.int32, (blk, blk), 0)
    i1 = lax.broadcasted_iota(jnp.int32, (blk, blk), 1)
    same_row = (i0 >> GRID_W_LOG2) == (i1 >> GRID_W_LOG2)
    band = jnp.where(same_row & (i1 - i0 >= -lo) & (i1 - i0 <= hi), 1.0, 0.0).astype(BF16)

    def col_step(t, carry):
        rows = [pl.multiple_of(t * POOL_COL_STEP + u * blk, blk) for u in range(POOL_COL_STEP // blk)]
        sums = []
        for r in rows:
            sums.append(_dot(band, jnp.concatenate(_split_bf16(p_ref[0, pl.ds(r, blk), :]), axis=-1)))
        for r, s in zip(rows, sums):
            buf_ref[pl.ds(pad + r, blk), :] = s[:, :LANES] + s[:, LANES:]
        return carry
    lax.fori_loop(0, n_tok // POOL_COL_STEP, col_step, 0)

    def row_step(t, carry):
        r = pl.multiple_of(t * POOL_ROW_STEP, POOL_ROW_STEP)
        acc = buf_ref[pl.ds(r, POOL_ROW_STEP), :]
        for d in range(1, w):
            acc = acc + buf_ref[pl.ds(r + d * GRID_W, POOL_ROW_STEP), :]
        m = acc * inv_ref[pl.ds(r, POOL_ROW_STEP), :]
        y_ref[pl.ds(r, POOL_ROW_STEP), :] = (m - p_ref[0, pl.ds(r, POOL_ROW_STEP), :]).astype(BF16)
        return carry
    lax.fori_loop(0, n_tok // POOL_ROW_STEP, row_step, 0)

    wp = wp_ref[0].astype(BF16)
    scale = ps_ref[0]

    def out_step(t, carry):
        r = pl.multiple_of(t * POOL_OUT_STEP, POOL_OUT_STEP)
        o_ref[0, pl.ds(r, POOL_OUT_STEP), :] = (_dot(y_ref[pl.ds(r, POOL_OUT_STEP), :], wp) * scale).astype(BF16)
        return carry
    lax.fori_loop(0, n_tok // POOL_OUT_STEP, out_step, 0)


def _pool_kernel(p_ref, wp_ref, ps_ref, o_ref, buf_ref, inv_ref, y_ref, *, n_tok):
    g = pl.program_id(0)
    first_batch = pl.program_id(1) == 0
    for gi, w in enumerate(POOL_WINDOWS):
        @pl.when(g == gi)
        def _(w=w):
            _pool_body(w, first_batch, p_ref, wp_ref, ps_ref, o_ref, buf_ref, inv_ref, y_ref, n_tok)


def _pool(p, w_pool, pool_scale):
    b, l, _ = p.shape
    n_g = len(POOL_WINDOWS)
    max_w = max(POOL_WINDOWS)
    buf_rows = l + (max_w + 1) * GRID_W
    return pl.pallas_call(
        functools.partial(_pool_kernel, n_tok=l),
        grid=(n_g, b),
        in_specs=[pl.BlockSpec((1, l, POOL_GROUP), lambda g, bi: (bi, 0, g)),
                  pl.BlockSpec((1, POOL_GROUP, POOL_GROUP), lambda g, bi: (g, 0, 0)),
                  pl.BlockSpec((1, 1, POOL_GROUP), lambda g, bi: (g, 0, 0))],
        out_specs=pl.BlockSpec((1, l, POOL_GROUP), lambda g, bi: (bi, 0, g)),
        out_shape=jax.ShapeDtypeStruct((b, l, POOL_W), BF16),
        scratch_shapes=[pltpu.VMEM((buf_rows, LANES), F32), pltpu.VMEM((l, LANES), F32),
                        pltpu.VMEM((l, LANES), BF16)],
        compiler_params=_params("arbitrary", "arbitrary"),
        name="pool",
    )(p, w_pool, pool_scale.reshape(n_g, 1, POOL_GROUP))


ROUTER_ROWS = 48
ROUTER_GROUP_ROW0 = N_EXPERTS
TOP_K = 2
SORT_TM = 512
PIECE = 16
SLOT_ROWS = TOP_K * SORT_TM + N_EXPERTS * PIECE
INFO_W1, INFO_W2, INFO_DEST1, INFO_DEST2 = range(4)


def _outproj_kernel(of_ref, ob_ref, g_ref, pooled_ref, x_ref, gn_ref, wout_ref, gt_ref, sh_ref, sc_ref, g2_ref,
                    wr_hi_ref, wr_lo_ref, x1_ref, xs_ref, info_ref, cnt_ref):
    o = of_ref[0].astype(F32) + ob_ref[0].astype(F32)
    gate = _silu(g_ref[0].astype(F32))
    parts = []
    for h in range(GLA_HEADS):
        sl = slice(h * GLA_DV, (h + 1) * GLA_DV)
        parts.append((_rms(o[:, sl]) * gn_ref[...] * gate[:, sl]).astype(BF16))
    o_n = jnp.concatenate(parts, axis=-1)
    out = _dot(o_n, wout_ref[0:V_W, :]) + _dot(pooled_ref[0], wout_ref[V_W:V_W + POOL_W, :])
    x1 = x_ref[0] + gt_ref[0] * out
    x1_ref[0] = x1.astype(BF16)
    h2 = _rms(x1) * g2_ref[...]
    h2 = h2 * (1.0 + sc_ref[0]) + sh_ref[0]
    h2_bf = h2.astype(BF16)
    h2_lo = (h2 - h2_bf.astype(F32)).astype(BF16)
    logits = (lax.dot_general(wr_hi_ref[...], h2_bf, _NT, preferred_element_type=F32)
              + lax.dot_general(wr_hi_ref[...], h2_lo, _NT, preferred_element_type=F32)
              + lax.dot_general(wr_lo_ref[...], h2_bf, _NT, preferred_element_type=F32))
    tm = logits.shape[1]
    row = lax.broadcasted_iota(jnp.int32, logits.shape, 0)
    neg = -jnp.inf
    big = jnp.int32(1 << 20)
    is_group = (row >= ROUTER_GROUP_ROW0) & (row < ROUTER_GROUP_ROW0 + N_GROUPS)
    gl = jnp.where(is_group, logits, neg)
    g_max = jnp.max(gl, axis=0, keepdims=True)
    g_idx = jnp.min(jnp.where(gl == g_max, row - ROUTER_GROUP_ROW0, big), axis=0, keepdims=True)
    g_prob = 1.0 / jnp.sum(jnp.where(is_group, jnp.exp(gl - g_max), 0.0), axis=0, keepdims=True)
    n_e = EXPERTS_PER_GROUP
    el = logits[0:n_e]
    for g in range(1, N_GROUPS):
        el = jnp.where(g_idx == g, logits[g * n_e:(g + 1) * n_e], el)
    r8 = lax.broadcasted_iota(jnp.int32, el.shape, 0)
    m1 = jnp.max(el, axis=0, keepdims=True)
    i1 = jnp.min(jnp.where(el == m1, r8, big), axis=0, keepdims=True)
    el2 = jnp.where(r8 == i1, neg, el)
    m2 = jnp.max(el2, axis=0, keepdims=True)
    i2 = jnp.min(jnp.where(el2 == m2, r8, big), axis=0, keepdims=True)
    r = jnp.exp(m2 - m1)
    w1 = g_prob / (1.0 + r)
    w2 = g_prob * r / (1.0 + r)

    e1 = g_idx * n_e + i1
    e2 = g_idx * n_e + i2
    r_e = lax.broadcasted_iota(jnp.int32, (N_EXPERTS, tm), 0)
    pick1 = r_e == e1
    pick2 = r_e == e2
    onehot = jnp.where(pick1 | pick2, 1.0, 0.0)
    t_r = lax.broadcasted_iota(jnp.int32, (tm, tm), 0)
    t_c = lax.broadcasted_iota(jnp.int32, (tm, tm), 1)
    rank = _dot(onehot.astype(BF16), jnp.where(t_r < t_c, 1.0, 0.0).astype(BF16))
    count = jnp.sum(onehot, axis=1, keepdims=True)
    n_pieces = jnp.floor((count + (PIECE - 1)) * (1.0 / PIECE))
    e_r = lax.broadcasted_iota(jnp.int32, (N_EXPERTS, N_EXPERTS), 0)
    e_c = lax.broadcasted_iota(jnp.int32, (N_EXPERTS, N_EXPERTS), 1)
    run_start = _dot(jnp.where(e_c < e_r, 1.0, 0.0).astype(BF16),
                     jnp.broadcast_to(n_pieces, (N_EXPERTS, LANES)).astype(BF16))[:, 0:1] * PIECE
    slot = rank + run_start
    dest1 = jnp.sum(jnp.where(pick1, slot, 0.0), axis=0, keepdims=True)
    dest2 = jnp.sum(jnp.where(pick2, slot, 0.0), axis=0, keepdims=True)
    info_rows = [w1, w2, dest1, dest2]
    info_t = jnp.concatenate(info_rows + [jnp.zeros((LANES - len(info_rows), tm), F32)], axis=0)
    info_ref[0] = info_t.T
    cnt_ref[0] = jnp.broadcast_to(count, (N_EXPERTS, LANES)).astype(jnp.int32)
    s_id = lax.broadcasted_iota(jnp.int32, (SLOT_ROWS, tm), 0)
    perm_t = jnp.where((s_id == dest1.astype(jnp.int32)) | (s_id == dest2.astype(jnp.int32)),
                       1.0, 0.0).astype(BF16)
    xs_ref[0] = _dot(perm_t, h2_bf).astype(BF16)


def _outproj(o_f, o_b, g, pooled, x, gn, w_out, gt_a, sh_m, sc_m, g2, wr_hi, wr_lo, tm):
    b, l, d = x.shape
    per_batch = l // tm
    n_tiles = b * per_batch
    tok = lambda s: (s // per_batch, s % per_batch, 0)
    vec = lambda s: (s // per_batch, 0, 0)
    fixed = lambda s: (0, 0)
    return pl.pallas_call(
        _outproj_kernel,
        grid=(n_tiles,),
        in_specs=[pl.BlockSpec((1, tm, V_W), tok), pl.BlockSpec((1, tm, V_W), tok),
                  pl.BlockSpec((1, tm, V_W), tok), pl.BlockSpec((1, tm, POOL_W), tok),
                  pl.BlockSpec((1, tm, d), tok),
                  pl.BlockSpec((1, GLA_DV), fixed),
                  pl.BlockSpec(w_out.shape, fixed),
                  pl.BlockSpec((1, 1, d), vec), pl.BlockSpec((1, 1, d), vec), pl.BlockSpec((1, 1, d), vec),
                  pl.BlockSpec((1, d), fixed),
                  pl.BlockSpec(wr_hi.shape, fixed), pl.BlockSpec(wr_lo.shape, fixed)],
        out_specs=[pl.BlockSpec((1, tm, d), tok),
                   pl.BlockSpec((1, SLOT_ROWS, d), lambda s: (s, 0, 0)),
                   pl.BlockSpec((1, tm, LANES), tok),
                   pl.BlockSpec((1, N_EXPERTS, LANES), lambda s: (s, 0, 0))],
        out_shape=[jax.ShapeDtypeStruct((b, l, d), BF16),
                   jax.ShapeDtypeStruct((n_tiles, SLOT_ROWS, d), BF16),
                   jax.ShapeDtypeStruct((b, l, LANES), F32),
                   jax.ShapeDtypeStruct((n_tiles, N_EXPERTS, LANES), jnp.int32)],
        compiler_params=_params("arbitrary"),
        name="outproj_router",
    )(o_f, o_b, g, pooled, x, gn, w_out, gt_a, sh_m, sc_m, g2, wr_hi, wr_lo)


EXPERT_TM = 512
PIECES_PER_STEP = EXPERT_TM // PIECE
GATHER_SLOTS = 6
SLOT_PIECES = SLOT_ROWS // PIECE


def _expert_steps(n_tok):
    pieces = TOP_K * n_tok // PIECE + (n_tok // SORT_TM) * N_EXPERTS
    return -(-pieces // PIECES_PER_STEP) + N_EXPERTS


def _experts_kernel(exp_ref, valid_ref, src_ref, xs_hbm, wi_ref, wo_ref, ys_ref, xbuf, wi_bf, wo_bf, sem):
    i = pl.program_id(0)
    n_steps = pl.num_programs(0)
    slot = i % GATHER_SLOTS

    def piece_copy(s, j, src_row):
        return pltpu.make_async_copy(xs_hbm.at[pl.ds(src_row, PIECE), :], xbuf.at[s, pl.ds(j * PIECE, PIECE), :],
                                     sem.at[s])

    def fetch(step):
        @pl.when((step < n_steps) & (valid_ref[jnp.minimum(step, n_steps - 1)] == 1))
        def _():
            for j in range(PIECES_PER_STEP):
                piece_copy(step % GATHER_SLOTS, j,
                           pl.multiple_of(src_ref[step * PIECES_PER_STEP + j], PIECE)).start(priority=j % 2)

    @pl.when(i == 0)
    def _():
        for ahead in range(GATHER_SLOTS - 1):
            fetch(jnp.int32(ahead))

    fetch(i + GATHER_SLOTS - 1)
    expert = exp_ref[i]

    @pl.when((i == 0) | (exp_ref[jnp.maximum(i - 1, 0)] != expert))
    def _():
        wi_bf[...] = wi_ref[0].astype(BF16)
        wo_bf[...] = wo_ref[0].astype(BF16)

    @pl.when(valid_ref[i] == 1)
    def _():
        for j in range(PIECES_PER_STEP):
            piece_copy(slot, j, 0).wait()
        au = _dot(xbuf[slot], wi_bf[...])
        hidden = (_silu(au[:, :D_EXPERT]) * au[:, D_EXPERT:]).astype(BF16)
        ys_ref[...] = _dot(hidden, wo_bf[...]).astype(BF16)

    @pl.when(valid_ref[i] == 0)
    def _():
        ys_ref[...] = jnp.zeros_like(ys_ref)


def _experts(xs, step_expert, step_valid, piece_src, wi, wo):
    n_steps = step_expert.shape[0]
    width = xs.shape[1]
    d = wo.shape[2]
    return pl.pallas_call(
        _experts_kernel,
        grid_spec=pltpu.PrefetchScalarGridSpec(
            num_scalar_prefetch=3,
            grid=(n_steps,),
            in_specs=[pl.BlockSpec(memory_space=pl.ANY),
                      pl.BlockSpec((1,) + wi.shape[1:], lambda i, e, v, s: (e[i], 0, 0)),
                      pl.BlockSpec((1,) + wo.shape[1:], lambda i, e, v, s: (e[i], 0, 0))],
            out_specs=pl.BlockSpec((EXPERT_TM, d), lambda i, e, v, s: (i, 0)),
            scratch_shapes=[pltpu.VMEM((GATHER_SLOTS, EXPERT_TM, width), BF16), pltpu.VMEM(wi.shape[1:], BF16),
                            pltpu.VMEM(wo.shape[1:], BF16), pltpu.SemaphoreType.DMA((GATHER_SLOTS,))]),
        out_shape=jax.ShapeDtypeStruct((n_steps * EXPERT_TM, d), BF16),
        compiler_params=_params("arbitrary"),
        name="moe_experts",
    )(step_expert, step_valid, piece_src, xs, wi, wo)


def _combine_kernel(src_ref, n_ref, ys_hbm, info_ref, x1_ref, gt_ref, gf_ref, o_ref, stage, sem):
    i = pl.program_id(0)
    n_steps = pl.num_programs(0)
    tm = SORT_TM

    def piece_copy(s, src_row, m):
        return pltpu.make_async_copy(ys_hbm.at[pl.ds(src_row, PIECE), :],
                                     stage.at[s, pl.ds(pl.multiple_of(m * PIECE, PIECE), PIECE), :], sem.at[s])

    def fetch(tile, s):
        def body(m, carry):
            piece_copy(s, pl.multiple_of(src_ref[tile * SLOT_PIECES + m], PIECE), m).start()
            return carry
        lax.fori_loop(0, n_ref[tile], body, 0)

    @pl.when(i == 0)
    def _():
        fetch(0, 0)

    @pl.when(i + 1 < n_steps)
    def _():
        fetch(i + 1, (i + 1) % 2)

    slot = i % 2
    n_tile = n_ref[i]

    def wait_body(m, carry):
        piece_copy(slot, 0, m).wait()
        return carry
    lax.fori_loop(0, n_tile, wait_body, 0)

    def zero_body(m, carry):
        stage[slot, pl.ds(pl.multiple_of(m * PIECE, PIECE), PIECE), :] = jnp.zeros((PIECE, stage.shape[2]), BF16)
        return carry
    lax.fori_loop(n_tile, SLOT_PIECES, zero_body, 0)

    ys = stage[slot]
    info = info_ref[...]
    lane = lax.broadcasted_iota(jnp.int32, info.shape, 1)
    column = lambda ln: jnp.sum(jnp.where(lane == ln, info, 0.0), axis=-1, keepdims=True)
    dest1 = column(INFO_DEST1).astype(jnp.int32)
    dest2 = column(INFO_DEST2).astype(jnp.int32)
    slot_id = lax.broadcasted_iota(jnp.int32, (tm, SLOT_ROWS), 1)
    gated = jnp.where(slot_id == dest1, column(INFO_W1),
                      jnp.where(slot_id == dest2, column(INFO_W2), 0.0)).astype(BF16)
    y = _dot(gated, ys)
    x2 = x1_ref[...].astype(F32) + gt_ref[0] * y
    o_ref[...] = _rms(x2) * gf_ref[...]


def _combine(ys, piece_src, tile_pieces, info, x1, gt_m, gf, tiles_per_batch):
    t, d = x1.shape
    n_tiles = t // SORT_TM
    tok = lambda i, *_: (i, 0)
    return pl.pallas_call(
        _combine_kernel,
        grid_spec=pltpu.PrefetchScalarGridSpec(
            num_scalar_prefetch=2,
            grid=(n_tiles,),
            in_specs=[pl.BlockSpec(memory_space=pl.ANY),
                      pl.BlockSpec((SORT_TM, LANES), tok),
                      pl.BlockSpec((SORT_TM, d), tok),
                      pl.BlockSpec((1, 1, d), lambda i, *_: (i // tiles_per_batch, 0, 0)),
                      pl.BlockSpec((1, d), lambda i, *_: (0, 0))],
            out_specs=pl.BlockSpec((SORT_TM, d), tok),
            scratch_shapes=[pltpu.VMEM((2, SLOT_ROWS, d), BF16), pltpu.SemaphoreType.DMA((2,))]),
        out_shape=jax.ShapeDtypeStruct((t, d), F32),
        compiler_params=_params("arbitrary"),
        name="moe_combine",
    )(piece_src, tile_pieces, ys, info, x1, gt_m, gf)


def _sparse_moe(xs, info, counts, x1, wi, wo, gt_m, gf):
    b, l, d = x1.shape
    t = b * l
    n_tiles = t // SORT_TM
    i32 = jnp.int32

    cnt = counts[:, :, 0]
    run_pieces = (cnt + (PIECE - 1)) // PIECE
    run_slot = (jnp.cumsum(run_pieces, axis=1) - run_pieces) * PIECE
    before = jnp.cumsum(run_pieces, axis=0) - run_pieces
    expert_pieces = jnp.sum(run_pieces, axis=0)
    steps_e = (expert_pieces + PIECES_PER_STEP - 1) // PIECES_PER_STEP
    step_end = jnp.cumsum(steps_e)
    region = (step_end - steps_e) * PIECES_PER_STEP
    run_pos = (region[None, :] + before) * PIECE

    n_steps = _expert_steps(t)
    step = jnp.arange(n_steps, dtype=i32)
    step_expert = jnp.minimum(jnp.sum((step[:, None] >= step_end[None, :]).astype(i32), axis=1), N_EXPERTS - 1)
    step_valid = (step < step_end[-1]).astype(i32)
    lookup = lambda onehot, table: jnp.dot(onehot, table.astype(F32), precision=lax.Precision.HIGHEST)
    piece = jnp.arange(n_steps * PIECES_PER_STEP, dtype=i32)
    is_expert = (jnp.repeat(step_expert, PIECES_PER_STEP)[:, None] == jnp.arange(N_EXPERTS, dtype=i32)).astype(F32)
    k_local = piece.astype(F32) - lookup(is_expert, region)
    ends = lookup(is_expert, jnp.cumsum(run_pieces, axis=0).T)
    tile_of = jnp.minimum(jnp.sum((ends <= k_local[:, None]).astype(i32), axis=1), n_tiles - 1)
    is_tile = (tile_of[:, None] == jnp.arange(n_tiles, dtype=i32)).astype(F32)
    j_in_run = k_local - jnp.sum(is_tile * lookup(is_expert, before.T), axis=1)
    slot_start = jnp.sum(is_tile * lookup(is_expert, run_slot.T), axis=1)
    real = (k_local < lookup(is_expert, expert_pieces)) & (jnp.repeat(step_valid, PIECES_PER_STEP) == 1)
    zero_piece = SLOT_ROWS - PIECE
    piece_src = jnp.where(real, tile_of * SLOT_ROWS + (slot_start + j_in_run * PIECE).astype(i32), zero_piece)


    m = jnp.arange(SLOT_PIECES, dtype=i32)
    run_end = jnp.cumsum(run_pieces, axis=1)
    run_of = jnp.minimum(jnp.sum((run_end[:, None, :] <= m[None, :, None]).astype(i32), axis=-1), N_EXPERTS - 1)
    is_run = run_of[:, :, None] == jnp.arange(N_EXPERTS, dtype=i32)
    pos_m = jnp.sum(jnp.where(is_run, run_pos[:, None, :], 0), axis=-1)
    first_m = jnp.sum(jnp.where(is_run, (run_end - run_pieces)[:, None, :], 0), axis=-1)
    back_src = jnp.where(m[None, :] < run_end[:, -1:], pos_m + (m[None, :] - first_m) * PIECE, 0)

    flat = lambda a: a.reshape(-1).astype(i32)
    ys = _experts(xs.reshape(n_tiles * SLOT_ROWS, d), step_expert, step_valid, flat(piece_src), wi, wo)
    out = _combine(ys, flat(back_src), flat(run_end[:, -1]), info.reshape(t, LANES), x1.reshape(t, d), gt_m, gf,
                   l // SORT_TM)
    return out.reshape(b, l, d)


def kernel(x, c, ctx, c_ctx, w_ada, b_ada, norm1_g, w_in, w_decay, b_decay, gla_norm_g, w_pool, pool_scale, w_out,
           norm2_g, w_router_group, w_router_expert, w_expert_in, w_expert_out, final_norm_g):
    assert w_ada.shape[0] == 1, "single-layer trunk"
    b, l, d = x.shape
    off_a = 2 * QK_W + 2 * V_W
    a_cols = 2 * GATE_RANK
    off_p = off_a + a_cols

    c8 = jnp.zeros((8, d), F32).at[:b].set(c).at[b].set(c_ctx)
    mod = _ada_mod(c8, w_ada[0], b_ada[0])
    sh_a, sc_a, gt_a, sh_m, sc_m, gt_m = [m.reshape(8, 1, d) for m in jnp.split(mod, 6, axis=-1)]
    ctx_rows = lambda m: jnp.broadcast_to(m[b:b + 1], (b, 1, d))

    w = w_in[0]
    w_r = jnp.concatenate([w[:, :off_a], w[:, off_p:off_p + POOL_W], w[:, off_a:off_p],
                           jnp.zeros((d, LANES - a_cols), F32)], axis=1).astype(BF16)
    wdec = jnp.zeros((LANES, 2 * QK_W), F32)
    wdec = wdec.at[:GATE_RANK, :QK_W].set(w_decay[0, 0]).at[GATE_RANK:a_cols, QK_W:].set(w_decay[0, 1]).astype(BF16)
    bdec = b_decay[0].reshape(1, 2 * QK_W)
    g1 = norm1_g[0].reshape(1, d)

    q_c, k_c, v_c, _, la_c, _ = _inproj(ctx, ctx_rows(sh_a), ctx_rows(sc_a), g1, w_r, wdec, bdec, tm=ctx.shape[1])
    zero_s = jnp.zeros((b, QK_W, GLA_DV), F32)
    _, _, s_f, s_b = _gla(q_c, k_c, v_c, la_c, zero_s, zero_s, tb=ctx.shape[1])

    q, k, v, g, la, p = _inproj(x, sh_a[:b], sc_a[:b], g1, w_r, wdec, bdec, tm=1024)
    o_f, o_b, _, _ = _gla(q, k, v, la, s_f, s_b, tb=1024)
    pooled = _pool(p, w_pool[0], pool_scale[0])

    w_router = jnp.zeros((ROUTER_ROWS, d), F32)
    w_router = w_router.at[:N_EXPERTS].set(w_router_expert[0].T)
    w_router = w_router.at[ROUTER_GROUP_ROW0:ROUTER_GROUP_ROW0 + N_GROUPS].set(w_router_group[0].T)
    wr_hi = w_router.astype(BF16)
    wr_lo = (w_router - wr_hi.astype(F32)).astype(BF16)
    x1, xs, info, counts = _outproj(o_f, o_b, g, pooled, x, gla_norm_g[0].reshape(1, GLA_DV), w_out[0].astype(BF16),
                                    gt_a[:b], sh_m[:b], sc_m[:b], norm2_g[0].reshape(1, d), wr_hi, wr_lo,
                                    tm=SORT_TM)

    return _sparse_moe(xs, info, counts, x1, w_expert_in[0], w_expert_out[0], gt_m[:b], final_norm_g.reshape(1, d))
```

```python
import functools

import jax
import jax.numpy as jnp
from jax import lax
from jax.experimental import pallas as pl
from jax.experimental.pallas import tpu as pltpu

F32 = jnp.float32
BF16 = jnp.bfloat16

GRID_W = 64
GRID_W_LOG2 = 6
assert 1 << GRID_W_LOG2 == GRID_W
GLA_HEADS = 4
GLA_DK = 64
GLA_DV = 128
GATE_RANK = 16
GATE_NORMALIZER = 16.0
CHUNK = 64
POOL_WINDOWS = (2, 4, 8, 16)
POOL_GROUP = 128
N_GROUPS = 4
EXPERTS_PER_GROUP = 8
N_EXPERTS = N_GROUPS * EXPERTS_PER_GROUP
D_EXPERT = 256
EPS = 1e-6

QK_W = GLA_HEADS * GLA_DK
V_W = GLA_HEADS * GLA_DV
POOL_W = POOL_GROUP * len(POOL_WINDOWS)
LANES = 128
VMEM_LIMIT = 48 * 1024 * 1024

_NT = (((1,), (1,)), ((), ()))


def _dot(a, b):
    return jnp.dot(a, b, preferred_element_type=F32)


def _split_bf16(x):
    hi = x.astype(BF16)
    lo = (x - hi.astype(F32)).astype(BF16)
    return hi, lo


def _rms(x):
    return x * lax.rsqrt(jnp.mean(x * x, axis=-1, keepdims=True) + EPS)


def _silu(x):
    return x / (1.0 + jnp.exp(-x))


def _params(*sem):
    return pltpu.CompilerParams(dimension_semantics=sem, vmem_limit_bytes=VMEM_LIMIT)


def _ada_kernel(c_ref, w_ref, b_ref, o_ref):
    s = _silu(c_ref[...]).astype(BF16)
    o_ref[...] = _dot(s, w_ref[...].astype(BF16)) + b_ref[...]


def _ada_mod(c8, w_ada, b_ada):
    rows, d = c8.shape
    n = w_ada.shape[1]
    tn = 1024
    return pl.pallas_call(
        _ada_kernel,
        grid=(n // tn,),
        in_specs=[pl.BlockSpec((rows, d), lambda j: (0, 0)),
                  pl.BlockSpec((d, tn), lambda j: (0, j)),
                  pl.BlockSpec((1, tn), lambda j: (0, j))],
        out_specs=pl.BlockSpec((rows, tn), lambda j: (0, j)),
        out_shape=jax.ShapeDtypeStruct((rows, n), F32),
        compiler_params=_params("arbitrary"),
        name="ada_mod",
    )(c8, w_ada, b_ada.reshape(1, n))


def _inproj_kernel(x_ref, sh_ref, sc_ref, g1_ref, w_ref, wdec_ref, bdec_ref,
                   q_ref, k_ref, v_ref, g_ref, la_ref, p_ref):
    x = x_ref[0]
    h = _rms(x) * g1_ref[...]
    h = h * (1.0 + sc_ref[0]) + sh_ref[0]
    hb = h.astype(BF16)
    o = 0
    q_ref[0] = (_dot(hb, w_ref[:, o:o + QK_W]) * (GLA_DK ** -0.5)).astype(BF16)
    o += QK_W
    k_ref[0] = _dot(hb, w_ref[:, o:o + QK_W]).astype(BF16)
    o += QK_W
    v_ref[0] = _dot(hb, w_ref[:, o:o + V_W]).astype(BF16)
    o += V_W
    g_ref[0] = _dot(hb, w_ref[:, o:o + V_W]).astype(BF16)
    o += V_W
    p_ref[0] = _dot(hb, w_ref[:, o:o + POOL_W])
    o += POOL_W
    a_low = _dot(hb, w_ref[:, o:o + LANES])
    z = _dot(a_low.astype(BF16), wdec_ref[...]) + bdec_ref[...]
    log_sig = jnp.minimum(z, 0.0) - jnp.log(1.0 + jnp.exp(-jnp.abs(z)))
    la_ref[0] = log_sig / GATE_NORMALIZER


def _inproj(x, shift, scale, g1, w_r, wdec, bdec, tm):
    b, l, d = x.shape
    wcols = w_r.shape[1]
    tok = lambda bi, i: (bi, i, 0)
    vec = lambda bi, i: (bi, 0, 0)
    fixed = lambda bi, i: (0, 0)
    outs = [(QK_W, BF16), (QK_W, BF16), (V_W, BF16), (V_W, BF16), (2 * QK_W, F32), (POOL_W, F32)]
    return pl.pallas_call(
        _inproj_kernel,
        grid=(b, l // tm),
        in_specs=[pl.BlockSpec((1, tm, d), tok),
                  pl.BlockSpec((1, 1, d), vec),
                  pl.BlockSpec((1, 1, d), vec),
                  pl.BlockSpec((1, d), fixed),
                  pl.BlockSpec((d, wcols), fixed),
                  pl.BlockSpec(wdec.shape, fixed),
                  pl.BlockSpec(bdec.shape, fixed)],
        out_specs=[pl.BlockSpec((1, tm, w), tok) for w, _ in outs],
        out_shape=[jax.ShapeDtypeStruct((b, l, w), dt) for w, dt in outs],
        compiler_params=_params("arbitrary", "arbitrary"),
        name="inproj",
    )(x, shift, scale, g1, w_r, wdec, bdec)


GLA_GROUP = 256
CHUNKS_PER_GROUP = GLA_GROUP // CHUNK
CHUNK_LOG2 = 6
assert 1 << CHUNK_LOG2 == CHUNK


def _head_masks(rows):
    lane = lax.broadcasted_iota(jnp.int32, (rows, QK_W), 1)
    return [(lane >= h * GLA_DK) & (lane < (h + 1) * GLA_DK) for h in range(GLA_HEADS)]


def _gla_bulk(items, qd_scr, oi_scr, ds_scr, dc_scr):
    n = GLA_GROUP
    i0 = lax.broadcasted_iota(jnp.int32, (n, n), 0)
    i1 = lax.broadcasted_iota(jnp.int32, (n, n), 1)
    same_chunk = (i0 >> CHUNK_LOG2) == (i1 >> CHUNK_LOG2)
    causal = {False: same_chunk & (i1 <= i0), True: same_chunk & (i1 >= i0)}
    tri = {rev: jnp.where(m, 1.0, 0.0).astype(BF16) for rev, m in causal.items()}
    masks = _head_masks(n)
    rows = [pl.multiple_of(g * n, n) for _, _, g, *_ in items]

    cum = []
    for (d, rev, g, q_ref, k_ref, v_ref, la_ref), r in zip(items, rows):
        la_hi, la_lo = _split_bf16(la_ref[0, pl.ds(r, n), :])
        cum.append(_dot(tri[rev], la_hi) + _dot(tri[rev], la_lo))

    decayed = []
    for (d, rev, g, q_ref, k_ref, v_ref, la_ref), r, b in zip(items, rows, cum):
        last_row = 0 if rev else CHUNK - 1
        b_end = b.reshape(CHUNKS_PER_GROUP, CHUNK, QK_W)[:, last_row:last_row + 1, :]
        b_last = jnp.broadcast_to(b_end, (CHUNKS_PER_GROUP, CHUNK, QK_W)).reshape(n, QK_W)
        k = k_ref[0, pl.ds(r, n), :].astype(F32)
        qd = (q_ref[0, pl.ds(r, n), :].astype(F32) * jnp.exp(b)).astype(BF16)
        qd_scr[d, pl.ds(r, n), :] = qd
        decayed.append((qd, (k * jnp.exp(-b)).astype(BF16), k * jnp.exp(b_last - b), jnp.exp(b_end)))

    scores = [[lax.dot_general(jnp.where(m, qd, jnp.zeros_like(qd)), kd, _NT, preferred_element_type=F32)
               for m in masks] for qd, kd, _, _ in decayed]
    for (d, rev, g, q_ref, k_ref, v_ref, la_ref), r, sc in zip(items, rows, scores):
        for h in range(GLA_HEADS):
            cols = slice(h * GLA_DV, (h + 1) * GLA_DV)
            probs = jnp.where(causal[rev], sc[h], 0.0).astype(BF16)
            oi_scr[d, pl.ds(r, n), cols] = _dot(probs, v_ref[0, pl.ds(r, n), cols])

    for (d, rev, g, q_ref, k_ref, v_ref, la_ref), r, (_, _, k_st, decay) in zip(items, rows, decayed):
        for c in range(CHUNKS_PER_GROUP):
            ci = g * CHUNKS_PER_GROUP + c
            k_st_t = k_st[c * CHUNK:(c + 1) * CHUNK].T.astype(BF16)
            for h in range(GLA_HEADS):
                v_ch = v_ref[0, pl.ds(r + c * CHUNK, CHUNK), h * GLA_DV:(h + 1) * GLA_DV]
                ds_scr[d, ci, h * GLA_DK:(h + 1) * GLA_DK, :] = _dot(k_st_t[h * GLA_DK:(h + 1) * GLA_DK], v_ch)
            dc_scr[d, ci] = jnp.broadcast_to(decay[c], (LANES, QK_W)).T


def _gla_kernel(qf_ref, kf_ref, vf_ref, laf_ref, qb_ref, kb_ref, vb_ref, lab_ref, s0f_ref, s0b_ref,
                of_ref, ob_ref, sf_out_ref, sb_out_ref, s_ref, qd_scr, oi_scr, ds_scr, dc_scr, st_scr, *, n_chunks):
    j = pl.program_id(1)

    @pl.when(j == 0)
    def _():
        s_ref[0] = s0f_ref[0]
        s_ref[1] = s0b_ref[0]

    n_groups = n_chunks // CHUNKS_PER_GROUP
    unroll = 2 if n_groups % 2 == 0 else 1

    def bulk(it, carry):
        items = []
        for u in range(unroll):
            g = it * unroll + u
            items.append((0, False, g, qf_ref, kf_ref, vf_ref, laf_ref))
            items.append((1, True, g, qb_ref, kb_ref, vb_ref, lab_ref))
        _gla_bulk(items, qd_scr, oi_scr, ds_scr, dc_scr)
        return carry
    lax.fori_loop(0, n_groups // unroll, bulk, 0)

    def recur(i, carry):
        for d, ci in ((0, i), (1, n_chunks - 1 - i)):
            s = s_ref[d]
            st_scr[d, ci] = s.astype(BF16)
            s_ref[d] = dc_scr[d, ci] * s + ds_scr[d, ci]
        return carry
    lax.fori_loop(0, n_chunks, recur, 0)

    masks = _head_masks(CHUNK)

    def inter(it, carry):
        work = [(d, o_ref, (it * unroll + u) * CHUNKS_PER_GROUP + c)
                for u in range(unroll) for d, o_ref in ((0, of_ref), (1, ob_ref)) for c in range(CHUNKS_PER_GROUP)]
        from_state = []
        for d, o_ref, ci in work:
            qd = qd_scr[d, pl.ds(pl.multiple_of(ci * CHUNK, CHUNK), CHUNK), :]
            lhs = jnp.concatenate([jnp.where(m, qd, jnp.zeros_like(qd)) for m in masks], axis=0)
            from_state.append(_dot(lhs, st_scr[d, ci]))
        for (d, o_ref, ci), fs in zip(work, from_state):
            r = pl.multiple_of(ci * CHUNK, CHUNK)
            for h in range(GLA_HEADS):
                cols = slice(h * GLA_DV, (h + 1) * GLA_DV)
                o_ref[0, pl.ds(r, CHUNK), cols] = (oi_scr[d, pl.ds(r, CHUNK), cols]
                                                   + fs[h * CHUNK:(h + 1) * CHUNK]).astype(BF16)
        return carry
    lax.fori_loop(0, n_groups // unroll, inter, 0)

    @pl.when(j == pl.num_programs(1) - 1)
    def _():
        sf_out_ref[0] = s_ref[0]
        sb_out_ref[0] = s_ref[1]


def _gla(q, k, v, la, s0f, s0b, tb):
    b, l, _ = q.shape
    assert tb % GLA_GROUP == 0 and l % tb == 0
    nb = l // tb
    n_chunks = tb // CHUNK
    fwd = lambda bi, j: (bi, j, 0)
    bwd = lambda bi, j: (bi, nb - 1 - j, 0)
    fwd_la = lambda bi, j: (bi, j, 0)
    bwd_la = lambda bi, j: (bi, nb - 1 - j, 1)
    st = lambda bi, j: (bi, 0, 0)
    state_shape = (b, QK_W, GLA_DV)
    return pl.pallas_call(
        functools.partial(_gla_kernel, n_chunks=tb // CHUNK),
        grid=(b, nb),
        in_specs=[pl.BlockSpec((1, tb, QK_W), fwd), pl.BlockSpec((1, tb, QK_W), fwd),
                  pl.BlockSpec((1, tb, V_W), fwd), pl.BlockSpec((1, tb, QK_W), fwd_la),
                  pl.BlockSpec((1, tb, QK_W), bwd), pl.BlockSpec((1, tb, QK_W), bwd),
                  pl.BlockSpec((1, tb, V_W), bwd), pl.BlockSpec((1, tb, QK_W), bwd_la),
                  pl.BlockSpec((1, QK_W, GLA_DV), st), pl.BlockSpec((1, QK_W, GLA_DV), st)],
        out_specs=[pl.BlockSpec((1, tb, V_W), fwd), pl.BlockSpec((1, tb, V_W), bwd),
                   pl.BlockSpec((1, QK_W, GLA_DV), st), pl.BlockSpec((1, QK_W, GLA_DV), st)],
        out_shape=[jax.ShapeDtypeStruct((b, l, V_W), BF16), jax.ShapeDtypeStruct((b, l, V_W), BF16),
                   jax.ShapeDtypeStruct(state_shape, F32), jax.ShapeDtypeStruct(state_shape, F32)],
        scratch_shapes=[pltpu.VMEM((2, QK_W, GLA_DV), F32),
                        pltpu.VMEM((2, tb, QK_W), BF16),
                        pltpu.VMEM((2, tb, V_W), F32),
                        pltpu.VMEM((2, n_chunks, QK_W, GLA_DV), F32),
                        pltpu.VMEM((2, n_chunks, QK_W, GLA_DV), F32),
                        pltpu.VMEM((2, n_chunks, QK_W, GLA_DV), BF16)],
        compiler_params=_params("arbitrary", "arbitrary"),
        name="gla",
    )(q, k, v, la, q, k, v, la, s0f, s0b)


POOL_BLK = 256
POOL_COL_STEP = 1024
POOL_ROW_STEP = 256
POOL_OUT_STEP = 1024


def _pool_body(w, first_batch, p_ref, wp_ref, ps_ref, o_ref, buf_ref, inv_ref, y_ref, n_tok):
    lo = w // 2
    hi = w - 1 - lo
    pad = lo * GRID_W
    n_rows = n_tok // GRID_W
    blk = POOL_BLK

    @pl.when(first_batch)
    def _():
        zeros = jnp.zeros((GRID_W * 8, LANES), F32)
        buf_ref[0:pad, :] = zeros[0:pad]
        buf_ref[pad + n_tok:pad + n_tok + hi * GRID_W + GRID_W, :] = zeros[0:hi * GRID_W + GRID_W]

        def inv_step(t, carry):
            r = pl.multiple_of(t * POOL_OUT_STEP, POOL_OUT_STEP)
            tok = r + lax.broadcasted_iota(jnp.int32, (POOL_OUT_STEP, LANES), 0)
            g_row = tok >> GRID_W_LOG2
            g_col = tok & (GRID_W - 1)
            cnt_c = jnp.minimum(g_col + hi, GRID_W - 1) - jnp.maximum(g_col - lo, 0) + 1
            cnt_r = jnp.minimum(g_row + hi, n_rows - 1) - jnp.maximum(g_row - lo, 0) + 1
            inv_ref[pl.ds(r, POOL_OUT_STEP), :] = 1.0 / (cnt_c * cnt_r).astype(F32)
            return carry
        lax.fori_loop(0, n_tok // POOL_OUT_STEP, inv_step, 0)

    i0 = lax.broadcasted_iota(jnp.int32, (blk, blk), 0)
    i1 = lax.broadcasted_iota(jnp.int32, (blk, blk), 1)
    same_row = (i0 >> GRID_W_LOG2) == (i1 >> GRID_W_LOG2)
    band = jnp.where(same_row & (i1 - i0 >= -lo) & (i1 - i0 <= hi), 1.0, 0.0).astype(BF16)

    def col_step(t, carry):
        rows = [pl.multiple_of(t * POOL_COL_STEP + u * blk, blk) for u in range(POOL_COL_STEP // blk)]
        sums = []
        for r in rows:
            sums.append(_dot(band, jnp.concatenate(_split_bf16(p_ref[0, pl.ds(r, blk), :]), axis=-1)))
        for r, s in zip(rows, sums):
            buf_ref[pl.ds(pad + r, blk), :] = s[:, :LANES] + s[:, LANES:]
        return carry
    lax.fori_loop(0, n_tok // POOL_COL_STEP, col_step, 0)

    def row_step(t, carry):
        r = pl.multiple_of(t * POOL_ROW_STEP, POOL_ROW_STEP)
        acc = buf_ref[pl.ds(r, POOL_ROW_STEP), :]
        for d in range(1, w):
            acc = acc + buf_ref[pl.ds(r + d * GRID_W, POOL_ROW_STEP), :]
        m = acc * inv_ref[pl.ds(r, POOL_ROW_STEP), :]
        y_ref[pl.ds(r, POOL_ROW_STEP), :] = (m - p_ref[0, pl.ds(r, POOL_ROW_STEP), :]).astype(BF16)
        return carry
    lax.fori_loop(0, n_tok // POOL_ROW_STEP, row_step, 0)

    wp = wp_ref[0].astype(BF16)
    scale = ps_ref[0]

    def out_step(t, carry):
        r = pl.multiple_of(t * POOL_OUT_STEP, POOL_OUT_STEP)
        o_ref[0, pl.ds(r, POOL_OUT_STEP), :] = (_dot(y_ref[pl.ds(r, POOL_OUT_STEP), :], wp) * scale).astype(BF16)
        return carry
    lax.fori_loop(0, n_tok // POOL_OUT_STEP, out_step, 0)


def _pool_kernel(p_ref, wp_ref, ps_ref, o_ref, buf_ref, inv_ref, y_ref, *, n_tok):
    g = pl.program_id(0)
    first_batch = pl.program_id(1) == 0
    for gi, w in enumerate(POOL_WINDOWS):
        @pl.when(g == gi)
        def _(w=w):
            _pool_body(w, first_batch, p_ref, wp_ref, ps_ref, o_ref, buf_ref, inv_ref, y_ref, n_tok)


def _pool(p, w_pool, pool_scale):
    b, l, _ = p.shape
    n_g = len(POOL_WINDOWS)
    max_w = max(POOL_WINDOWS)
    buf_rows = l + (max_w + 1) * GRID_W
    return pl.pallas_call(
        functools.partial(_pool_kernel, n_tok=l),
        grid=(n_g, b),
        in_specs=[pl.BlockSpec((1, l, POOL_GROUP), lambda g, bi: (bi, 0, g)),
                  pl.BlockSpec((1, POOL_GROUP, POOL_GROUP), lambda g, bi: (g, 0, 0)),
                  pl.BlockSpec((1, 1, POOL_GROUP), lambda g, bi: (g, 0, 0))],
        out_specs=pl.BlockSpec((1, l, POOL_GROUP), lambda g, bi: (bi, 0, g)),
        out_shape=jax.ShapeDtypeStruct((b, l, POOL_W), BF16),
        scratch_shapes=[pltpu.VMEM((buf_rows, LANES), F32), pltpu.VMEM((l, LANES), F32),
                        pltpu.VMEM((l, LANES), BF16)],
        compiler_params=_params("arbitrary", "arbitrary"),
        name="pool",
    )(p, w_pool, pool_scale.reshape(n_g, 1, POOL_GROUP))


ROUTER_ROWS = 48
ROUTER_GROUP_ROW0 = N_EXPERTS
TOP_K = 2
SORT_TM = 512
PIECE = 16
SLOT_ROWS = TOP_K * SORT_TM + N_EXPERTS * PIECE
INFO_W1, INFO_W2, INFO_DEST1, INFO_DEST2 = range(4)


def _outproj_kernel(of_ref, ob_ref, g_ref, pooled_ref, x_ref, gn_ref, wout_ref, gt_ref, sh_ref, sc_ref, g2_ref,
                    wr_hi_ref, wr_lo_ref, x1_ref, xs_ref, info_ref, cnt_ref):
    o = of_ref[0].astype(F32) + ob_ref[0].astype(F32)
    gate = _silu(g_ref[0].astype(F32))
    parts = []
    for h in range(GLA_HEADS):
        sl = slice(h * GLA_DV, (h + 1) * GLA_DV)
        parts.append((_rms(o[:, sl]) * gn_ref[...] * gate[:, sl]).astype(BF16))
    o_n = jnp.concatenate(parts, axis=-1)
    out = _dot(o_n, wout_ref[0:V_W, :]) + _dot(pooled_ref[0], wout_ref[V_W:V_W + POOL_W, :])
    x1 = x_ref[0] + gt_ref[0] * out
    x1_ref[0] = x1.astype(BF16)
    h2 = _rms(x1) * g2_ref[...]
    h2 = h2 * (1.0 + sc_ref[0]) + sh_ref[0]
    h2_bf = h2.astype(BF16)
    h2_lo = (h2 - h2_bf.astype(F32)).astype(BF16)
    logits = (lax.dot_general(wr_hi_ref[...], h2_bf, _NT, preferred_element_type=F32)
              + lax.dot_general(wr_hi_ref[...], h2_lo, _NT, preferred_element_type=F32)
              + lax.dot_general(wr_lo_ref[...], h2_bf, _NT, preferred_element_type=F32))
    tm = logits.shape[1]
    row = lax.broadcasted_iota(jnp.int32, logits.shape, 0)
    neg = -jnp.inf
    big = jnp.int32(1 << 20)
    is_group = (row >= ROUTER_GROUP_ROW0) & (row < ROUTER_GROUP_ROW0 + N_GROUPS)
    gl = jnp.where(is_group, logits, neg)
    g_max = jnp.max(gl, axis=0, keepdims=True)
    g_idx = jnp.min(jnp.where(gl == g_max, row - ROUTER_GROUP_ROW0, big), axis=0, keepdims=True)
    g_prob = 1.0 / jnp.sum(jnp.where(is_group, jnp.exp(gl - g_max), 0.0), axis=0, keepdims=True)
    n_e = EXPERTS_PER_GROUP
    el = logits[0:n_e]
    for g in range(1, N_GROUPS):
        el = jnp.where(g_idx == g, logits[g * n_e:(g + 1) * n_e], el)
    r8 = lax.broadcasted_iota(jnp.int32, el.shape, 0)
    m1 = jnp.max(el, axis=0, keepdims=True)
    i1 = jnp.min(jnp.where(el == m1, r8, big), axis=0, keepdims=True)
    el2 = jnp.where(r8 == i1, neg, el)
    m2 = jnp.max(el2, axis=0, keepdims=True)
    i2 = jnp.min(jnp.where(el2 == m2, r8, big), axis=0, keepdims=True)
    r = jnp.exp(m2 - m1)
    w1 = g_prob / (1.0 + r)
    w2 = g_prob * r / (1.0 + r)

    e1 = g_idx * n_e + i1
    e2 = g_idx * n_e + i2
    r_e = lax.broadcasted_iota(jnp.int32, (N_EXPERTS, tm), 0)
    pick1 = r_e == e1
    pick2 = r_e == e2
    onehot = jnp.where(pick1 | pick2, 1.0, 0.0)
    t_r = lax.broadcasted_iota(jnp.int32, (tm, tm), 0)
    t_c = lax.broadcasted_iota(jnp.int32, (tm, tm), 1)
    rank = _dot(onehot.astype(BF16), jnp.where(t_r < t_c, 1.0, 0.0).astype(BF16))
    count = jnp.sum(onehot, axis=1, keepdims=True)
    n_pieces = jnp.floor((count + (PIECE - 1)) * (1.0 / PIECE))
    e_r = lax.broadcasted_iota(jnp.int32, (N_EXPERTS, N_EXPERTS), 0)
    e_c = lax.broadcasted_iota(jnp.int32, (N_EXPERTS, N_EXPERTS), 1)
    run_start = _dot(jnp.where(e_c < e_r, 1.0, 0.0).astype(BF16),
                     jnp.broadcast_to(n_pieces, (N_EXPERTS, LANES)).astype(BF16))[:, 0:1] * PIECE
    slot = rank + run_start
    dest1 = jnp.sum(jnp.where(pick1, slot, 0.0), axis=0, keepdims=True)
    dest2 = jnp.sum(jnp.where(pick2, slot, 0.0), axis=0, keepdims=True)
    info_rows = [w1, w2, dest1, dest2]
    info_t = jnp.concatenate(info_rows + [jnp.zeros((LANES - len(info_rows), tm), F32)], axis=0)
    info_ref[0] = info_t.T
    cnt_ref[0] = jnp.broadcast_to(count, (N_EXPERTS, LANES)).astype(jnp.int32)
    s_id = lax.broadcasted_iota(jnp.int32, (SLOT_ROWS, tm), 0)
    perm_t = jnp.where((s_id == dest1.astype(jnp.int32)) | (s_id == dest2.astype(jnp.int32)),
                       1.0, 0.0).astype(BF16)
    xs_ref[0] = _dot(perm_t, h2_bf).astype(BF16)


def _outproj(o_f, o_b, g, pooled, x, gn, w_out, gt_a, sh_m, sc_m, g2, wr_hi, wr_lo, tm):
    b, l, d = x.shape
    per_batch = l // tm
    n_tiles = b * per_batch
    tok = lambda s: (s // per_batch, s % per_batch, 0)
    vec = lambda s: (s // per_batch, 0, 0)
    fixed = lambda s: (0, 0)
    return pl.pallas_call(
        _outproj_kernel,
        grid=(n_tiles,),
        in_specs=[pl.BlockSpec((1, tm, V_W), tok), pl.BlockSpec((1, tm, V_W), tok),
                  pl.BlockSpec((1, tm, V_W), tok), pl.BlockSpec((1, tm, POOL_W), tok),
                  pl.BlockSpec((1, tm, d), tok),
                  pl.BlockSpec((1, GLA_DV), fixed),
                  pl.BlockSpec(w_out.shape, fixed),
                  pl.BlockSpec((1, 1, d), vec), pl.BlockSpec((1, 1, d), vec), pl.BlockSpec((1, 1, d), vec),
                  pl.BlockSpec((1, d), fixed),
                  pl.BlockSpec(wr_hi.shape, fixed), pl.BlockSpec(wr_lo.shape, fixed)],
        out_specs=[pl.BlockSpec((1, tm, d), tok),
                   pl.BlockSpec((1, SLOT_ROWS, d), lambda s: (s, 0, 0)),
                   pl.BlockSpec((1, tm, LANES), tok),
                   pl.BlockSpec((1, N_EXPERTS, LANES), lambda s: (s, 0, 0))],
        out_shape=[jax.ShapeDtypeStruct((b, l, d), BF16),
                   jax.ShapeDtypeStruct((n_tiles, SLOT_ROWS, d), BF16),
                   jax.ShapeDtypeStruct((b, l, LANES), F32),
                   jax.ShapeDtypeStruct((n_tiles, N_EXPERTS, LANES), jnp.int32)],
        compiler_params=_params("arbitrary"),
        name="outproj_router",
    )(o_f, o_b, g, pooled, x, gn, w_out, gt_a, sh_m, sc_m, g2, wr_hi, wr_lo)


EXPERT_TM = 1024
PIECES_PER_STEP = EXPERT_TM // PIECE
GATHER_SLOTS = 6
SLOT_PIECES = SLOT_ROWS // PIECE


def _expert_steps(n_tok):
    pieces = TOP_K * n_tok // PIECE + (n_tok // SORT_TM) * N_EXPERTS
    return -(-pieces // PIECES_PER_STEP) + N_EXPERTS


def _experts_kernel(exp_ref, valid_ref, src_ref, xs_hbm, wi_ref, wo_ref, ys_ref, xbuf, wi_bf, wo_bf, sem):
    i = pl.program_id(0)
    n_steps = pl.num_programs(0)
    slot = i % GATHER_SLOTS

    def piece_copy(s, j, src_row):
        return pltpu.make_async_copy(xs_hbm.at[pl.ds(src_row, PIECE), :], xbuf.at[s, pl.ds(j * PIECE, PIECE), :],
                                     sem.at[s])

    def fetch(step):
        @pl.when((step < n_steps) & (valid_ref[jnp.minimum(step, n_steps - 1)] == 1))
        def _():
            for j in range(PIECES_PER_STEP):
                piece_copy(step % GATHER_SLOTS, j,
                           pl.multiple_of(src_ref[step * PIECES_PER_STEP + j], PIECE)).start(priority=j % 2)

    @pl.when(i == 0)
    def _():
        for ahead in range(GATHER_SLOTS - 1):
            fetch(jnp.int32(ahead))

    fetch(i + GATHER_SLOTS - 1)
    expert = exp_ref[i]

    @pl.when((i == 0) | (exp_ref[jnp.maximum(i - 1, 0)] != expert))
    def _():
        wi_bf[...] = wi_ref[0].astype(BF16)
        wo_bf[...] = wo_ref[0].astype(BF16)

    @pl.when(valid_ref[i] == 1)
    def _():
        for j in range(PIECES_PER_STEP):
            piece_copy(slot, j, 0).wait()
        au = _dot(xbuf[slot], wi_bf[...])
        hidden = (_silu(au[:, :D_EXPERT]) * au[:, D_EXPERT:]).astype(BF16)
        ys_ref[...] = _dot(hidden, wo_bf[...]).astype(BF16)

    @pl.when(valid_ref[i] == 0)
    def _():
        ys_ref[...] = jnp.zeros_like(ys_ref)


def _experts(xs, step_expert, step_valid, piece_src, wi, wo):
    n_steps = step_expert.shape[0]
    width = xs.shape[1]
    d = wo.shape[2]
    return pl.pallas_call(
        _experts_kernel,
        grid_spec=pltpu.PrefetchScalarGridSpec(
            num_scalar_prefetch=3,
            grid=(n_steps,),
            in_specs=[pl.BlockSpec(memory_space=pl.ANY),
                      pl.BlockSpec((1,) + wi.shape[1:], lambda i, e, v, s: (e[i], 0, 0)),
                      pl.BlockSpec((1,) + wo.shape[1:], lambda i, e, v, s: (e[i], 0, 0))],
            out_specs=pl.BlockSpec((EXPERT_TM, d), lambda i, e, v, s: (i, 0)),
            scratch_shapes=[pltpu.VMEM((GATHER_SLOTS, EXPERT_TM, width), BF16), pltpu.VMEM(wi.shape[1:], BF16),
                            pltpu.VMEM(wo.shape[1:], BF16), pltpu.SemaphoreType.DMA((GATHER_SLOTS,))]),
        out_shape=jax.ShapeDtypeStruct((n_steps * EXPERT_TM, d), BF16),
        compiler_params=_params("arbitrary"),
        name="moe_experts",
    )(step_expert, step_valid, piece_src, xs, wi, wo)


def _combine_kernel(src_ref, n_ref, ys_hbm, info_ref, x1_ref, gt_ref, gf_ref, o_ref, stage, sem):
    i = pl.program_id(0)
    n_steps = pl.num_programs(0)
    tm = SORT_TM

    def piece_copy(s, src_row, m):
        return pltpu.make_async_copy(ys_hbm.at[pl.ds(src_row, PIECE), :],
                                     stage.at[s, pl.ds(pl.multiple_of(m * PIECE, PIECE), PIECE), :], sem.at[s])

    def fetch(tile, s):
        def body(m, carry):
            piece_copy(s, pl.multiple_of(src_ref[tile * SLOT_PIECES + m], PIECE), m).start()
            return carry
        lax.fori_loop(0, n_ref[tile], body, 0)

    @pl.when(i == 0)
    def _():
        fetch(0, 0)

    @pl.when(i + 1 < n_steps)
    def _():
        fetch(i + 1, (i + 1) % 2)

    slot = i % 2
    n_tile = n_ref[i]

    def wait_body(m, carry):
        piece_copy(slot, 0, m).wait()
        return carry
    lax.fori_loop(0, n_tile, wait_body, 0)

    def zero_body(m, carry):
        stage[slot, pl.ds(pl.multiple_of(m * PIECE, PIECE), PIECE), :] = jnp.zeros((PIECE, stage.shape[2]), BF16)
        return carry
    lax.fori_loop(n_tile, SLOT_PIECES, zero_body, 0)

    ys = stage[slot]
    info = info_ref[...]
    lane = lax.broadcasted_iota(jnp.int32, info.shape, 1)
    column = lambda ln: jnp.sum(jnp.where(lane == ln, info, 0.0), axis=-1, keepdims=True)
    dest1 = column(INFO_DEST1).astype(jnp.int32)
    dest2 = column(INFO_DEST2).astype(jnp.int32)
    slot_id = lax.broadcasted_iota(jnp.int32, (tm, SLOT_ROWS), 1)
    gated = jnp.where(slot_id == dest1, column(INFO_W1),
                      jnp.where(slot_id == dest2, column(INFO_W2), 0.0)).astype(BF16)
    y = _dot(gated, ys)
    x2 = x1_ref[...].astype(F32) + gt_ref[0] * y
    o_ref[...] = _rms(x2) * gf_ref[...]


def _combine(ys, piece_src, tile_pieces, info, x1, gt_m, gf, tiles_per_batch):
    t, d = x1.shape
    n_tiles = t // SORT_TM
    tok = lambda i, *_: (i, 0)
    return pl.pallas_call(
        _combine_kernel,
        grid_spec=pltpu.PrefetchScalarGridSpec(
            num_scalar_prefetch=2,
            grid=(n_tiles,),
            in_specs=[pl.BlockSpec(memory_space=pl.ANY),
                      pl.BlockSpec((SORT_TM, LANES), tok),
                      pl.BlockSpec((SORT_TM, d), tok),
                      pl.BlockSpec((1, 1, d), lambda i, *_: (i // tiles_per_batch, 0, 0)),
                      pl.BlockSpec((1, d), lambda i, *_: (0, 0))],
            out_specs=pl.BlockSpec((SORT_TM, d), tok),
            scratch_shapes=[pltpu.VMEM((2, SLOT_ROWS, d), BF16), pltpu.SemaphoreType.DMA((2,))]),
        out_shape=jax.ShapeDtypeStruct((t, d), F32),
        compiler_params=_params("arbitrary"),
        name="moe_combine",
    )(piece_src, tile_pieces, ys, info, x1, gt_m, gf)


def _sparse_moe(xs, info, counts, x1, wi, wo, gt_m, gf):
    b, l, d = x1.shape
    t = b * l
    n_tiles = t // SORT_TM
    i32 = jnp.int32

    cnt = counts[:, :, 0]
    run_pieces = (cnt + (PIECE - 1)) // PIECE
    run_slot = (jnp.cumsum(run_pieces, axis=1) - run_pieces) * PIECE
    before = jnp.cumsum(run_pieces, axis=0) - run_pieces
    expert_pieces = jnp.sum(run_pieces, axis=0)
    steps_e = (expert_pieces + PIECES_PER_STEP - 1) // PIECES_PER_STEP
    step_end = jnp.cumsum(steps_e)
    region = (step_end - steps_e) * PIECES_PER_STEP
    run_pos = (region[None, :] + before) * PIECE

    n_steps = _expert_steps(t)
    step = jnp.arange(n_steps, dtype=i32)
    step_expert = jnp.minimum(jnp.sum((step[:, None] >= step_end[None, :]).astype(i32), axis=1), N_EXPERTS - 1)
    step_valid = (step < step_end[-1]).astype(i32)
    lookup = lambda onehot, table: jnp.dot(onehot, table.astype(F32), precision=lax.Precision.HIGHEST)
    piece = jnp.arange(n_steps * PIECES_PER_STEP, dtype=i32)
    is_expert = (jnp.repeat(step_expert, PIECES_PER_STEP)[:, None] == jnp.arange(N_EXPERTS, dtype=i32)).astype(F32)
    k_local = piece.astype(F32) - lookup(is_expert, region)
    ends = lookup(is_expert, jnp.cumsum(run_pieces, axis=0).T)
    tile_of = jnp.minimum(jnp.sum((ends <= k_local[:, None]).astype(i32), axis=1), n_tiles - 1)
    is_tile = (tile_of[:, None] == jnp.arange(n_tiles, dtype=i32)).astype(F32)
    j_in_run = k_local - jnp.sum(is_tile * lookup(is_expert, before.T), axis=1)
    slot_start = jnp.sum(is_tile * lookup(is_expert, run_slot.T), axis=1)
    real = (k_local < lookup(is_expert, expert_pieces)) & (jnp.repeat(step_valid, PIECES_PER_STEP) == 1)
    zero_piece = SLOT_ROWS - PIECE
    piece_src = jnp.where(real, tile_of * SLOT_ROWS + (slot_start + j_in_run * PIECE).astype(i32), zero_piece)


    m = jnp.arange(SLOT_PIECES, dtype=i32)
    run_end = jnp.cumsum(run_pieces, axis=1)
    run_of = jnp.minimum(jnp.sum((run_end[:, None, :] <= m[None, :, None]).astype(i32), axis=-1), N_EXPERTS - 1)
    is_run = run_of[:, :, None] == jnp.arange(N_EXPERTS, dtype=i32)
    pos_m = jnp.sum(jnp.where(is_run, run_pos[:, None, :], 0), axis=-1)
    first_m = jnp.sum(jnp.where(is_run, (run_end - run_pieces)[:, None, :], 0), axis=-1)
    back_src = jnp.where(m[None, :] < run_end[:, -1:], pos_m + (m[None, :] - first_m) * PIECE, 0)

    flat = lambda a: a.reshape(-1).astype(i32)
    ys = _experts(xs.reshape(n_tiles * SLOT_ROWS, d), step_expert, step_valid, flat(piece_src), wi, wo)
    out = _combine(ys, flat(back_src), flat(run_end[:, -1]), info.reshape(t, LANES), x1.reshape(t, d), gt_m, gf,
                   l // SORT_TM)
    return out.reshape(b, l, d)


def kernel(x, c, ctx, c_ctx, w_ada, b_ada, norm1_g, w_in, w_decay, b_decay, gla_norm_g, w_pool, pool_scale, w_out,
           norm2_g, w_router_group, w_router_expert, w_expert_in, w_expert_out, final_norm_g):
    assert w_ada.shape[0] == 1, "single-layer trunk"
    b, l, d = x.shape
    off_a = 2 * QK_W + 2 * V_W
    a_cols = 2 * GATE_RANK
    off_p = off_a + a_cols

    c8 = jnp.zeros((8, d), F32).at[:b].set(c).at[b].set(c_ctx)
    mod = _ada_mod(c8, w_ada[0], b_ada[0])
    sh_a, sc_a, gt_a, sh_m, sc_m, gt_m = [m.reshape(8, 1, d) for m in jnp.split(mod, 6, axis=-1)]
    ctx_rows = lambda m: jnp.broadcast_to(m[b:b + 1], (b, 1, d))

    w = w_in[0]
    w_r = jnp.concatenate([w[:, :off_a], w[:, off_p:off_p + POOL_W], w[:, off_a:off_p],
                           jnp.zeros((d, LANES - a_cols), F32)], axis=1).astype(BF16)
    wdec = jnp.zeros((LANES, 2 * QK_W), F32)
    wdec = wdec.at[:GATE_RANK, :QK_W].set(w_decay[0, 0]).at[GATE_RANK:a_cols, QK_W:].set(w_decay[0, 1]).astype(BF16)
    bdec = b_decay[0].reshape(1, 2 * QK_W)
    g1 = norm1_g[0].reshape(1, d)

    q_c, k_c, v_c, _, la_c, _ = _inproj(ctx, ctx_rows(sh_a), ctx_rows(sc_a), g1, w_r, wdec, bdec, tm=ctx.shape[1])
    zero_s = jnp.zeros((b, QK_W, GLA_DV), F32)
    _, _, s_f, s_b = _gla(q_c, k_c, v_c, la_c, zero_s, zero_s, tb=ctx.shape[1])

    q, k, v, g, la, p = _inproj(x, sh_a[:b], sc_a[:b], g1, w_r, wdec, bdec, tm=1024)
    o_f, o_b, _, _ = _gla(q, k, v, la, s_f, s_b, tb=1024)
    pooled = _pool(p, w_pool[0], pool_scale[0])

    w_router = jnp.zeros((ROUTER_ROWS, d), F32)
    w_router = w_router.at[:N_EXPERTS].set(w_router_expert[0].T)
    w_router = w_router.at[ROUTER_GROUP_ROW0:ROUTER_GROUP_ROW0 + N_GROUPS].set(w_router_group[0].T)
    wr_hi = w_router.astype(BF16)
    wr_lo = (w_router - wr_hi.astype(F32)).astype(BF16)
    x1, xs, info, counts = _outproj(o_f, o_b, g, pooled, x, gla_norm_g[0].reshape(1, GLA_DV), w_out[0].astype(BF16),
                                    gt_a[:b], sh_m[:b], sc_m[:b], norm2_g[0].reshape(1, d), wr_hi, wr_lo,
                                    tm=SORT_TM)

    return _sparse_moe(xs, info, counts, x1, w_expert_in[0], w_expert_out[0], gt_m[:b], final_norm_g.reshape(1, d))
```
